```python
import jax, jax.numpy as jnp
from jax import lax
import numpy as np

D_MODEL = 1024
BATCH = 8
SEQ = 4096
DEPTH = 2
DEC_BATCH = 8
DEC_SEQ = 32
PAST_LEN = 4096

CHUNK = 64
N_EVEN = (DEPTH + 1) // 2
N_ODD = DEPTH // 2
POOL_GROUPS = 4
POOL_WINDOWS = (2, 4, 8, 16)
POOL_WIDTH = D_MODEL // 2
POOL_GROUP_DIM = POOL_WIDTH // POOL_GROUPS
POOL_HIST = max(POOL_WINDOWS) - 1
ATT_WIDTH = D_MODEL // 2
ATT_HEADS = 8
ATT_HEAD_DIM = ATT_WIDTH // ATT_HEADS
LEFT_CHUNKS = 8
BAND = LEFT_CHUNKS * CHUNK
REL_CLIP = 128
ML_WIDTH = D_MODEL
ML_HEADS = 4
ML_HEAD_DIM = ML_WIDTH // ML_HEADS
EVEN_IN = 2 * POOL_WIDTH + 4 * ATT_WIDTH
ODD_IN = 5 * ML_WIDTH + 2 * ML_HEADS
RMS_EPS = 1e-6
LN_EPS = 1e-6

kernel_name = "chunk_pool_band_mlstm_stream"


def rmsnorm(x, g):
    xf = x.astype(jnp.float32)
    y = xf * lax.rsqrt(jnp.mean(xf * xf, axis=-1, keepdims=True) + RMS_EPS)
    return (y * g.astype(jnp.float32)).astype(x.dtype)


def rel_bias_lookup(table, rel):
    return table[:, jnp.clip(rel, -REL_CLIP, REL_CLIP) + REL_CLIP]


def multi_scale_pool(u_ext, n_hist, pos0, w_mix, scale):
    B, T_ext, _ = u_ext.shape
    L = T_ext - n_hist
    uf = u_ext.astype(jnp.float32).reshape(B, T_ext, POOL_GROUPS, POOL_GROUP_DIM)
    cs = jnp.concatenate([jnp.zeros_like(uf[:, :1]), jnp.cumsum(uf, axis=1)], axis=1)
    win = jnp.array(POOL_WINDOWS, jnp.int32)
    t = jnp.arange(L)
    hi = n_hist + 1 + t
    lo = jnp.maximum(hi[:, None] - win[None, :], 0)
    grp = jnp.arange(POOL_GROUPS)[None, :]
    win_sum = cs[:, hi] - cs[:, lo, grp]
    count = jnp.minimum(pos0 + t[:, None] + 1, win[None, :]).astype(jnp.float32)
    pooled = win_sum / count[None, :, :, None] - uf[:, n_hist:]
    mixed = jnp.einsum('blgc,gcd->blgd', pooled, w_mix.astype(jnp.float32))
    return (mixed.reshape(B, L, POOL_WIDTH) * scale.astype(jnp.float32)).astype(u_ext.dtype)


def _band_attention_one(q, k, v, table):
    T = q.shape[0]
    n_chunks = T // CHUNK
    band_len = (LEFT_CHUNKS + 1) * CHUNK
    pad = ((BAND, 0), (0, 0), (0, 0))
    kc = jnp.pad(k, pad).reshape(n_chunks + LEFT_CHUNKS, CHUNK, ATT_HEADS, ATT_HEAD_DIM)
    vc = jnp.pad(v, pad).reshape(n_chunks + LEFT_CHUNKS, CHUNK, ATT_HEADS, ATT_HEAD_DIM)
    idx = jnp.arange(n_chunks)[:, None] + jnp.arange(LEFT_CHUNKS + 1)[None, :]
    kb = kc[idx].reshape(n_chunks, band_len, ATT_HEADS, ATT_HEAD_DIM)
    vb = vc[idx].reshape(n_chunks, band_len, ATT_HEADS, ATT_HEAD_DIM)
    qc = q.reshape(n_chunks, CHUNK, ATT_HEADS, ATT_HEAD_DIM)
    s = jnp.einsum('cqhd,ckhd->hcqk', qc, kb).astype(jnp.float32) * (ATT_HEAD_DIM ** -0.5)
    qi = jnp.arange(CHUNK)
    kj = jnp.arange(band_len)
    bias = rel_bias_lookup(table, BAND + qi[:, None] - kj[None, :]).astype(jnp.float32)
    key_chunk = jnp.arange(n_chunks)[:, None] - LEFT_CHUNKS + kj[None, :] // CHUNK
    s = jnp.where((key_chunk >= 0)[None, :, None, :], s + bias[:, None], -jnp.inf)
    p = jax.nn.softmax(s, axis=-1)
    o = jnp.einsum('hcqk,ckhd->cqhd', p.astype(vb.dtype), vb)
    return o.reshape(T, ATT_HEADS, ATT_HEAD_DIM)


def band_attention_prompt(q, k, v, table):
    return lax.map(lambda qkv: _band_attention_one(qkv[0], qkv[1], qkv[2], table), (q, k, v))


def band_attention_sample(q, k_ext, v_ext, table):
    L = q.shape[1]
    W = k_ext.shape[1] - L
    rel = W + jnp.arange(L)[:, None] - jnp.arange(W + L)[None, :]
    bias = rel_bias_lookup(table, rel).astype(jnp.float32)
    s = jnp.einsum('bqhd,bkhd->bhqk', q, k_ext).astype(jnp.float32) * (ATT_HEAD_DIM ** -0.5) + bias[None]
    p = jax.nn.softmax(s, axis=-1)
    return jnp.einsum('bhqk,bkhd->bqhd', p.astype(v_ext.dtype), v_ext)


def even_mixer(h, w_in, w_mix, scale, table, w_out, pool_hist=None, k_hist=None, v_hist=None):
    B, L, _ = h.shape
    P, A = POOL_WIDTH, ATT_WIDTH
    proj = h @ w_in
    u, q, k, v, g_pool, g_att = jnp.split(proj, [P, P + A, P + 2 * A, P + 3 * A, 2 * P + 3 * A], axis=-1)
    q = q.reshape(B, L, ATT_HEADS, ATT_HEAD_DIM)
    k = k.reshape(B, L, ATT_HEADS, ATT_HEAD_DIM)
    v = v.reshape(B, L, ATT_HEADS, ATT_HEAD_DIM)
    if pool_hist is None:
        pool_out = multi_scale_pool(u, 0, 0, w_mix, scale)
        att = band_attention_prompt(q, k, v, table)
        keep = min(BAND, L)
        new_pool, new_k, new_v = u[:, L - POOL_HIST:], k[:, L - keep:], v[:, L - keep:]
    else:
        u_ext = jnp.concatenate([pool_hist.astype(u.dtype), u], axis=1)
        pool_out = multi_scale_pool(u_ext, POOL_HIST, PAST_LEN, w_mix, scale)
        k_ext = jnp.concatenate([k_hist.astype(k.dtype), k], axis=1)
        v_ext = jnp.concatenate([v_hist.astype(v.dtype), v], axis=1)
        att = band_attention_sample(q, k_ext, v_ext, table)
        new_pool, new_k, new_v = u_ext[:, L:], k, v
    mixed = jnp.concatenate([pool_out * jax.nn.silu(g_pool),
                             att.reshape(B, L, A) * jax.nn.silu(g_att)], axis=-1)
    return mixed @ w_out, new_pool, new_k, new_v


def mlstm_chunk(carry, xs):
    C, n, m = carry
    q, k, v, ig, lf = xs
    L = q.shape[2]
    b = jnp.cumsum(lf, axis=-1)
    causal = jnp.tril(jnp.ones((L, L), bool))
    dmat = jnp.where(causal, b[..., :, None] - b[..., None, :] + ig[..., None, :], -jnp.inf)
    inter = b + m[..., None]
    m_t = jnp.maximum(inter, jnp.max(dmat, axis=-1))
    a = jnp.exp(inter - m_t)
    s = jnp.einsum('bhtd,bhsd->bhts', q, k) * jnp.exp(dmat - m_t[..., None])
    num = a[..., None] * jnp.einsum('bhvk,bhtk->bhtv', C, q) + jnp.einsum('bhts,bhsv->bhtv', s, v)
    den = a * jnp.einsum('bhk,bhtk->bht', n, q) + jnp.sum(s, axis=-1)
    h = num / jnp.maximum(jnp.abs(den), jnp.exp(-m_t))[..., None]
    g = b[..., -1:] - b + ig
    m_new = jnp.maximum(b[..., -1] + m, jnp.max(g, axis=-1))
    decay = jnp.exp(b[..., -1] + m - m_new)
    wgt = jnp.exp(g - m_new[..., None])
    C_new = decay[..., None, None] * C + jnp.einsum('bhsv,bhsk->bhvk', v * wgt[..., None], k)
    n_new = decay[..., None] * n + jnp.einsum('bhs,bhsk->bhk', wgt, k)
    return (C_new, n_new, m_new), h


def odd_mixer(h, w_in, b_gate, gain, w_out, state=None):
    B, L, _ = h.shape
    W = ML_WIDTH
    proj = h @ w_in
    q, k, v, o, z, gates = jnp.split(proj, [W, 2 * W, 3 * W, 4 * W, 5 * W], axis=-1)

    def heads(t):
        return t.astype(jnp.float32).reshape(B, L, ML_HEADS, ML_HEAD_DIM).transpose(0, 2, 1, 3)

    q, k, v = heads(q), heads(k) * (ML_HEAD_DIM ** -0.5), heads(v)
    gates = gates.astype(jnp.float32) + b_gate.astype(jnp.float32)
    ig = gates[..., :ML_HEADS].transpose(0, 2, 1)
    lf = jax.nn.log_sigmoid(gates[..., ML_HEADS:]).transpose(0, 2, 1)
    if state is None:
        n_chunks = L // CHUNK

        def to_chunks(t):
            return jnp.moveaxis(t.reshape(t.shape[:2] + (n_chunks, CHUNK) + t.shape[3:]), 2, 0)

        init = (jnp.zeros((B, ML_HEADS, ML_HEAD_DIM, ML_HEAD_DIM), jnp.float32),
                jnp.zeros((B, ML_HEADS, ML_HEAD_DIM), jnp.float32),
                jnp.zeros((B, ML_HEADS), jnp.float32))
        xs = (to_chunks(q), to_chunks(k), to_chunks(v), to_chunks(ig), to_chunks(lf))
        (C, n, m), hc = lax.scan(mlstm_chunk, init, xs)
        hc = jnp.moveaxis(hc, 0, 2).reshape(B, ML_HEADS, L, ML_HEAD_DIM)
    else:
        C0, n0, m0 = state
        init = (C0.astype(jnp.float32), n0.astype(jnp.float32), m0.astype(jnp.float32))
        (C, n, m), hc = mlstm_chunk(init, (q, k, v, ig, lf))
    mu = jnp.mean(hc, axis=-1, keepdims=True)
    var = jnp.mean(jnp.square(hc - mu), axis=-1, keepdims=True)
    hn = (hc - mu) * lax.rsqrt(var + LN_EPS)
    hn = hn.transpose(0, 2, 1, 3).reshape(B, L, W) * gain.astype(jnp.float32)
    out = (hn * jax.nn.sigmoid(o.astype(jnp.float32)) * jax.nn.silu(z.astype(jnp.float32))).astype(h.dtype)
    return out @ w_out, C, n, m


def setup_inputs(seed: int = 0) -> dict:
    key = jax.random.key(seed)
    ks = jax.random.split(key, 20)
    nrm = jax.random.normal
    f32 = jnp.float32
    cache_rows = min(BAND, PAST_LEN)
    b_i = 0.1 * nrm(ks[16], (N_ODD, ML_HEADS), f32)
    b_f = jnp.linspace(3.0, 6.0, ML_HEADS, dtype=f32)[None, :] + 0.1 * nrm(ks[17], (N_ODD, ML_HEADS), f32)
    return {
        "x_prompt": nrm(ks[0], (BATCH, SEQ, D_MODEL), f32),
        "x_sample": nrm(ks[1], (DEC_BATCH, DEC_SEQ, D_MODEL), f32),
        "cache_pool": nrm(ks[2], (N_EVEN, DEC_BATCH, POOL_HIST, POOL_WIDTH), f32),
        "cache_k": nrm(ks[3], (N_EVEN, DEC_BATCH, cache_rows, ATT_HEADS, ATT_HEAD_DIM), f32),
        "cache_v": nrm(ks[4], (N_EVEN, DEC_BATCH, cache_rows, ATT_HEADS, ATT_HEAD_DIM), f32),
        "state_C": 0.05 * nrm(ks[5], (N_ODD, DEC_BATCH, ML_HEADS, ML_HEAD_DIM, ML_HEAD_DIM), f32),
        "state_n": 0.5 * jnp.abs(nrm(ks[6], (N_ODD, DEC_BATCH, ML_HEADS, ML_HEAD_DIM), f32)),
        "state_m": nrm(ks[7], (N_ODD, DEC_BATCH, ML_HEADS), f32),
        "norm_pre": 1.0 + 0.05 * nrm(ks[8], (DEPTH, D_MODEL), f32),
        "norm_post": 1.0 + 0.05 * nrm(ks[9], (DEPTH, D_MODEL), f32),
        "w_in_even": nrm(ks[10], (N_EVEN, D_MODEL, EVEN_IN), f32) * D_MODEL ** -0.5,
        "w_pool_mix": nrm(ks[11], (N_EVEN, POOL_GROUPS, POOL_GROUP_DIM, POOL_GROUP_DIM), f32) * POOL_GROUP_DIM ** -0.5,
        "pool_scale": 1.0 + 0.1 * nrm(ks[12], (N_EVEN, POOL_WIDTH), f32),
        "rel_bias": 0.5 * nrm(ks[13], (N_EVEN, ATT_HEADS, 2 * REL_CLIP + 1), f32),
        "w_out_even": nrm(ks[14], (N_EVEN, POOL_WIDTH + ATT_WIDTH, D_MODEL), f32) * (POOL_WIDTH + ATT_WIDTH) ** -0.5,
        "w_in_odd": nrm(ks[15], (N_ODD, D_MODEL, ODD_IN), f32) * D_MODEL ** -0.5,
        "b_gate_odd": jnp.concatenate([b_i, b_f], axis=-1),
        "mlstm_norm": 1.0 + 0.05 * nrm(ks[18], (N_ODD, ML_WIDTH), f32),
        "w_out_odd": nrm(ks[19], (N_ODD, ML_WIDTH, D_MODEL), f32) * ML_WIDTH ** -0.5,
    }


def reference(x_prompt, x_sample, cache_pool, cache_k, cache_v, state_C, state_n, state_m,
              norm_pre, norm_post, w_in_even, w_pool_mix, pool_scale, rel_bias, w_out_even,
              w_in_odd, b_gate_odd, mlstm_norm, w_out_odd):
    xp, xs = x_prompt, x_sample
    pool_p, k_p, v_p, C_p, n_p, m_p = [], [], [], [], [], []
    pool_s, k_s, v_s, C_s, n_s, m_s = [], [], [], [], [], []
    for layer in range(DEPTH):
        if layer % 2 == 0:
            e = layer // 2
            wts = (w_in_even[e], w_pool_mix[e], pool_scale[e], rel_bias[e], w_out_even[e])
            yp, pp, kp, vp = even_mixer(rmsnorm(xp, norm_pre[layer]), *wts)
            ys, ps, ks_, vs = even_mixer(rmsnorm(xs, norm_pre[layer]), *wts,
                                         pool_hist=cache_pool[e], k_hist=cache_k[e], v_hist=cache_v[e])
            pool_p.append(pp); k_p.append(kp); v_p.append(vp)
            pool_s.append(ps); k_s.append(ks_); v_s.append(vs)
        else:
            o = layer // 2
            wts = (w_in_odd[o], b_gate_odd[o], mlstm_norm[o], w_out_odd[o])
            yp, cp, np_, mp = odd_mixer(rmsnorm(xp, norm_pre[layer]), *wts)
            ys, cs_, ns, ms = odd_mixer(rmsnorm(xs, norm_pre[layer]), *wts,
                                        state=(state_C[o], state_n[o], state_m[o]))
            C_p.append(cp); n_p.append(np_); m_p.append(mp)
            C_s.append(cs_); n_s.append(ns); m_s.append(ms)
        xp = xp + rmsnorm(yp, norm_post[layer])
        xs = xs + rmsnorm(ys, norm_post[layer])
    return (xp, xs,
            jnp.stack(pool_p), jnp.stack(k_p), jnp.stack(v_p),
            jnp.stack(C_p), jnp.stack(n_p), jnp.stack(m_p),
            jnp.stack(pool_s), jnp.stack(k_s), jnp.stack(v_s),
            jnp.stack(C_s), jnp.stack(n_s), jnp.stack(m_s))
```

```python
import functools

import jax
import jax.numpy as jnp
from jax import lax
from jax.experimental import pallas as pl
from jax.experimental.pallas import tpu as pltpu

D_MODEL = 1024
PAST_LEN = 4096
CHUNK = 64
POOL_WINDOWS = (2, 4, 8, 16)
POOL_GROUPS = len(POOL_WINDOWS)
POOL_WIDTH = D_MODEL // 2
POOL_GROUP_DIM = POOL_WIDTH // POOL_GROUPS
POOL_HIST = max(POOL_WINDOWS) - 1
POOL_HIST_ROWS = 16
ATT_WIDTH = D_MODEL // 2
ATT_HEADS = 8
ATT_HEAD_DIM = ATT_WIDTH // ATT_HEADS
HEAD_PAIRS = ATT_HEADS // 2
BAND = 8 * CHUNK
REL_CLIP = 128
REL_TABLE = 2 * REL_CLIP + 1
REL_TABLE_PAD = 384
ML_WIDTH = D_MODEL
ML_HEADS = 4
ML_HEAD_DIM = ML_WIDTH // ML_HEADS
EVEN_IN = 2 * POOL_WIDTH + 4 * ATT_WIDTH
RMS_EPS = 1e-6
LN_EPS = 1e-6

LANES = 128
V7X_VMEM_LIMIT_BYTES = 56 * 1024 * 1024

BF16 = jnp.bfloat16
F32 = jnp.float32


def _dot(a, b):
    return jnp.dot(a, b, preferred_element_type=F32)


def _dot_nt(a, b):
    return lax.dot_general(a, b, (((1,), (1,)), ((), ())), preferred_element_type=F32)


def _dot_tn(a, b):
    return lax.dot_general(a, b, (((0,), (0,)), ((), ())), preferred_element_type=F32)


def _split3(x):
    hi = x.astype(BF16)
    r1 = x - hi.astype(F32)
    mid = r1.astype(BF16)
    lo = (r1 - mid.astype(F32)).astype(BF16)
    return hi, mid, lo


def _rmsnorm(x, g):
    return x * lax.rsqrt(jnp.mean(x * x, axis=-1, keepdims=True) + RMS_EPS) * g


def _silu(x):
    return x * jax.nn.sigmoid(x)


def _bias_kernel(tab_ref, out_ref, *, lc):
    kb = BAND + lc
    parts = _split3(tab_ref[...])
    kj = lax.broadcasted_iota(jnp.int32, (REL_TABLE_PAD, kb), 1)
    ti = lax.broadcasted_iota(jnp.int32, (REL_TABLE_PAD, kb), 0)

    def row(qi, carry):
        idx = jnp.clip(BAND + qi - kj, -REL_CLIP, REL_CLIP) + REL_CLIP
        onehot = jnp.where(ti == idx, 1.0, 0.0).astype(BF16)
        acc = _dot(parts[0], onehot) + _dot(parts[1], onehot) + _dot(parts[2], onehot)
        out_ref[qi] = acc
        return carry

    lax.fori_loop(0, lc, row, 0)


def _expand_bias(table, lc):
    kb = BAND + lc
    tab = jnp.pad(table.astype(F32), ((0, 0), (0, REL_TABLE_PAD - REL_TABLE)))
    out = pl.pallas_call(
        functools.partial(_bias_kernel, lc=lc),
        out_shape=jax.ShapeDtypeStruct((lc, ATT_HEADS, kb), F32),
        name="rel_bias_expand",
    )(tab)
    return out.transpose(1, 0, 2).reshape(HEAD_PAIRS, 2 * lc, kb)


def _even_kernel(*refs, tt, lc, has_hist, pos0):
    if has_hist:
        (x_ref, hu_ref, hk_ref, hv_ref, gpre_ref, gpost_ref, win_ref, wmix_ref, scale_ref,
         bias_ref, wout_ref, y_ref, pool_ref, kout_ref, vout_ref,
         uext, kext, vext, q_sc, gate_sc, mixed_sc) = refs
    else:
        (x_ref, gpre_ref, gpost_ref, win_ref, wmix_ref, scale_ref,
         bias_ref, wout_ref, y_ref, pool_ref, kout_ref, vout_ref,
         uext, kext, vext, q_sc, gate_sc, mixed_sc) = refs
    t = pl.program_id(1)
    hr = POOL_HIST_ROWS
    kb = BAND + lc
    n_chunks = tt // lc

    @pl.when(t == 0)
    def _():
        if has_hist:
            uext[0:hr, :] = hu_ref[0]
            kext[0:BAND, :] = hk_ref[0].astype(BF16)
            vext[0:BAND, :] = hv_ref[0].astype(BF16)
        else:
            uext[0:hr, :] = jnp.zeros((hr, POOL_WIDTH), F32)
            kext[0:BAND, :] = jnp.zeros((BAND, ATT_WIDTH), BF16)
            vext[0:BAND, :] = jnp.zeros((BAND, ATT_WIDTH), BF16)

    @pl.when(t > 0)
    def _():
        uext[0:hr, :] = uext[tt:tt + hr, :]
        kext[0:BAND, :] = kext[tt:tt + BAND, :]
        vext[0:BAND, :] = vext[tt:tt + BAND, :]

    x = x_ref[0]
    h = _rmsnorm(x, gpre_ref[...]).astype(BF16)
    p_, a_ = POOL_WIDTH, ATT_WIDTH
    uext[hr:hr + tt, :] = _dot(h, win_ref[:, 0:p_])
    q_sc[...] = (_dot(h, win_ref[:, p_:p_ + a_]) * (ATT_HEAD_DIM ** -0.5)).astype(BF16)
    k = _dot(h, win_ref[:, p_ + a_:p_ + 2 * a_])
    kext[BAND:BAND + tt, :] = k.astype(BF16)
    kout_ref[0] = k
    v = _dot(h, win_ref[:, p_ + 2 * a_:p_ + 3 * a_])
    vext[BAND:BAND + tt, :] = v.astype(BF16)
    vout_ref[0] = v
    gate_sc[...] = _silu(_dot(h, win_ref[:, p_ + 3 * a_:]))

    pos = pos0 + t * tt + lax.broadcasted_iota(jnp.int32, (tt, 1), 0)
    for g, w in enumerate(POOL_WINDOWS):
        ln = slice(g * POOL_GROUP_DIM, (g + 1) * POOL_GROUP_DIM)
        tok = uext[hr:hr + tt, ln]
        win_sum = tok
        for back in range(1, w):
            win_sum = win_sum + uext[hr - back:hr - back + tt, ln]
        count = jnp.minimum(pos + 1, w).astype(F32)
        pooled = win_sum / count - tok
        mixed = _dot(pooled.astype(BF16), wmix_ref[g]) * scale_ref[:, ln]
        mixed_sc[:, ln] = (mixed * gate_sc[:, ln]).astype(BF16)

    lane = lax.broadcasted_iota(jnp.int32, (lc, LANES), 1)
    even_head = lane < ATT_HEAD_DIM

    def chunk(c, carry):
        r0 = c * lc if isinstance(c, int) else pl.multiple_of(c * lc, lc)
        if not has_hist:
            kpos = t * tt + r0 - BAND + lax.broadcasted_iota(jnp.int32, (1, kb), 1)
            key_ok = kpos >= 0
        for p in range(HEAD_PAIRS):
            ln = slice(p * LANES, (p + 1) * LANES)
            qp = q_sc[pl.ds(r0, lc), ln]
            zero = jnp.zeros_like(qp)
            q2 = jnp.concatenate([jnp.where(even_head, qp, zero), jnp.where(even_head, zero, qp)], axis=0)
            s = _dot_nt(q2, kext[pl.ds(r0, kb), ln]) + bias_ref[p]
            if not has_hist:
                s = jnp.where(key_ok, s, -jnp.inf)
            e = jnp.exp(s - jnp.max(s, axis=-1, keepdims=True))
            denom = jnp.sum(e, axis=-1, keepdims=True)
            o2 = _dot(e.astype(BF16), vext[pl.ds(r0, kb), ln]) / denom
            o = jnp.where(even_head, o2[0:lc], o2[lc:2 * lc])
            mo = slice(POOL_WIDTH + p * LANES, POOL_WIDTH + (p + 1) * LANES)
            mixed_sc[pl.ds(r0, lc), mo] = (o * gate_sc[pl.ds(r0, lc), mo]).astype(BF16)
        return carry

    if n_chunks == 1:
        chunk(0, 0)
    else:
        lax.fori_loop(0, n_chunks, chunk, 0)

    y = _dot(mixed_sc[...], wout_ref[...])
    y_ref[0] = x_ref[0] + _rmsnorm(y, gpost_ref[...])
    pool_ref[0] = uext[tt:tt + hr, :]


def _even_layer(x, hist, gpre, gpost, w_in, w_mix, scale, bias, w_out, *, tt, lc, pos0):
    b, t_len, d = x.shape
    has_hist = hist is not None
    nt = t_len // tt
    assert t_len % tt == 0 and tt % lc == 0
    keep = min(BAND, t_len)
    assert keep == tt, "key/value cache rows must be exactly the last time tile"
    assert nt == 1 or tt >= BAND
    kb = BAND + lc

    def whole(shape):
        return pl.BlockSpec(shape, lambda i, j: (0,) * len(shape))

    def per_seq(shape):
        return pl.BlockSpec((1,) + shape, lambda i, j: (i, 0, 0))

    in_specs = [pl.BlockSpec((1, tt, d), lambda i, j: (i, j, 0))]
    args = [x]
    if has_hist:
        in_specs += [per_seq((POOL_HIST_ROWS, POOL_WIDTH)), per_seq((BAND, ATT_WIDTH)), per_seq((BAND, ATT_WIDTH))]
        args += list(hist)
    in_specs += [whole((1, d)), whole((1, d)), whole((d, EVEN_IN)),
                 whole((POOL_GROUPS, POOL_GROUP_DIM, POOL_GROUP_DIM)), whole((1, POOL_WIDTH)),
                 whole((HEAD_PAIRS, 2 * lc, kb)), whole((POOL_WIDTH + ATT_WIDTH, d))]
    args += [gpre, gpost, w_in, w_mix, scale, bias, w_out]
    out_shape = (jax.ShapeDtypeStruct((b, t_len, d), F32),
                 jax.ShapeDtypeStruct((b, POOL_HIST_ROWS, POOL_WIDTH), F32),
                 jax.ShapeDtypeStruct((b, keep, ATT_WIDTH), F32),
                 jax.ShapeDtypeStruct((b, keep, ATT_WIDTH), F32))
    out_specs = (pl.BlockSpec((1, tt, d), lambda i, j: (i, j, 0)),
                 per_seq((POOL_HIST_ROWS, POOL_WIDTH)), per_seq((keep, ATT_WIDTH)), per_seq((keep, ATT_WIDTH)))
    scratch = [pltpu.VMEM((POOL_HIST_ROWS + tt, POOL_WIDTH), F32),
               pltpu.VMEM((BAND + tt, ATT_WIDTH), BF16),
               pltpu.VMEM((BAND + tt, ATT_WIDTH), BF16),
               pltpu.VMEM((tt, ATT_WIDTH), BF16),
               pltpu.VMEM((tt, POOL_WIDTH + ATT_WIDTH), F32),
               pltpu.VMEM((tt, POOL_WIDTH + ATT_WIDTH), BF16)]
    return pl.pallas_call(
        functools.partial(_even_kernel, tt=tt, lc=lc, has_hist=has_hist, pos0=pos0),
        grid=(b, nt), in_specs=in_specs, out_specs=out_specs, out_shape=out_shape,
        scratch_shapes=scratch,
        compiler_params=pltpu.CompilerParams(dimension_semantics=("parallel", "arbitrary"),
                                             vmem_limit_bytes=V7X_VMEM_LIMIT_BYTES),
        name="even_layer_hist" if has_hist else "even_layer",
    )(*args)


def _odd_kernel(x_ref, c0_ref, n0_ref, m0_ref, gpre_ref, gpost_ref, wqkv_ref, woz_ref, wg_ref, bg_ref,
                gain_ref, wout_ref, y_ref, cout_ref, nout_ref, mout_ref,
                q_sc, k_sc, v_sc, oz_sc, mixed_sc, c_sc, n_sc, m_sc, bcol_sc, igcol_sc, brow_sc, igrow_sc,
                *, tt, lc):
    t = pl.program_id(1)
    n_chunks = tt // lc
    w_, dh = ML_WIDTH, ML_HEAD_DIM

    @pl.when(t == 0)
    def _():
        c_sc[...] = c0_ref[0]
        n_sc[...] = n0_ref[0]
        m_sc[...] = m0_ref[0]

    x = x_ref[0]
    h = _rmsnorm(x, gpre_ref[...]).astype(BF16)
    q_sc[...] = _dot(h, wqkv_ref[:, 0:w_]).astype(BF16)
    k_sc[...] = (_dot(h, wqkv_ref[:, w_:2 * w_]) * (dh ** -0.5)).astype(BF16)
    v_sc[...] = _dot(h, wqkv_ref[:, 2 * w_:3 * w_]).astype(BF16)
    oz_sc[...] = jax.nn.sigmoid(_dot(h, woz_ref[:, 0:w_])) * _silu(_dot(h, woz_ref[:, w_:2 * w_]))

    gates = _dot(h, wg_ref[...]) + bg_ref[...]
    lf = jax.nn.log_sigmoid(gates)
    ri = lax.broadcasted_iota(jnp.int32, (tt, tt), 0)
    ci = lax.broadcasted_iota(jnp.int32, (tt, tt), 1)
    tri = jnp.where(((ri // lc) == (ci // lc)) & (ci <= ri), 1.0, 0.0).astype(BF16)
    lf3 = _split3(lf)
    bcol = _dot(tri, lf3[0]) + _dot(tri, lf3[1]) + _dot(tri, lf3[2])
    bcol_sc[...] = bcol
    igcol_sc[...] = gates
    sel = jnp.where(lax.broadcasted_iota(jnp.int32, (8, LANES), 0)
                    == lax.broadcasted_iota(jnp.int32, (8, LANES), 1), 1.0, 0.0).astype(BF16)
    b3 = _split3(bcol)
    g3 = _split3(gates)
    brow = _dot_nt(sel, b3[0]) + _dot_nt(sel, b3[1]) + _dot_nt(sel, b3[2])
    igrow = _dot_nt(sel, g3[0]) + _dot_nt(sel, g3[1]) + _dot_nt(sel, g3[2])
    for c in range(n_chunks):
        brow_sc[c] = brow[:, c * lc:(c + 1) * lc]
        igrow_sc[c] = igrow[:, c * lc:(c + 1) * lc]

    causal = (lax.broadcasted_iota(jnp.int32, (lc, lc), 1) <= lax.broadcasted_iota(jnp.int32, (lc, lc), 0))
    nh = ML_HEADS

    def chunk(c, carry):
        r0 = c * lc if isinstance(c, int) else pl.multiple_of(c * lc, lc)
        rows = pl.ds(r0, lc)
        bcol_c = bcol_sc[rows, :]
        igcol_c = igcol_sc[rows, :]
        brow_c = brow_sc[c]
        igrow_c = igrow_sc[c]
        for hd in range(nh):
            ln = slice(hd * dh, (hd + 1) * dh)
            b_c = bcol_c[:, nh + hd:nh + hd + 1]
            ig_c = igcol_c[:, hd:hd + 1]
            b_r = brow_c[nh + hd:nh + hd + 1, :]
            ig_r = igrow_c[hd:hd + 1, :]
            m_prev = m_sc[hd][:, 0:1]
            b_last = b_c[lc - 1:lc, :]
            dmat = jnp.where(causal, b_c - b_r + ig_r, -jnp.inf)
            inter = b_c + m_prev
            m_t = jnp.maximum(inter, jnp.max(dmat, axis=-1, keepdims=True))
            a = jnp.exp(inter - m_t)
            q = q_sc[rows, ln]
            k = k_sc[rows, ln]
            v = v_sc[rows, ln]
            s = _dot_nt(q, k) * jnp.exp(dmat - m_t)
            c_old = c_sc[hd]
            n_old = n_sc[hd]
            num = a * _dot_nt(q, c_old.astype(BF16)) + _dot(s.astype(BF16), v)
            qn = jnp.sum(q.astype(F32) * n_old, axis=-1, keepdims=True)
            den = a * qn + jnp.sum(s, axis=-1, keepdims=True)
            hv = num / jnp.maximum(jnp.abs(den), jnp.exp(-m_t))
            mu = jnp.mean(hv, axis=-1, keepdims=True)
            dlt = hv - mu
            var = jnp.mean(dlt * dlt, axis=-1, keepdims=True)
            hn = dlt * lax.rsqrt(var + LN_EPS) * gain_ref[:, ln]
            mixed_sc[rows, ln] = (hn * oz_sc[rows, ln]).astype(BF16)
            g_c = b_last - b_c + ig_c
            m_new = jnp.maximum(b_last + m_prev, jnp.max(g_c, axis=0, keepdims=True))
            decay = jnp.exp(b_last + m_prev - m_new)
            wgt = jnp.exp(g_c - m_new)
            vw = (v.astype(F32) * wgt).astype(BF16)
            c_sc[hd] = decay * c_old + _dot_tn(vw, k)
            n_sc[hd] = decay * n_old + jnp.sum(wgt * k.astype(F32), axis=0, keepdims=True)
            m_sc[hd] = jnp.broadcast_to(m_new, (1, LANES))
        return carry

    if n_chunks == 1:
        chunk(0, 0)
    else:
        lax.fori_loop(0, n_chunks, chunk, 0)

    y = _dot(mixed_sc[...], wout_ref[...])
    y_ref[0] = x_ref[0] + _rmsnorm(y, gpost_ref[...])
    cout_ref[0] = c_sc[...]
    nout_ref[0] = n_sc[...]
    mout_ref[0] = m_sc[...]


def _odd_layer(x, state, gpre, gpost, w_qkv, w_oz, w_g, b_g, gain, w_out, *, tt, lc):
    b, t_len, d = x.shape
    nt = t_len // tt
    assert t_len % tt == 0 and tt % lc == 0
    nh, dh = ML_HEADS, ML_HEAD_DIM
    n_chunks = tt // lc

    def whole(shape):
        return pl.BlockSpec(shape, lambda i, j: (0,) * len(shape))

    def per_seq(shape):
        return pl.BlockSpec((1,) + shape, lambda i, j: (i,) + (0,) * len(shape))

    st_specs = [per_seq((nh, dh, dh)), per_seq((nh, 1, dh)), per_seq((nh, 1, LANES))]
    in_specs = ([pl.BlockSpec((1, tt, d), lambda i, j: (i, j, 0))] + st_specs +
                [whole((1, d)), whole((1, d)), whole((d, 3 * ML_WIDTH)), whole((d, 2 * ML_WIDTH)),
                 whole((d, LANES)), whole((1, LANES)), whole((1, ML_WIDTH)), whole((ML_WIDTH, d))])
    out_shape = (jax.ShapeDtypeStruct((b, t_len, d), F32),
                 jax.ShapeDtypeStruct((b, nh, dh, dh), F32),
                 jax.ShapeDtypeStruct((b, nh, 1, dh), F32),
                 jax.ShapeDtypeStruct((b, nh, 1, LANES), F32))
    out_specs = (pl.BlockSpec((1, tt, d), lambda i, j: (i, j, 0)), *st_specs)
    scratch = [pltpu.VMEM((tt, ML_WIDTH), BF16), pltpu.VMEM((tt, ML_WIDTH), BF16), pltpu.VMEM((tt, ML_WIDTH), BF16),
               pltpu.VMEM((tt, ML_WIDTH), F32), pltpu.VMEM((tt, ML_WIDTH), BF16),
               pltpu.VMEM((nh, dh, dh), F32), pltpu.VMEM((nh, 1, dh), F32), pltpu.VMEM((nh, 1, LANES), F32),
               pltpu.VMEM((tt, LANES), F32), pltpu.VMEM((tt, LANES), F32),
               pltpu.VMEM((n_chunks, 8, lc), F32), pltpu.VMEM((n_chunks, 8, lc), F32)]
    return pl.pallas_call(
        functools.partial(_odd_kernel, tt=tt, lc=lc),
        grid=(b, nt), in_specs=in_specs, out_specs=out_specs, out_shape=out_shape,
        scratch_shapes=scratch,
        compiler_params=pltpu.CompilerParams(dimension_semantics=("parallel", "arbitrary"),
                                             vmem_limit_bytes=V7X_VMEM_LIMIT_BYTES),
        name="odd_layer",
    )(x, *state, gpre, gpost, w_qkv, w_oz, w_g, b_g, gain, w_out)


PROMPT_TILE = 512


def kernel(x_prompt, x_sample, cache_pool, cache_k, cache_v, state_C, state_n, state_m, norm_pre, norm_post,
           w_in_even, w_pool_mix, pool_scale, rel_bias, w_out_even, w_in_odd, b_gate_odd, mlstm_norm, w_out_odd):
    depth = norm_pre.shape[0]
    bp, dec_seq = x_prompt.shape[0], x_sample.shape[1]
    bs = x_sample.shape[0]
    nh, dh = ML_HEADS, ML_HEAD_DIM
    xp, xs = x_prompt, x_sample
    outs = {name: [] for name in ("pool_p", "k_p", "v_p", "C_p", "n_p", "m_p",
                                  "pool_s", "k_s", "v_s", "C_s", "n_s", "m_s")}
    for layer in range(depth):
        gpre = norm_pre[layer][None, :]
        gpost = norm_post[layer][None, :]
        if layer % 2 == 0:
            e = layer // 2
            w_in = w_in_even[e].astype(BF16)
            w_mix = w_pool_mix[e].astype(BF16)
            w_out = w_out_even[e].astype(BF16)
            scale = pool_scale[e][None, :]
            bias_p = _expand_bias(rel_bias[e], CHUNK)
            bias_s = _expand_bias(rel_bias[e], dec_seq)
            xp, pp, kp, vp = _even_layer(xp, None, gpre, gpost, w_in, w_mix, scale, bias_p, w_out,
                                         tt=PROMPT_TILE, lc=CHUNK, pos0=0)
            hist = (jnp.pad(cache_pool[e], ((0, 0), (POOL_HIST_ROWS - POOL_HIST, 0), (0, 0))),
                    cache_k[e].reshape(bs, BAND, ATT_WIDTH), cache_v[e].reshape(bs, BAND, ATT_WIDTH))
            xs, ps, ks, vs = _even_layer(xs, hist, gpre, gpost, w_in, w_mix, scale, bias_s, w_out,
                                         tt=dec_seq, lc=dec_seq, pos0=PAST_LEN)
            outs["pool_p"].append(pp[:, POOL_HIST_ROWS - POOL_HIST:])
            outs["k_p"].append(kp.reshape(bp, -1, ATT_HEADS, ATT_HEAD_DIM))
            outs["v_p"].append(vp.reshape(bp, -1, ATT_HEADS, ATT_HEAD_DIM))
            outs["pool_s"].append(ps[:, POOL_HIST_ROWS - POOL_HIST:])
            outs["k_s"].append(ks.reshape(bs, -1, ATT_HEADS, ATT_HEAD_DIM))
            outs["v_s"].append(vs.reshape(bs, -1, ATT_HEADS, ATT_HEAD_DIM))
        else:
            o = layer // 2
            w = w_in_odd[o]
            w_qkv = w[:, :3 * ML_WIDTH].astype(BF16)
            w_oz = w[:, 3 * ML_WIDTH:5 * ML_WIDTH].astype(BF16)
            w_g = jnp.pad(w[:, 5 * ML_WIDTH:], ((0, 0), (0, LANES - 2 * nh))).astype(BF16)
            b_g = jnp.pad(b_gate_odd[o], (0, LANES - 2 * nh))[None, :]
            gain = mlstm_norm[o][None, :]
            w_out = w_out_odd[o].astype(BF16)
            zero_state = (jnp.zeros((bp, nh, dh, dh), F32), jnp.zeros((bp, nh, 1, dh), F32),
                          jnp.zeros((bp, nh, 1, LANES), F32))
            xp, cp, np_, mp = _odd_layer(xp, zero_state, gpre, gpost, w_qkv, w_oz, w_g, b_g, gain, w_out,
                                         tt=PROMPT_TILE, lc=CHUNK)
            state = (state_C[o], state_n[o][:, :, None, :],
                     jnp.broadcast_to(state_m[o][:, :, None, None], (bs, nh, 1, LANES)))
            xs, cs, ns, ms = _odd_layer(xs, state, gpre, gpost, w_qkv, w_oz, w_g, b_g, gain, w_out,
                                        tt=dec_seq, lc=dec_seq)
            outs["C_p"].append(cp)
            outs["n_p"].append(np_[:, :, 0, :])
            outs["m_p"].append(mp[:, :, 0, 0])
            outs["C_s"].append(cs)
            outs["n_s"].append(ns[:, :, 0, :])
            outs["m_s"].append(ms[:, :, 0, 0])
    return (xp, xs,
            jnp.stack(outs["pool_p"]), jnp.stack(outs["k_p"]), jnp.stack(outs["v_p"]),
            jnp.stack(outs["C_p"]), jnp.stack(outs["n_p"]), jnp.stack(outs["m_p"]),
            jnp.stack(outs["pool_s"]), jnp.stack(outs["k_s"]), jnp.stack(outs["v_s"]),
            jnp.stack(outs["C_s"]), jnp.stack(outs["n_s"]), jnp.stack(outs["m_s"]))
```

```python
import functools

import jax
import jax.numpy as jnp
from jax import lax
from jax.experimental import pallas as pl
from jax.experimental.pallas import tpu as pltpu

D_MODEL = 1024
PAST_LEN = 4096
CHUNK = 64
POOL_WINDOWS = (2, 4, 8, 16)
POOL_GROUPS = len(POOL_WINDOWS)
POOL_WIDTH = D_MODEL // 2
POOL_GROUP_DIM = POOL_WIDTH // POOL_GROUPS
POOL_HIST = max(POOL_WINDOWS) - 1
POOL_HIST_ROWS = 16
ATT_WIDTH = D_MODEL // 2
ATT_HEADS = 8
ATT_HEAD_DIM = ATT_WIDTH // ATT_HEADS
HEAD_PAIRS = ATT_HEADS // 2
BAND = 8 * CHUNK
REL_CLIP = 128
REL_TABLE = 2 * REL_CLIP + 1
REL_TABLE_PAD = 384
ML_WIDTH = D_MODEL
ML_HEADS = 4
ML_HEAD_DIM = ML_WIDTH // ML_HEADS
EVEN_IN = 2 * POOL_WIDTH + 4 * ATT_WIDTH
RMS_EPS = 1e-6
LN_EPS = 1e-6

LANES = 128
V7X_VMEM_LIMIT_BYTES = 56 * 1024 * 1024

BF16 = jnp.bfloat16
F32 = jnp.float32


def _dot(a, b):
    return jnp.dot(a, b, preferred_element_type=F32)


def _dot_nt(a, b):
    return lax.dot_general(a, b, (((1,), (1,)), ((), ())), preferred_element_type=F32)


def _dot_tn(a, b):
    return lax.dot_general(a, b, (((0,), (0,)), ((), ())), preferred_element_type=F32)


def _split3(x):
    hi = x.astype(BF16)
    r1 = x - hi.astype(F32)
    mid = r1.astype(BF16)
    lo = (r1 - mid.astype(F32)).astype(BF16)
    return hi, mid, lo


def _rmsnorm(x, g):
    return x * lax.rsqrt(jnp.mean(x * x, axis=-1, keepdims=True) + RMS_EPS) * g


def _silu(x):
    return x * jax.nn.sigmoid(x)


def _bias_kernel(tab_ref, out_ref, *, lc):
    kb = BAND + lc
    parts = _split3(tab_ref[...])
    kj = lax.broadcasted_iota(jnp.int32, (REL_TABLE_PAD, kb), 1)
    ti = lax.broadcasted_iota(jnp.int32, (REL_TABLE_PAD, kb), 0)

    def row(qi, carry):
        idx = jnp.clip(BAND + qi - kj, -REL_CLIP, REL_CLIP) + REL_CLIP
        onehot = jnp.where(ti == idx, 1.0, 0.0).astype(BF16)
        acc = _dot(parts[0], onehot) + _dot(parts[1], onehot) + _dot(parts[2], onehot)
        out_ref[qi] = acc
        return carry

    lax.fori_loop(0, lc, row, 0)


def _expand_bias(table, lc):
    kb = BAND + lc
    tab = jnp.pad(table.astype(F32), ((0, 0), (0, REL_TABLE_PAD - REL_TABLE)))
    out = pl.pallas_call(
        functools.partial(_bias_kernel, lc=lc),
        out_shape=jax.ShapeDtypeStruct((lc, ATT_HEADS, kb), F32),
        name="rel_bias_expand",
    )(tab)
    return out.transpose(1, 0, 2).reshape(HEAD_PAIRS, 2 * lc, kb)


def _even_kernel(*refs, tt, lc, has_hist, pos0):
    if has_hist:
        (x_ref, hu_ref, hk_ref, hv_ref, gpre_ref, gpost_ref, win_ref, wmix_ref, scale_ref,
         bias_ref, wout_ref, y_ref, pool_ref, kout_ref, vout_ref,
         uext, kext, vext, q_sc, gate_sc, mixed_sc) = refs
    else:
        (x_ref, gpre_ref, gpost_ref, win_ref, wmix_ref, scale_ref,
         bias_ref, wout_ref, y_ref, pool_ref, kout_ref, vout_ref,
         uext, kext, vext, q_sc, gate_sc, mixed_sc) = refs
    t = pl.program_id(1)
    hr = POOL_HIST_ROWS
    kb = BAND + lc
    n_chunks = tt // lc

    @pl.when(t == 0)
    def _():
        if has_hist:
            uext[0:hr, :] = hu_ref[0]
            kext[0:BAND, :] = hk_ref[0].astype(BF16)
            vext[0:BAND, :] = hv_ref[0].astype(BF16)
        else:
            uext[0:hr, :] = jnp.zeros((hr, POOL_WIDTH), F32)
            kext[0:BAND, :] = jnp.zeros((BAND, ATT_WIDTH), BF16)
            vext[0:BAND, :] = jnp.zeros((BAND, ATT_WIDTH), BF16)

    @pl.when(t > 0)
    def _():
        uext[0:hr, :] = uext[tt:tt + hr, :]
        kext[0:BAND, :] = kext[tt:tt + BAND, :]
        vext[0:BAND, :] = vext[tt:tt + BAND, :]

    x = x_ref[0]
    h = _rmsnorm(x, gpre_ref[...]).astype(BF16)
    p_, a_ = POOL_WIDTH, ATT_WIDTH
    uext[hr:hr + tt, :] = _dot(h, win_ref[:, 0:p_])
    q_sc[...] = (_dot(h, win_ref[:, p_:p_ + a_]) * (ATT_HEAD_DIM ** -0.5)).astype(BF16)
    k = _dot(h, win_ref[:, p_ + a_:p_ + 2 * a_])
    kext[BAND:BAND + tt, :] = k.astype(BF16)
    kout_ref[0] = k
    v = _dot(h, win_ref[:, p_ + 2 * a_:p_ + 3 * a_])
    vext[BAND:BAND + tt, :] = v.astype(BF16)
    vout_ref[0] = v
    gate_sc[...] = _silu(_dot(h, win_ref[:, p_ + 3 * a_:]))

    pos = pos0 + t * tt + lax.broadcasted_iota(jnp.int32, (tt, 1), 0)
    for g, w in enumerate(POOL_WINDOWS):
        ln = slice(g * POOL_GROUP_DIM, (g + 1) * POOL_GROUP_DIM)
        tok = uext[hr:hr + tt, ln]
        win_sum = tok
        for back in range(1, w):
            win_sum = win_sum + uext[hr - back:hr - back + tt, ln]
        count = jnp.minimum(pos + 1, w).astype(F32)
        pooled = win_sum / count - tok
        mixed = _dot(pooled.astype(BF16), wmix_ref[g]) * scale_ref[:, ln]
        mixed_sc[:, ln] = (mixed * gate_sc[:, ln]).astype(BF16)

    lane = lax.broadcasted_iota(jnp.int32, (lc, LANES), 1)
    even_head = lane < ATT_HEAD_DIM

    def chunk(c, carry):
        r0 = c * lc if isinstance(c, int) else pl.multiple_of(c * lc, lc)
        if not has_hist:
            kpos = t * tt + r0 - BAND + lax.broadcasted_iota(jnp.int32, (1, kb), 1)
            key_ok = kpos >= 0
        for p in range(HEAD_PAIRS):
            ln = slice(p * LANES, (p + 1) * LANES)
            qp = q_sc[pl.ds(r0, lc), ln]
            zero = jnp.zeros_like(qp)
            q2 = jnp.concatenate([jnp.where(even_head, qp, zero), jnp.where(even_head, zero, qp)], axis=0)
            s = _dot_nt(q2, kext[pl.ds(r0, kb), ln]) + bias_ref[p]
            if not has_hist:
                s = jnp.where(key_ok, s, -jnp.inf)
            e = jnp.exp(s - jnp.max(s, axis=-1, keepdims=True))
            denom = jnp.sum(e, axis=-1, keepdims=True)
            o2 = _dot(e.astype(BF16), vext[pl.ds(r0, kb), ln]) / denom
            o = jnp.where(even_head, o2[0:lc], o2[lc:2 * lc])
            mo = slice(POOL_WIDTH + p * LANES, POOL_WIDTH + (p + 1) * LANES)
            mixed_sc[pl.ds(r0, lc), mo] = (o * gate_sc[pl.ds(r0, lc), mo]).astype(BF16)
        return carry

    if n_chunks == 1:
        chunk(0, 0)
    else:
        lax.fori_loop(0, n_chunks, chunk, 0)

    y = _dot(mixed_sc[...], wout_ref[...])
    y_ref[0] = x_ref[0] + _rmsnorm(y, gpost_ref[...])
    pool_ref[0] = uext[tt:tt + hr, :]


def _even_layer(x, hist, gpre, gpost, w_in, w_mix, scale, bias, w_out, *, tt, lc, pos0):
    b, t_len, d = x.shape
    has_hist = hist is not None
    nt = t_len // tt
    assert t_len % tt == 0 and tt % lc == 0
    keep = min(BAND, t_len)
    assert keep == tt, "key/value cache rows must be exactly the last time tile"
    assert nt == 1 or tt >= BAND
    kb = BAND + lc

    def whole(shape):
        return pl.BlockSpec(shape, lambda i, j: (0,) * len(shape))

    def per_seq(shape):
        return pl.BlockSpec((1,) + shape, lambda i, j: (i, 0, 0))

    in_specs = [pl.BlockSpec((1, tt, d), lambda i, j: (i, j, 0))]
    args = [x]
    if has_hist:
        in_specs += [per_seq((POOL_HIST_ROWS, POOL_WIDTH)), per_seq((BAND, ATT_WIDTH)), per_seq((BAND, ATT_WIDTH))]
        args += list(hist)
    in_specs += [whole((1, d)), whole((1, d)), whole((d, EVEN_IN)),
                 whole((POOL_GROUPS, POOL_GROUP_DIM, POOL_GROUP_DIM)), whole((1, POOL_WIDTH)),
                 whole((HEAD_PAIRS, 2 * lc, kb)), whole((POOL_WIDTH + ATT_WIDTH, d))]
    args += [gpre, gpost, w_in, w_mix, scale, bias, w_out]
    out_shape = (jax.ShapeDtypeStruct((b, t_len, d), F32),
                 jax.ShapeDtypeStruct((b, POOL_HIST_ROWS, POOL_WIDTH), F32),
                 jax.ShapeDtypeStruct((b, keep, ATT_WIDTH), F32),
                 jax.ShapeDtypeStruct((b, keep, ATT_WIDTH), F32))
    out_specs = (pl.BlockSpec((1, tt, d), lambda i, j: (i, j, 0)),
                 per_seq((POOL_HIST_ROWS, POOL_WIDTH)), per_seq((keep, ATT_WIDTH)), per_seq((keep, ATT_WIDTH)))
    scratch = [pltpu.VMEM((POOL_HIST_ROWS + tt, POOL_WIDTH), F32),
               pltpu.VMEM((BAND + tt, ATT_WIDTH), BF16),
               pltpu.VMEM((BAND + tt, ATT_WIDTH), BF16),
               pltpu.VMEM((tt, ATT_WIDTH), BF16),
               pltpu.VMEM((tt, POOL_WIDTH + ATT_WIDTH), F32),
               pltpu.VMEM((tt, POOL_WIDTH + ATT_WIDTH), BF16)]
    return pl.pallas_call(
        functools.partial(_even_kernel, tt=tt, lc=lc, has_hist=has_hist, pos0=pos0),
        grid=(b, nt), in_specs=in_specs, out_specs=out_specs, out_shape=out_shape,
        scratch_shapes=scratch,
        compiler_params=pltpu.CompilerParams(dimension_semantics=("parallel", "arbitrary"),
                                             vmem_limit_bytes=V7X_VMEM_LIMIT_BYTES),
        name="even_layer_hist" if has_hist else "even_layer",
    )(*args)


def _odd_kernel(x_ref, c0_ref, n0_ref, m0_ref, gpre_ref, gpost_ref, wqkv_ref, woz_ref, wg_ref, bg_ref,
                gain_ref, wout_ref, y_ref, cout_ref, nout_ref, mout_ref,
                q_sc, k_sc, v_sc, oz_sc, mixed_sc, c_sc, n_sc, m_sc, *, tt, lc):
    t = pl.program_id(1)
    n_chunks = tt // lc
    w_, dh, nh = ML_WIDTH, ML_HEAD_DIM, ML_HEADS

    @pl.when(t == 0)
    def _():
        c_sc[...] = c0_ref[0]
        n_sc[...] = n0_ref[0]
        m_sc[...] = m0_ref[0]

    x = x_ref[0]
    h = _rmsnorm(x, gpre_ref[...]).astype(BF16)
    q_sc[...] = _dot(h, wqkv_ref[:, 0:w_]).astype(BF16)
    k_sc[...] = (_dot(h, wqkv_ref[:, w_:2 * w_]) * (dh ** -0.5)).astype(BF16)
    v_sc[...] = _dot(h, wqkv_ref[:, 2 * w_:3 * w_]).astype(BF16)
    oz_sc[...] = jax.nn.sigmoid(_dot(h, woz_ref[:, 0:w_])) * _silu(_dot(h, woz_ref[:, w_:2 * w_]))

    gates = _dot(h, wg_ref[...]) + bg_ref[...]
    ig = gates[:, 0:LANES]
    lf = jax.nn.log_sigmoid(gates[:, LANES:2 * LANES])
    ri = lax.broadcasted_iota(jnp.int32, (tt, tt), 0)
    ci = lax.broadcasted_iota(jnp.int32, (tt, tt), 1)
    tri = jnp.where(((ri // lc) == (ci // lc)) & (ci <= ri), 1.0, 0.0).astype(BF16)
    lf3 = _split3(lf)
    bcol = _dot(tri, lf3[0]) + _dot(tri, lf3[1]) + _dot(tri, lf3[2])
    sel = jnp.where(lax.broadcasted_iota(jnp.int32, (8, LANES), 0)
                    == lax.broadcasted_iota(jnp.int32, (8, LANES), 1), 1.0, 0.0).astype(BF16)
    b3 = _split3(bcol)
    g3 = _split3(ig)
    brow = _dot_nt(sel, b3[0]) + _dot_nt(sel, b3[1]) + _dot_nt(sel, b3[2])
    igrow = _dot_nt(sel, g3[0]) + _dot_nt(sel, g3[1]) + _dot_nt(sel, g3[2])

    causal = (lax.broadcasted_iota(jnp.int32, (lc, lc), 1) <= lax.broadcasted_iota(jnp.int32, (lc, lc), 0))
    m_prev_all = m_sc[...]
    for c in range(n_chunks):
        rows = slice(c * lc, (c + 1) * lc)
        b_all = bcol[rows]
        b_last_all = b_all[lc - 1:lc]
        g_all = b_last_all - b_all + ig[rows]
        m_new_all = jnp.maximum(b_last_all + m_prev_all, jnp.max(g_all, axis=0, keepdims=True))
        decay_all = jnp.exp(b_last_all + m_prev_all - m_new_all)
        wgt_all = jnp.exp(g_all - m_new_all)
        inter_all = b_all + m_prev_all
        for hd in range(nh):
            ln = slice(hd * dh, (hd + 1) * dh)
            b_c = b_all[:, hd:hd + 1]
            b_r = brow[hd:hd + 1, rows]
            ig_r = igrow[hd:hd + 1, rows]
            inter = inter_all[:, hd:hd + 1]
            dmat = jnp.where(causal, b_c - b_r + ig_r, -jnp.inf)
            m_t = jnp.maximum(inter, jnp.max(dmat, axis=-1, keepdims=True))
            a = jnp.exp(inter - m_t)
            q = q_sc[rows, ln]
            k = k_sc[rows, ln]
            v = v_sc[rows, ln]
            s = _dot_nt(q, k) * jnp.exp(dmat - m_t)
            c_old = c_sc[hd]
            n_old = n_sc[hd]
            num = a * _dot_nt(q, c_old.astype(BF16)) + _dot(s.astype(BF16), v)
            qn = jnp.sum(q.astype(F32) * n_old, axis=-1, keepdims=True)
            den = a * qn + jnp.sum(s, axis=-1, keepdims=True)
            hv = num / jnp.maximum(jnp.abs(den), jnp.exp(-m_t))
            mu = jnp.mean(hv, axis=-1, keepdims=True)
            dlt = hv - mu
            var = jnp.mean(dlt * dlt, axis=-1, keepdims=True)
            hn = dlt * lax.rsqrt(var + LN_EPS) * gain_ref[:, ln]
            mixed_sc[rows, ln] = (hn * oz_sc[rows, ln]).astype(BF16)
            decay = decay_all[:, hd:hd + 1]
            wgt = wgt_all[:, hd:hd + 1]
            vw = (v.astype(F32) * wgt).astype(BF16)
            c_sc[hd] = decay * c_old + _dot_tn(vw, k)
            n_sc[hd] = decay * n_old + jnp.sum(wgt * k.astype(F32), axis=0, keepdims=True)
        m_prev_all = m_new_all
    m_sc[...] = m_prev_all

    y = _dot(mixed_sc[...], wout_ref[...])
    y_ref[0] = x_ref[0] + _rmsnorm(y, gpost_ref[...])
    cout_ref[0] = c_sc[...]
    nout_ref[0] = n_sc[...]
    mout_ref[0] = m_sc[...]


def _odd_layer(x, state, gpre, gpost, w_qkv, w_oz, w_g, b_g, gain, w_out, *, tt, lc):
    b, t_len, d = x.shape
    nt = t_len // tt
    assert t_len % tt == 0 and tt % lc == 0
    nh, dh = ML_HEADS, ML_HEAD_DIM

    def whole(shape):
        return pl.BlockSpec(shape, lambda i, j: (0,) * len(shape))

    def per_seq(shape):
        return pl.BlockSpec((1,) + shape, lambda i, j: (i,) + (0,) * len(shape))

    st_specs = [per_seq((nh, dh, dh)), per_seq((nh, 1, dh)), per_seq((1, LANES))]
    in_specs = ([pl.BlockSpec((1, tt, d), lambda i, j: (i, j, 0))] + st_specs +
                [whole((1, d)), whole((1, d)), whole((d, 3 * ML_WIDTH)), whole((d, 2 * ML_WIDTH)),
                 whole((d, 2 * LANES)), whole((1, 2 * LANES)), whole((1, ML_WIDTH)), whole((ML_WIDTH, d))])
    out_shape = (jax.ShapeDtypeStruct((b, t_len, d), F32),
                 jax.ShapeDtypeStruct((b, nh, dh, dh), F32),
                 jax.ShapeDtypeStruct((b, nh, 1, dh), F32),
                 jax.ShapeDtypeStruct((b, 1, LANES), F32))
    out_specs = (pl.BlockSpec((1, tt, d), lambda i, j: (i, j, 0)), *st_specs)
    scratch = [pltpu.VMEM((tt, ML_WIDTH), BF16), pltpu.VMEM((tt, ML_WIDTH), BF16), pltpu.VMEM((tt, ML_WIDTH), BF16),
               pltpu.VMEM((tt, ML_WIDTH), F32), pltpu.VMEM((tt, ML_WIDTH), BF16),
               pltpu.VMEM((nh, dh, dh), F32), pltpu.VMEM((nh, 1, dh), F32), pltpu.VMEM((1, LANES), F32)]
    return pl.pallas_call(
        functools.partial(_odd_kernel, tt=tt, lc=lc),
        grid=(b, nt), in_specs=in_specs, out_specs=out_specs, out_shape=out_shape,
        scratch_shapes=scratch,
        compiler_params=pltpu.CompilerParams(dimension_semantics=("parallel", "arbitrary"),
                                             vmem_limit_bytes=V7X_VMEM_LIMIT_BYTES),
        name="odd_layer",
    )(x, *state, gpre, gpost, w_qkv, w_oz, w_g, b_g, gain, w_out)


PROMPT_TILE = 512
ML_PROMPT_CHUNK = 256


def kernel(x_prompt, x_sample, cache_pool, cache_k, cache_v, state_C, state_n, state_m, norm_pre, norm_post,
           w_in_even, w_pool_mix, pool_scale, rel_bias, w_out_even, w_in_odd, b_gate_odd, mlstm_norm, w_out_odd):
    depth = norm_pre.shape[0]
    bp, dec_seq = x_prompt.shape[0], x_sample.shape[1]
    bs = x_sample.shape[0]
    nh, dh = ML_HEADS, ML_HEAD_DIM
    xp, xs = x_prompt, x_sample
    outs = {name: [] for name in ("pool_p", "k_p", "v_p", "C_p", "n_p", "m_p",
                                  "pool_s", "k_s", "v_s", "C_s", "n_s", "m_s")}
    for layer in range(depth):
        gpre = norm_pre[layer][None, :]
        gpost = norm_post[layer][None, :]
        if layer % 2 == 0:
            e = layer // 2
            w_in = w_in_even[e].astype(BF16)
            w_mix = w_pool_mix[e].astype(BF16)
            w_out = w_out_even[e].astype(BF16)
            scale = pool_scale[e][None, :]
            bias_p = _expand_bias(rel_bias[e], CHUNK)
            bias_s = _expand_bias(rel_bias[e], dec_seq)
            xp, pp, kp, vp = _even_layer(xp, None, gpre, gpost, w_in, w_mix, scale, bias_p, w_out,
                                         tt=PROMPT_TILE, lc=CHUNK, pos0=0)
            hist = (jnp.pad(cache_pool[e], ((0, 0), (POOL_HIST_ROWS - POOL_HIST, 0), (0, 0))),
                    cache_k[e].reshape(bs, BAND, ATT_WIDTH), cache_v[e].reshape(bs, BAND, ATT_WIDTH))
            xs, ps, ks, vs = _even_layer(xs, hist, gpre, gpost, w_in, w_mix, scale, bias_s, w_out,
                                         tt=dec_seq, lc=dec_seq, pos0=PAST_LEN)
            outs["pool_p"].append(pp[:, POOL_HIST_ROWS - POOL_HIST:])
            outs["k_p"].append(kp.reshape(bp, -1, ATT_HEADS, ATT_HEAD_DIM))
            outs["v_p"].append(vp.reshape(bp, -1, ATT_HEADS, ATT_HEAD_DIM))
            outs["pool_s"].append(ps[:, POOL_HIST_ROWS - POOL_HIST:])
            outs["k_s"].append(ks.reshape(bs, -1, ATT_HEADS, ATT_HEAD_DIM))
            outs["v_s"].append(vs.reshape(bs, -1, ATT_HEADS, ATT_HEAD_DIM))
        else:
            o = layer // 2
            w = w_in_odd[o]
            w_qkv = w[:, :3 * ML_WIDTH].astype(BF16)
            w_oz = w[:, 3 * ML_WIDTH:5 * ML_WIDTH].astype(BF16)
            pad_h = ((0, 0), (0, LANES - nh))
            w_g = jnp.concatenate([jnp.pad(w[:, 5 * ML_WIDTH:5 * ML_WIDTH + nh], pad_h),
                                   jnp.pad(w[:, 5 * ML_WIDTH + nh:], pad_h)], axis=1).astype(BF16)
            b_g = jnp.concatenate([jnp.pad(b_gate_odd[o][:nh], (0, LANES - nh)),
                                   jnp.pad(b_gate_odd[o][nh:], (0, LANES - nh))])[None, :]
            gain = mlstm_norm[o][None, :]
            w_out = w_out_odd[o].astype(BF16)
            zero_state = (jnp.zeros((bp, nh, dh, dh), F32), jnp.zeros((bp, nh, 1, dh), F32),
                          jnp.zeros((bp, 1, LANES), F32))
            xp, cp, np_, mp = _odd_layer(xp, zero_state, gpre, gpost, w_qkv, w_oz, w_g, b_g, gain, w_out,
                                         tt=PROMPT_TILE, lc=ML_PROMPT_CHUNK)
            state = (state_C[o], state_n[o][:, :, None, :],
                     jnp.pad(state_m[o], ((0, 0), (0, LANES - nh)))[:, None, :])
            xs, cs, ns, ms = _odd_layer(xs, state, gpre, gpost, w_qkv, w_oz, w_g, b_g, gain, w_out,
                                        tt=dec_seq, lc=dec_seq)
            outs["C_p"].append(cp)
            outs["n_p"].append(np_[:, :, 0, :])
            outs["m_p"].append(mp[:, 0, :nh])
            outs["C_s"].append(cs)
            outs["n_s"].append(ns[:, :, 0, :])
            outs["m_s"].append(ms[:, 0, :nh])
    return (xp, xs,
            jnp.stack(outs["pool_p"]), jnp.stack(outs["k_p"]), jnp.stack(outs["v_p"]),
            jnp.stack(outs["C_p"]), jnp.stack(outs["n_p"]), jnp.stack(outs["m_p"]),
            jnp.stack(outs["pool_s"]), jnp.stack(outs["k_s"]), jnp.stack(outs["v_s"]),
            jnp.stack(outs["C_s"]), jnp.stack(outs["n_s"]), jnp.stack(outs["m_s"]))
```

```python
import functools

import jax
import jax.numpy as jnp
from jax import lax
from jax.experimental import pallas as pl
from jax.experimental.pallas import tpu as pltpu

D_MODEL = 1024
PAST_LEN = 4096
CHUNK = 64
POOL_WINDOWS = (2, 4, 8, 16)
POOL_GROUPS = len(POOL_WINDOWS)
POOL_WIDTH = D_MODEL // 2
POOL_GROUP_DIM = POOL_WIDTH // POOL_GROUPS
POOL_HIST = max(POOL_WINDOWS) - 1
POOL_HIST_ROWS = 16
ATT_WIDTH = D_MODEL // 2
ATT_HEADS = 8
ATT_HEAD_DIM = ATT_WIDTH // ATT_HEADS
HEAD_PAIRS = ATT_HEADS // 2
BAND = 8 * CHUNK
REL_CLIP = 128
REL_TABLE = 2 * REL_CLIP + 1
REL_TABLE_PAD = 384
ML_WIDTH = D_MODEL
ML_HEADS = 4
ML_HEAD_DIM = ML_WIDTH // ML_HEADS
EVEN_IN = 2 * POOL_WIDTH + 4 * ATT_WIDTH
RMS_EPS = 1e-6
LN_EPS = 1e-6

LANES = 128
V7X_VMEM_LIMIT_BYTES = 56 * 1024 * 1024

BF16 = jnp.bfloat16
F32 = jnp.float32


def _dot(a, b):
    return jnp.dot(a, b, preferred_element_type=F32)


def _dot_nt(a, b):
    return lax.dot_general(a, b, (((1,), (1,)), ((), ())), preferred_element_type=F32)


def _dot_tn(a, b):
    return lax.dot_general(a, b, (((0,), (0,)), ((), ())), preferred_element_type=F32)


def _split3(x):
    hi = x.astype(BF16)
    r1 = x - hi.astype(F32)
    mid = r1.astype(BF16)
    lo = (r1 - mid.astype(F32)).astype(BF16)
    return hi, mid, lo


def _rmsnorm(x, g):
    return x * lax.rsqrt(jnp.mean(x * x, axis=-1, keepdims=True) + RMS_EPS) * g


def _silu(x):
    return x * jax.nn.sigmoid(x)


def _bias_kernel(tab_ref, out_ref, *, lc, nb):
    rb = nb * lc
    kb = BAND + rb
    parts = _split3(tab_ref[...])
    kj = lax.broadcasted_iota(jnp.int32, (REL_TABLE_PAD, kb), 1)
    ti = lax.broadcasted_iota(jnp.int32, (REL_TABLE_PAD, kb), 0)
    key_chunk = lax.broadcasted_iota(jnp.int32, (ATT_HEADS, kb), 1) // lc

    def row(qi, carry):
        idx = jnp.clip(BAND + qi - kj, -REL_CLIP, REL_CLIP) + REL_CLIP
        onehot = jnp.where(ti == idx, 1.0, 0.0).astype(BF16)
        acc = _dot(parts[0], onehot) + _dot(parts[1], onehot) + _dot(parts[2], onehot)
        q_chunk = qi // lc
        in_band = (key_chunk >= q_chunk) & (key_chunk <= q_chunk + BAND // lc)
        out_ref[qi] = jnp.where(in_band, acc, -jnp.inf)
        return carry

    lax.fori_loop(0, rb, row, 0)


def _expand_bias(table, lc, nb):
    rb = nb * lc
    kb = BAND + rb
    tab = jnp.pad(table.astype(F32), ((0, 0), (0, REL_TABLE_PAD - REL_TABLE)))
    out = pl.pallas_call(
        functools.partial(_bias_kernel, lc=lc, nb=nb),
        out_shape=jax.ShapeDtypeStruct((rb, ATT_HEADS, kb), F32),
        name="rel_bias_expand",
    )(tab)
    return out.transpose(1, 0, 2).reshape(HEAD_PAIRS, 2 * rb, kb)


def _even_kernel(*refs, tt, rb, has_hist, pos0):
    if has_hist:
        (x_ref, hu_ref, hk_ref, hv_ref, gpre_ref, gpost_ref, win_ref, wmix_ref, scale_ref,
         bias_ref, wout_ref, y_ref, pool_ref, kout_ref, vout_ref,
         uext, kext, vext, q_sc, gate_sc, mixed_sc, s_sc) = refs
    else:
        (x_ref, gpre_ref, gpost_ref, win_ref, wmix_ref, scale_ref,
         bias_ref, wout_ref, y_ref, pool_ref, kout_ref, vout_ref,
         uext, kext, vext, q_sc, gate_sc, mixed_sc, s_sc) = refs
    t = pl.program_id(1)
    hr = POOL_HIST_ROWS
    kb = BAND + rb

    @pl.when(t == 0)
    def _():
        if has_hist:
            uext[0:hr, :] = hu_ref[0]
            kext[0:BAND, :] = hk_ref[0].astype(BF16)
            vext[0:BAND, :] = hv_ref[0].astype(BF16)
        else:
            uext[0:hr, :] = jnp.zeros((hr, POOL_WIDTH), F32)
            kext[0:BAND, :] = jnp.zeros((BAND, ATT_WIDTH), BF16)
            vext[0:BAND, :] = jnp.zeros((BAND, ATT_WIDTH), BF16)

    @pl.when(t > 0)
    def _():
        uext[0:hr, :] = uext[tt:tt + hr, :]
        kext[0:BAND, :] = kext[tt:tt + BAND, :]
        vext[0:BAND, :] = vext[tt:tt + BAND, :]

    x = x_ref[0]
    h = _rmsnorm(x, gpre_ref[...]).astype(BF16)
    p_, a_ = POOL_WIDTH, ATT_WIDTH
    uext[hr:hr + tt, :] = _dot(h, win_ref[:, 0:p_])
    q_sc[...] = (_dot(h, win_ref[:, p_:p_ + a_]) * (ATT_HEAD_DIM ** -0.5)).astype(BF16)
    k = _dot(h, win_ref[:, p_ + a_:p_ + 2 * a_])
    kext[BAND:BAND + tt, :] = k.astype(BF16)
    kout_ref[0] = k
    v = _dot(h, win_ref[:, p_ + 2 * a_:p_ + 3 * a_])
    vext[BAND:BAND + tt, :] = v.astype(BF16)
    vout_ref[0] = v
    gate_sc[...] = _silu(_dot(h, win_ref[:, p_ + 3 * a_:]))

    pos = pos0 + t * tt + lax.broadcasted_iota(jnp.int32, (tt, 1), 0)
    for g, w in enumerate(POOL_WINDOWS):
        ln = slice(g * POOL_GROUP_DIM, (g + 1) * POOL_GROUP_DIM)
        tok = uext[hr:hr + tt, ln]
        win_sum = tok
        for back in range(1, w):
            win_sum = win_sum + uext[hr - back:hr - back + tt, ln]
        count = jnp.minimum(pos + 1, w).astype(F32)
        pooled = win_sum / count - tok
        mixed = _dot(pooled.astype(BF16), wmix_ref[g]) * scale_ref[:, ln]
        mixed_sc[:, ln] = (mixed * gate_sc[:, ln]).astype(BF16)

    lane = lax.broadcasted_iota(jnp.int32, (rb, LANES), 1)
    even_head = lane < ATT_HEAD_DIM

    def attend(sequence_start):
        units = [(blk, p) for blk in range(tt // rb) for p in range(HEAD_PAIRS)]

        def band(blk):
            r0 = blk * rb
            k0 = BAND if sequence_start else r0
            c0 = BAND - r0 if sequence_start else 0
            return r0, k0, c0, kb - c0

        def stage_scores(i):
            blk, p = units[i]
            r0, k0, c0, kw = band(blk)
            ln = slice(p * LANES, (p + 1) * LANES)
            qp = q_sc[r0:r0 + rb, ln]
            zero = jnp.zeros_like(qp)
            q2 = jnp.concatenate([jnp.where(even_head, qp, zero), jnp.where(even_head, zero, qp)], axis=0)
            s_sc[i % 2, :, 0:kw] = _dot_nt(q2, kext[k0:k0 + kw, ln]) + bias_ref[p, :, c0:c0 + kw]

        def finish(i):
            blk, p = units[i]
            r0, k0, c0, kw = band(blk)
            ln = slice(p * LANES, (p + 1) * LANES)
            row_max = jnp.max(s_sc[i % 2, :, 0:kw], axis=-1, keepdims=True)
            e = jnp.exp(s_sc[i % 2, :, 0:kw] - row_max)
            denom = jnp.sum(e, axis=-1, keepdims=True)
            o2 = _dot(e.astype(BF16), vext[k0:k0 + kw, ln]) / denom
            o = jnp.where(even_head, o2[0:rb], o2[rb:2 * rb])
            mo = slice(POOL_WIDTH + p * LANES, POOL_WIDTH + (p + 1) * LANES)
            mixed_sc[r0:r0 + rb, mo] = (o * gate_sc[r0:r0 + rb, mo]).astype(BF16)

        stage_scores(0)
        for i in range(len(units)):
            if i + 1 < len(units):
                stage_scores(i + 1)
            finish(i)

    if has_hist:
        attend(False)
    else:
        pl.when(t == 0)(functools.partial(attend, True))
        pl.when(t > 0)(functools.partial(attend, False))

    y = _dot(mixed_sc[...], wout_ref[...])
    y_ref[0] = x_ref[0] + _rmsnorm(y, gpost_ref[...])
    pool_ref[0] = uext[tt:tt + hr, :]


def _even_layer(x, hist, gpre, gpost, w_in, w_mix, scale, bias, w_out, *, tt, rb, pos0):
    b, t_len, d = x.shape
    has_hist = hist is not None
    nt = t_len // tt
    assert t_len % tt == 0 and tt % rb == 0
    assert has_hist or rb % LANES == 0, "band start at a sequence start must stay lane aligned in the bias"
    keep = min(BAND, t_len)
    assert keep == tt, "key/value cache rows must be exactly the last time tile"
    assert nt == 1 or tt >= BAND
    kb = BAND + rb

    def whole(shape):
        return pl.BlockSpec(shape, lambda i, j: (0,) * len(shape))

    def per_seq(shape):
        return pl.BlockSpec((1,) + shape, lambda i, j: (i, 0, 0))

    in_specs = [pl.BlockSpec((1, tt, d), lambda i, j: (i, j, 0))]
    args = [x]
    if has_hist:
        in_specs += [per_seq((POOL_HIST_ROWS, POOL_WIDTH)), per_seq((BAND, ATT_WIDTH)), per_seq((BAND, ATT_WIDTH))]
        args += list(hist)
    in_specs += [whole((1, d)), whole((1, d)), whole((d, EVEN_IN)),
                 whole((POOL_GROUPS, POOL_GROUP_DIM, POOL_GROUP_DIM)), whole((1, POOL_WIDTH)),
                 whole((HEAD_PAIRS, 2 * rb, kb)), whole((POOL_WIDTH + ATT_WIDTH, d))]
    args += [gpre, gpost, w_in, w_mix, scale, bias, w_out]
    out_shape = (jax.ShapeDtypeStruct((b, t_len, d), F32),
                 jax.ShapeDtypeStruct((b, POOL_HIST_ROWS, POOL_WIDTH), F32),
                 jax.ShapeDtypeStruct((b, keep, ATT_WIDTH), F32),
                 jax.ShapeDtypeStruct((b, keep, ATT_WIDTH), F32))
    out_specs = (pl.BlockSpec((1, tt, d), lambda i, j: (i, j, 0)),
                 per_seq((POOL_HIST_ROWS, POOL_WIDTH)), per_seq((keep, ATT_WIDTH)), per_seq((keep, ATT_WIDTH)))
    scratch = [pltpu.VMEM((POOL_HIST_ROWS + tt, POOL_WIDTH), F32),
               pltpu.VMEM((BAND + tt, ATT_WIDTH), BF16),
               pltpu.VMEM((BAND + tt, ATT_WIDTH), BF16),
               pltpu.VMEM((tt, ATT_WIDTH), BF16),
               pltpu.VMEM((tt, POOL_WIDTH + ATT_WIDTH), F32),
               pltpu.VMEM((tt, POOL_WIDTH + ATT_WIDTH), BF16),
               pltpu.VMEM((2, 2 * rb, kb), F32)]
    return pl.pallas_call(
        functools.partial(_even_kernel, tt=tt, rb=rb, has_hist=has_hist, pos0=pos0),
        grid=(b, nt), in_specs=in_specs, out_specs=out_specs, out_shape=out_shape,
        scratch_shapes=scratch,
        compiler_params=pltpu.CompilerParams(dimension_semantics=("parallel", "arbitrary"),
                                             vmem_limit_bytes=V7X_VMEM_LIMIT_BYTES),
        name="even_layer_hist" if has_hist else "even_layer",
    )(*args)


def _odd_kernel(x_ref, c0_ref, n0_ref, m0_ref, gpre_ref, gpost_ref, wqkv_ref, woz_ref, wg_ref, bg_ref,
                gain_ref, wout_ref, y_ref, cout_ref, nout_ref, mout_ref,
                q_sc, k_sc, v_sc, oz_sc, mixed_sc, c_sc, n_sc, m_sc, *, tt, lc):
    t = pl.program_id(1)
    n_chunks = tt // lc
    w_, dh, nh = ML_WIDTH, ML_HEAD_DIM, ML_HEADS

    @pl.when(t == 0)
    def _():
        c_sc[...] = c0_ref[0]
        n_sc[...] = n0_ref[0]
        m_sc[...] = m0_ref[0]

    x = x_ref[0]
    h = _rmsnorm(x, gpre_ref[...]).astype(BF16)
    q_sc[...] = _dot(h, wqkv_ref[:, 0:w_]).astype(BF16)
    k_sc[...] = (_dot(h, wqkv_ref[:, w_:2 * w_]) * (dh ** -0.5)).astype(BF16)
    v_sc[...] = _dot(h, wqkv_ref[:, 2 * w_:3 * w_]).astype(BF16)
    oz_sc[...] = jax.nn.sigmoid(_dot(h, woz_ref[:, 0:w_])) * _silu(_dot(h, woz_ref[:, w_:2 * w_]))

    gates = _dot(h, wg_ref[...]) + bg_ref[...]
    ig = gates[:, 0:LANES]
    lf = jax.nn.log_sigmoid(gates[:, LANES:2 * LANES])
    ri = lax.broadcasted_iota(jnp.int32, (tt, tt), 0)
    ci = lax.broadcasted_iota(jnp.int32, (tt, tt), 1)
    tri = jnp.where(((ri // lc) == (ci // lc)) & (ci <= ri), 1.0, 0.0).astype(BF16)
    lf3 = _split3(lf)
    bcol = _dot(tri, lf3[0]) + _dot(tri, lf3[1]) + _dot(tri, lf3[2])
    sel = jnp.where(lax.broadcasted_iota(jnp.int32, (8, LANES), 0)
                    == lax.broadcasted_iota(jnp.int32, (8, LANES), 1), 1.0, 0.0).astype(BF16)
    b3 = _split3(bcol)
    g3 = _split3(ig)
    brow = _dot_nt(sel, b3[0]) + _dot_nt(sel, b3[1]) + _dot_nt(sel, b3[2])
    igrow = _dot_nt(sel, g3[0]) + _dot_nt(sel, g3[1]) + _dot_nt(sel, g3[2])

    causal = (lax.broadcasted_iota(jnp.int32, (lc, lc), 1) <= lax.broadcasted_iota(jnp.int32, (lc, lc), 0))
    m_prev_all = m_sc[...]
    for c in range(n_chunks):
        rows = slice(c * lc, (c + 1) * lc)
        b_all = bcol[rows]
        b_last_all = b_all[lc - 1:lc]
        g_all = b_last_all - b_all + ig[rows]
        m_new_all = jnp.maximum(b_last_all + m_prev_all, jnp.max(g_all, axis=0, keepdims=True))
        decay_all = jnp.exp(b_last_all + m_prev_all - m_new_all)
        wgt_all = jnp.exp(g_all - m_new_all)
        inter_all = b_all + m_prev_all
        for hd in range(nh):
            ln = slice(hd * dh, (hd + 1) * dh)
            b_c = b_all[:, hd:hd + 1]
            b_r = brow[hd:hd + 1, rows]
            ig_r = igrow[hd:hd + 1, rows]
            inter = inter_all[:, hd:hd + 1]
            dmat = jnp.where(causal, b_c - b_r + ig_r, -jnp.inf)
            m_t = jnp.maximum(inter, jnp.max(dmat, axis=-1, keepdims=True))
            a = jnp.exp(inter - m_t)
            q = q_sc[rows, ln]
            k = k_sc[rows, ln]
            v = v_sc[rows, ln]
            s = _dot_nt(q, k) * jnp.exp(dmat - m_t)
            c_old = c_sc[hd]
            n_old = n_sc[hd]
            num = a * _dot_nt(q, c_old.astype(BF16)) + _dot(s.astype(BF16), v)
            qn = jnp.sum(q.astype(F32) * n_old, axis=-1, keepdims=True)
            den = a * qn + jnp.sum(s, axis=-1, keepdims=True)
            hv = num / jnp.maximum(jnp.abs(den), jnp.exp(-m_t))
            mu = jnp.mean(hv, axis=-1, keepdims=True)
            dlt = hv - mu
            var = jnp.mean(dlt * dlt, axis=-1, keepdims=True)
            hn = dlt * lax.rsqrt(var + LN_EPS) * gain_ref[:, ln]
            mixed_sc[rows, ln] = (hn * oz_sc[rows, ln]).astype(BF16)
            decay = decay_all[:, hd:hd + 1]
            wgt = wgt_all[:, hd:hd + 1]
            vw = (v.astype(F32) * wgt).astype(BF16)
            c_sc[hd] = decay * c_old + _dot_tn(vw, k)
            n_sc[hd] = decay * n_old + jnp.sum(wgt * k.astype(F32), axis=0, keepdims=True)
        m_prev_all = m_new_all
    m_sc[...] = m_prev_all

    y = _dot(mixed_sc[...], wout_ref[...])
    y_ref[0] = x_ref[0] + _rmsnorm(y, gpost_ref[...])
    cout_ref[0] = c_sc[...]
    nout_ref[0] = n_sc[...]
    mout_ref[0] = m_sc[...]


def _odd_layer(x, state, gpre, gpost, w_qkv, w_oz, w_g, b_g, gain, w_out, *, tt, lc):
    b, t_len, d = x.shape
    nt = t_len // tt
    assert t_len % tt == 0 and tt % lc == 0
    nh, dh = ML_HEADS, ML_HEAD_DIM

    def whole(shape):
        return pl.BlockSpec(shape, lambda i, j: (0,) * len(shape))

    def per_seq(shape):
        return pl.BlockSpec((1,) + shape, lambda i, j: (i,) + (0,) * len(shape))

    st_specs = [per_seq((nh, dh, dh)), per_seq((nh, 1, dh)), per_seq((1, LANES))]
    in_specs = ([pl.BlockSpec((1, tt, d), lambda i, j: (i, j, 0))] + st_specs +
                [whole((1, d)), whole((1, d)), whole((d, 3 * ML_WIDTH)), whole((d, 2 * ML_WIDTH)),
                 whole((d, 2 * LANES)), whole((1, 2 * LANES)), whole((1, ML_WIDTH)), whole((ML_WIDTH, d))])
    out_shape = (jax.ShapeDtypeStruct((b, t_len, d), F32),
                 jax.ShapeDtypeStruct((b, nh, dh, dh), F32),
                 jax.ShapeDtypeStruct((b, nh, 1, dh), F32),
                 jax.ShapeDtypeStruct((b, 1, LANES), F32))
    out_specs = (pl.BlockSpec((1, tt, d), lambda i, j: (i, j, 0)), *st_specs)
    scratch = [pltpu.VMEM((tt, ML_WIDTH), BF16), pltpu.VMEM((tt, ML_WIDTH), BF16), pltpu.VMEM((tt, ML_WIDTH), BF16),
               pltpu.VMEM((tt, ML_WIDTH), F32), pltpu.VMEM((tt, ML_WIDTH), BF16),
               pltpu.VMEM((nh, dh, dh), F32), pltpu.VMEM((nh, 1, dh), F32), pltpu.VMEM((1, LANES), F32)]
    return pl.pallas_call(
        functools.partial(_odd_kernel, tt=tt, lc=lc),
        grid=(b, nt), in_specs=in_specs, out_specs=out_specs, out_shape=out_shape,
        scratch_shapes=scratch,
        compiler_params=pltpu.CompilerParams(dimension_semantics=("parallel", "arbitrary"),
                                             vmem_limit_bytes=V7X_VMEM_LIMIT_BYTES),
        name="odd_layer",
    )(x, *state, gpre, gpost, w_qkv, w_oz, w_g, b_g, gain, w_out)


PROMPT_TILE = 512
ATT_BLOCK_CHUNKS = 2
ML_PROMPT_CHUNK = 256


def kernel(x_prompt, x_sample, cache_pool, cache_k, cache_v, state_C, state_n, state_m, norm_pre, norm_post,
           w_in_even, w_pool_mix, pool_scale, rel_bias, w_out_even, w_in_odd, b_gate_odd, mlstm_norm, w_out_odd):
    depth = norm_pre.shape[0]
    bp, dec_seq = x_prompt.shape[0], x_sample.shape[1]
    bs = x_sample.shape[0]
    nh, dh = ML_HEADS, ML_HEAD_DIM
    xp, xs = x_prompt, x_sample
    outs = {name: [] for name in ("pool_p", "k_p", "v_p", "C_p", "n_p", "m_p",
                                  "pool_s", "k_s", "v_s", "C_s", "n_s", "m_s")}
    for layer in range(depth):
        gpre = norm_pre[layer][None, :]
        gpost = norm_post[layer][None, :]
        if layer % 2 == 0:
            e = layer // 2
            w_in = w_in_even[e].astype(BF16)
            w_mix = w_pool_mix[e].astype(BF16)
            w_out = w_out_even[e].astype(BF16)
            scale = pool_scale[e][None, :]
            bias_p = _expand_bias(rel_bias[e], CHUNK, ATT_BLOCK_CHUNKS)
            bias_s = _expand_bias(rel_bias[e], dec_seq, 1)
            xp, pp, kp, vp = _even_layer(xp, None, gpre, gpost, w_in, w_mix, scale, bias_p, w_out,
                                         tt=PROMPT_TILE, rb=ATT_BLOCK_CHUNKS * CHUNK, pos0=0)
            hist = (jnp.pad(cache_pool[e], ((0, 0), (POOL_HIST_ROWS - POOL_HIST, 0), (0, 0))),
                    cache_k[e].reshape(bs, BAND, ATT_WIDTH), cache_v[e].reshape(bs, BAND, ATT_WIDTH))
            xs, ps, ks, vs = _even_layer(xs, hist, gpre, gpost, w_in, w_mix, scale, bias_s, w_out,
                                         tt=dec_seq, rb=dec_seq, pos0=PAST_LEN)
            outs["pool_p"].append(pp[:, POOL_HIST_ROWS - POOL_HIST:])
            outs["k_p"].append(kp.reshape(bp, -1, ATT_HEADS, ATT_HEAD_DIM))
            outs["v_p"].append(vp.reshape(bp, -1, ATT_HEADS, ATT_HEAD_DIM))
            outs["pool_s"].append(ps[:, POOL_HIST_ROWS - POOL_HIST:])
            outs["k_s"].append(ks.reshape(bs, -1, ATT_HEADS, ATT_HEAD_DIM))
            outs["v_s"].append(vs.reshape(bs, -1, ATT_HEADS, ATT_HEAD_DIM))
        else:
            o = layer // 2
            w = w_in_odd[o]
            w_qkv = w[:, :3 * ML_WIDTH].astype(BF16)
            w_oz = w[:, 3 * ML_WIDTH:5 * ML_WIDTH].astype(BF16)
            pad_h = ((0, 0), (0, LANES - nh))
            w_g = jnp.concatenate([jnp.pad(w[:, 5 * ML_WIDTH:5 * ML_WIDTH + nh], pad_h),
                                   jnp.pad(w[:, 5 * ML_WIDTH + nh:], pad_h)], axis=1).astype(BF16)
            b_g = jnp.concatenate([jnp.pad(b_gate_odd[o][:nh], (0, LANES - nh)),
                                   jnp.pad(b_gate_odd[o][nh:], (0, LANES - nh))])[None, :]
            gain = mlstm_norm[o][None, :]
            w_out = w_out_odd[o].astype(BF16)
            zero_state = (jnp.zeros((bp, nh, dh, dh), F32), jnp.zeros((bp, nh, 1, dh), F32),
                          jnp.zeros((bp, 1, LANES), F32))
            xp, cp, np_, mp = _odd_layer(xp, zero_state, gpre, gpost, w_qkv, w_oz, w_g, b_g, gain, w_out,
                                         tt=PROMPT_TILE, lc=ML_PROMPT_CHUNK)
            state = (state_C[o], state_n[o][:, :, None, :],
                     jnp.pad(state_m[o], ((0, 0), (0, LANES - nh)))[:, None, :])
            xs, cs, ns, ms = _odd_layer(xs, state, gpre, gpost, w_qkv, w_oz, w_g, b_g, gain, w_out,
                                        tt=dec_seq, lc=dec_seq)
            outs["C_p"].append(cp)
            outs["n_p"].append(np_[:, :, 0, :])
            outs["m_p"].append(mp[:, 0, :nh])
            outs["C_s"].append(cs)
            outs["n_s"].append(ns[:, :, 0, :])
            outs["m_s"].append(ms[:, 0, :nh])
    return (xp, xs,
            jnp.stack(outs["pool_p"]), jnp.stack(outs["k_p"]), jnp.stack(outs["v_p"]),
            jnp.stack(outs["C_p"]), jnp.stack(outs["n_p"]), jnp.stack(outs["m_p"]),
            jnp.stack(outs["pool_s"]), jnp.stack(outs["k_s"]), jnp.stack(outs["v_s"]),
            jnp.stack(outs["C_s"]), jnp.stack(outs["n_s"]), jnp.stack(outs["m_s"]))
```

```python
import functools

import jax
import jax.numpy as jnp
from jax import lax
from jax.experimental import pallas as pl
from jax.experimental.pallas import tpu as pltpu

D_MODEL = 1024
PAST_LEN = 4096
CHUNK = 64
POOL_WINDOWS = (2, 4, 8, 16)
POOL_GROUPS = len(POOL_WINDOWS)
POOL_WIDTH = D_MODEL // 2
POOL_GROUP_DIM = POOL_WIDTH // POOL_GROUPS
POOL_HIST = max(POOL_WINDOWS) - 1
POOL_HIST_ROWS = 16
ATT_WIDTH = D_MODEL // 2
ATT_HEADS = 8
ATT_HEAD_DIM = ATT_WIDTH // ATT_HEADS
HEAD_PAIRS = ATT_HEADS // 2
BAND = 8 * CHUNK
REL_CLIP = 128
REL_TABLE = 2 * REL_CLIP + 1
REL_TABLE_PAD = 384
ML_WIDTH = D_MODEL
ML_HEADS = 4
ML_HEAD_DIM = ML_WIDTH // ML_HEADS
EVEN_IN = 2 * POOL_WIDTH + 4 * ATT_WIDTH
RMS_EPS = 1e-6
LN_EPS = 1e-6

LANES = 128
V7X_VMEM_LIMIT_BYTES = 56 * 1024 * 1024

BF16 = jnp.bfloat16
F32 = jnp.float32


def _dot(a, b):
    return jnp.dot(a, b, preferred_element_type=F32)


def _dot_nt(a, b):
    return lax.dot_general(a, b, (((1,), (1,)), ((), ())), preferred_element_type=F32)


def _dot_tn(a, b):
    return lax.dot_general(a, b, (((0,), (0,)), ((), ())), preferred_element_type=F32)


def _split3(x):
    hi = x.astype(BF16)
    r1 = x - hi.astype(F32)
    mid = r1.astype(BF16)
    lo = (r1 - mid.astype(F32)).astype(BF16)
    return hi, mid, lo


def _rmsnorm(x, g):
    return x * lax.rsqrt(jnp.mean(x * x, axis=-1, keepdims=True) + RMS_EPS) * g


def _silu(x):
    return x * jax.nn.sigmoid(x)


def _bias_kernel(tab_ref, out_ref, *, lc, nb):
    rb = nb * lc
    kb = BAND + rb
    n_dist = rb - 1 + kb
    gw = (n_dist + LANES - 1) // LANES * LANES
    parts = _split3(tab_ref[...])
    dist = lax.broadcasted_iota(jnp.int32, (REL_TABLE_PAD, gw), 1) - (rb - 1)
    ti = lax.broadcasted_iota(jnp.int32, (REL_TABLE_PAD, gw), 0)
    onehot = jnp.where(ti == jnp.clip(BAND - dist, -REL_CLIP, REL_CLIP) + REL_CLIP, 1.0, 0.0).astype(BF16)
    by_dist = _dot(parts[0], onehot) + _dot(parts[1], onehot) + _dot(parts[2], onehot)
    key_chunk = lax.broadcasted_iota(jnp.int32, (ATT_HEADS, kb), 1) // lc

    def row(qi, carry):
        shift = lax.rem(gw - (rb - 1 - qi), gw)
        acc = pltpu.roll(by_dist, shift, axis=1)[:, 0:kb]
        q_chunk = qi // lc
        in_band = (key_chunk >= q_chunk) & (key_chunk <= q_chunk + BAND // lc)
        out_ref[qi] = jnp.where(in_band, acc, -jnp.inf)
        return carry

    lax.fori_loop(0, rb, row, 0)


def _expand_bias(table, lc, nb):
    rb = nb * lc
    kb = BAND + rb
    tab = jnp.pad(table.astype(F32), ((0, 0), (0, REL_TABLE_PAD - REL_TABLE)))
    out = pl.pallas_call(
        functools.partial(_bias_kernel, lc=lc, nb=nb),
        out_shape=jax.ShapeDtypeStruct((rb, ATT_HEADS, kb), F32),
        name="rel_bias_expand",
    )(tab)
    return out.transpose(1, 0, 2).reshape(HEAD_PAIRS, 2 * rb, kb)


def _even_kernel(*refs, tt, rb, has_hist, pos0):
    if has_hist:
        (x_ref, hu_ref, hk_ref, hv_ref, gpre_ref, gpost_ref, win_ref, wmix_ref, scale_ref,
         bias_ref, wout_ref, y_ref, pool_ref, kout_ref, vout_ref,
         uext, kext, vext, q_sc, gate_sc, mixed_sc, s_sc) = refs
    else:
        (x_ref, gpre_ref, gpost_ref, win_ref, wmix_ref, scale_ref,
         bias_ref, wout_ref, y_ref, pool_ref, kout_ref, vout_ref,
         uext, kext, vext, q_sc, gate_sc, mixed_sc, s_sc) = refs
    t = pl.program_id(1)
    hr = POOL_HIST_ROWS
    kb = BAND + rb

    @pl.when(t == 0)
    def _():
        if has_hist:
            uext[0:hr, :] = hu_ref[0]
            kext[0:BAND, :] = hk_ref[0].astype(BF16)
            vext[0:BAND, :] = hv_ref[0].astype(BF16)
        else:
            uext[0:hr, :] = jnp.zeros((hr, POOL_WIDTH), F32)
            kext[0:BAND, :] = jnp.zeros((BAND, ATT_WIDTH), BF16)
            vext[0:BAND, :] = jnp.zeros((BAND, ATT_WIDTH), BF16)

    @pl.when(t > 0)
    def _():
        uext[0:hr, :] = uext[tt:tt + hr, :]
        kext[0:BAND, :] = kext[tt:tt + BAND, :]
        vext[0:BAND, :] = vext[tt:tt + BAND, :]

    x = x_ref[0]
    h = _rmsnorm(x, gpre_ref[...]).astype(BF16)
    p_, a_ = POOL_WIDTH, ATT_WIDTH
    uext[hr:hr + tt, :] = _dot(h, win_ref[:, 0:p_])
    q_sc[...] = (_dot(h, win_ref[:, p_:p_ + a_]) * (ATT_HEAD_DIM ** -0.5)).astype(BF16)
    k = _dot(h, win_ref[:, p_ + a_:p_ + 2 * a_])
    kext[BAND:BAND + tt, :] = k.astype(BF16)
    kout_ref[0] = k
    v = _dot(h, win_ref[:, p_ + 2 * a_:p_ + 3 * a_])
    vext[BAND:BAND + tt, :] = v.astype(BF16)
    vout_ref[0] = v
    gate_sc[...] = _silu(_dot(h, win_ref[:, p_ + 3 * a_:]))

    pos = pos0 + t * tt + lax.broadcasted_iota(jnp.int32, (tt, 1), 0)
    for g, w in enumerate(POOL_WINDOWS):
        ln = slice(g * POOL_GROUP_DIM, (g + 1) * POOL_GROUP_DIM)
        tok = uext[hr:hr + tt, ln]
        win_sum = tok
        for back in range(1, w):
            win_sum = win_sum + uext[hr - back:hr - back + tt, ln]
        count = jnp.minimum(pos + 1, w).astype(F32)
        pooled = win_sum / count - tok
        mixed = _dot(pooled.astype(BF16), wmix_ref[g]) * scale_ref[:, ln]
        mixed_sc[:, ln] = (mixed * gate_sc[:, ln]).astype(BF16)

    lane = lax.broadcasted_iota(jnp.int32, (rb, LANES), 1)
    even_head = lane < ATT_HEAD_DIM

    def attend(sequence_start):
        units = [(blk, p) for blk in range(tt // rb) for p in range(HEAD_PAIRS)]

        def band(blk):
            r0 = blk * rb
            k0 = BAND if sequence_start else r0
            c0 = BAND - r0 if sequence_start else 0
            return r0, k0, c0, kb - c0

        def stage_scores(i):
            blk, p = units[i]
            r0, k0, c0, kw = band(blk)
            ln = slice(p * LANES, (p + 1) * LANES)
            qp = q_sc[r0:r0 + rb, ln]
            zero = jnp.zeros_like(qp)
            q2 = jnp.concatenate([jnp.where(even_head, qp, zero), jnp.where(even_head, zero, qp)], axis=0)
            s_sc[i % 2, :, 0:kw] = _dot_nt(q2, kext[k0:k0 + kw, ln]) + bias_ref[p, :, c0:c0 + kw]

        def finish(i):
            blk, p = units[i]
            r0, k0, c0, kw = band(blk)
            ln = slice(p * LANES, (p + 1) * LANES)
            row_max = jnp.max(s_sc[i % 2, :, 0:kw], axis=-1, keepdims=True)
            e = jnp.exp(s_sc[i % 2, :, 0:kw] - row_max)
            denom = jnp.sum(e, axis=-1, keepdims=True)
            o2 = _dot(e.astype(BF16), vext[k0:k0 + kw, ln]) / denom
            o = jnp.where(even_head, o2[0:rb], o2[rb:2 * rb])
            mo = slice(POOL_WIDTH + p * LANES, POOL_WIDTH + (p + 1) * LANES)
            mixed_sc[r0:r0 + rb, mo] = (o * gate_sc[r0:r0 + rb, mo]).astype(BF16)

        stage_scores(0)
        for i in range(len(units)):
            if i + 1 < len(units):
                stage_scores(i + 1)
            finish(i)

    if has_hist:
        attend(False)
    else:
        pl.when(t == 0)(functools.partial(attend, True))
        pl.when(t > 0)(functools.partial(attend, False))

    y = _dot(mixed_sc[...], wout_ref[...])
    y_ref[0] = x_ref[0] + _rmsnorm(y, gpost_ref[...])
    pool_ref[0] = uext[tt:tt + hr, :]


def _even_layer(x, hist, gpre, gpost, w_in, w_mix, scale, bias, w_out, *, tt, rb, pos0):
    b, t_len, d = x.shape
    has_hist = hist is not None
    nt = t_len // tt
    assert t_len % tt == 0 and tt % rb == 0
    assert has_hist or rb % LANES == 0, "band start at a sequence start must stay lane aligned in the bias"
    keep = min(BAND, t_len)
    assert keep == tt, "key/value cache rows must be exactly the last time tile"
    assert nt == 1 or tt >= BAND
    kb = BAND + rb

    def whole(shape):
        return pl.BlockSpec(shape, lambda i, j: (0,) * len(shape))

    def per_seq(shape):
        return pl.BlockSpec((1,) + shape, lambda i, j: (i, 0, 0))

    in_specs = [pl.BlockSpec((1, tt, d), lambda i, j: (i, j, 0))]
    args = [x]
    if has_hist:
        in_specs += [per_seq((POOL_HIST_ROWS, POOL_WIDTH)), per_seq((BAND, ATT_WIDTH)), per_seq((BAND, ATT_WIDTH))]
        args += list(hist)
    in_specs += [whole((1, d)), whole((1, d)), whole((d, EVEN_IN)),
                 whole((POOL_GROUPS, POOL_GROUP_DIM, POOL_GROUP_DIM)), whole((1, POOL_WIDTH)),
                 whole((HEAD_PAIRS, 2 * rb, kb)), whole((POOL_WIDTH + ATT_WIDTH, d))]
    args += [gpre, gpost, w_in, w_mix, scale, bias, w_out]
    out_shape = (jax.ShapeDtypeStruct((b, t_len, d), F32),
                 jax.ShapeDtypeStruct((b, POOL_HIST_ROWS, POOL_WIDTH), F32),
                 jax.ShapeDtypeStruct((b, keep, ATT_WIDTH), F32),
                 jax.ShapeDtypeStruct((b, keep, ATT_WIDTH), F32))
    out_specs = (pl.BlockSpec((1, tt, d), lambda i, j: (i, j, 0)),
                 per_seq((POOL_HIST_ROWS, POOL_WIDTH)), per_seq((keep, ATT_WIDTH)), per_seq((keep, ATT_WIDTH)))
    scratch = [pltpu.VMEM((POOL_HIST_ROWS + tt, POOL_WIDTH), F32),
               pltpu.VMEM((BAND + tt, ATT_WIDTH), BF16),
               pltpu.VMEM((BAND + tt, ATT_WIDTH), BF16),
               pltpu.VMEM((tt, ATT_WIDTH), BF16),
               pltpu.VMEM((tt, POOL_WIDTH + ATT_WIDTH), F32),
               pltpu.VMEM((tt, POOL_WIDTH + ATT_WIDTH), BF16),
               pltpu.VMEM((2, 2 * rb, kb), F32)]
    return pl.pallas_call(
        functools.partial(_even_kernel, tt=tt, rb=rb, has_hist=has_hist, pos0=pos0),
        grid=(b, nt), in_specs=in_specs, out_specs=out_specs, out_shape=out_shape,
        scratch_shapes=scratch,
        compiler_params=pltpu.CompilerParams(dimension_semantics=("parallel", "arbitrary"),
                                             vmem_limit_bytes=V7X_VMEM_LIMIT_BYTES),
        name="even_layer_hist" if has_hist else "even_layer",
    )(*args)


def _odd_kernel(x_ref, c0_ref, n0_ref, m0_ref, gpre_ref, gpost_ref, wqkv_ref, woz_ref, wg_ref, bg_ref,
                gain_ref, wout_ref, y_ref, cout_ref, nout_ref, mout_ref,
                q_sc, k_sc, v_sc, oz_sc, mixed_sc, c_sc, n_sc, m_sc, *, tt, lc):
    t = pl.program_id(1)
    n_chunks = tt // lc
    w_, dh, nh = ML_WIDTH, ML_HEAD_DIM, ML_HEADS

    @pl.when(t == 0)
    def _():
        c_sc[...] = c0_ref[0]
        n_sc[...] = n0_ref[0]
        m_sc[...] = m0_ref[0]

    x = x_ref[0]
    h = _rmsnorm(x, gpre_ref[...]).astype(BF16)

    def project(c, part):
        rows = slice(c * lc, (c + 1) * lc)
        hc = h[rows]
        if part == 0:
            q_sc[rows, :] = _dot(hc, wqkv_ref[:, 0:w_]).astype(BF16)
        elif part == 1:
            k_sc[rows, :] = (_dot(hc, wqkv_ref[:, w_:2 * w_]) * (dh ** -0.5)).astype(BF16)
        elif part == 2:
            v_sc[rows, :] = _dot(hc, wqkv_ref[:, 2 * w_:3 * w_]).astype(BF16)
        else:
            oz_sc[rows, :] = (jax.nn.sigmoid(_dot(hc, woz_ref[:, 0:w_]))
                              * _silu(_dot(hc, woz_ref[:, w_:2 * w_])))

    gates = _dot(h, wg_ref[...]) + bg_ref[...]
    ig = gates[:, 0:LANES]
    lf = jax.nn.log_sigmoid(gates[:, LANES:2 * LANES])
    ri = lax.broadcasted_iota(jnp.int32, (tt, tt), 0)
    ci = lax.broadcasted_iota(jnp.int32, (tt, tt), 1)
    tri = jnp.where(((ri // lc) == (ci // lc)) & (ci <= ri), 1.0, 0.0).astype(BF16)
    lf3 = _split3(lf)
    bcol = _dot(tri, lf3[0]) + _dot(tri, lf3[1]) + _dot(tri, lf3[2])
    sel = jnp.where(lax.broadcasted_iota(jnp.int32, (8, LANES), 0)
                    == lax.broadcasted_iota(jnp.int32, (8, LANES), 1), 1.0, 0.0).astype(BF16)
    b3 = _split3(bcol)
    g3 = _split3(ig)
    brow = _dot_nt(sel, b3[0]) + _dot_nt(sel, b3[1]) + _dot_nt(sel, b3[2])
    igrow = _dot_nt(sel, g3[0]) + _dot_nt(sel, g3[1]) + _dot_nt(sel, g3[2])

    causal = (lax.broadcasted_iota(jnp.int32, (lc, lc), 1) <= lax.broadcasted_iota(jnp.int32, (lc, lc), 0))
    m_prev_all = m_sc[...]
    chunk_gates = []
    for c in range(n_chunks):
        rows = slice(c * lc, (c + 1) * lc)
        b_all = bcol[rows]
        b_last_all = b_all[lc - 1:lc]
        g_all = b_last_all - b_all + ig[rows]
        m_new_all = jnp.maximum(b_last_all + m_prev_all, jnp.max(g_all, axis=0, keepdims=True))
        chunk_gates.append(dict(b=b_all, inter=b_all + m_prev_all,
                                decay=jnp.exp(b_last_all + m_prev_all - m_new_all),
                                wgt=jnp.exp(g_all - m_new_all)))
        m_prev_all = m_new_all
    m_sc[...] = m_prev_all

    def unit(c, hd):
        rows = slice(c * lc, (c + 1) * lc)
        ln = slice(hd * dh, (hd + 1) * dh)
        gts = chunk_gates[c]
        b_c = gts["b"][:, hd:hd + 1]
        b_r = brow[hd:hd + 1, rows]
        ig_r = igrow[hd:hd + 1, rows]
        inter = gts["inter"][:, hd:hd + 1]
        dmat = jnp.where(causal, b_c - b_r + ig_r, -jnp.inf)
        m_t = jnp.maximum(inter, jnp.max(dmat, axis=-1, keepdims=True))
        a = jnp.exp(inter - m_t)
        q = q_sc[rows, ln]
        k = k_sc[rows, ln]
        v = v_sc[rows, ln]
        s = _dot_nt(q, k) * jnp.exp(dmat - m_t)
        c_old = c_sc[hd]
        n_old = n_sc[hd]
        num = a * _dot_nt(q, c_old.astype(BF16)) + _dot(s.astype(BF16), v)
        qn = jnp.sum(q.astype(F32) * n_old, axis=-1, keepdims=True)
        den = a * qn + jnp.sum(s, axis=-1, keepdims=True)
        hv = num / jnp.maximum(jnp.abs(den), jnp.exp(-m_t))
        mu = jnp.mean(hv, axis=-1, keepdims=True)
        dlt = hv - mu
        var = jnp.mean(dlt * dlt, axis=-1, keepdims=True)
        hn = dlt * lax.rsqrt(var + LN_EPS) * gain_ref[:, ln]
        mixed_sc[rows, ln] = (hn * oz_sc[rows, ln]).astype(BF16)
        decay = gts["decay"][:, hd:hd + 1]
        wgt = gts["wgt"][:, hd:hd + 1]
        vw = (v.astype(F32) * wgt).astype(BF16)
        c_sc[hd] = decay * c_old + _dot_tn(vw, k)
        n_sc[hd] = decay * n_old + jnp.sum(wgt * k.astype(F32), axis=0, keepdims=True)

    def out_project(c):
        rows = slice(c * lc, (c + 1) * lc)
        y = _dot(mixed_sc[rows, :], wout_ref[...])
        y_ref[0, rows, :] = x_ref[0, rows, :] + _rmsnorm(y, gpost_ref[...])

    n_parts = 4
    for part in range(n_parts):
        project(0, part)
    for c in range(n_chunks):
        if c > 0:
            out_project(c - 1)
        for hd in range(nh):
            if c + 1 < n_chunks:
                for part in range(hd * n_parts // nh, (hd + 1) * n_parts // nh):
                    project(c + 1, part)
            unit(c, hd)
    out_project(n_chunks - 1)
    cout_ref[0] = c_sc[...]
    nout_ref[0] = n_sc[...]
    mout_ref[0] = m_sc[...]


def _odd_layer(x, state, gpre, gpost, w_qkv, w_oz, w_g, b_g, gain, w_out, *, tt, lc):
    b, t_len, d = x.shape
    nt = t_len // tt
    assert t_len % tt == 0 and tt % lc == 0
    nh, dh = ML_HEADS, ML_HEAD_DIM

    def whole(shape):
        return pl.BlockSpec(shape, lambda i, j: (0,) * len(shape))

    def per_seq(shape):
        return pl.BlockSpec((1,) + shape, lambda i, j: (i,) + (0,) * len(shape))

    st_specs = [per_seq((nh, dh, dh)), per_seq((nh, 1, dh)), per_seq((1, LANES))]
    in_specs = ([pl.BlockSpec((1, tt, d), lambda i, j: (i, j, 0))] + st_specs +
                [whole((1, d)), whole((1, d)), whole((d, 3 * ML_WIDTH)), whole((d, 2 * ML_WIDTH)),
                 whole((d, 2 * LANES)), whole((1, 2 * LANES)), whole((1, ML_WIDTH)), whole((ML_WIDTH, d))])
    out_shape = (jax.ShapeDtypeStruct((b, t_len, d), F32),
                 jax.ShapeDtypeStruct((b, nh, dh, dh), F32),
                 jax.ShapeDtypeStruct((b, nh, 1, dh), F32),
                 jax.ShapeDtypeStruct((b, 1, LANES), F32))
    out_specs = (pl.BlockSpec((1, tt, d), lambda i, j: (i, j, 0)), *st_specs)
    scratch = [pltpu.VMEM((tt, ML_WIDTH), BF16), pltpu.VMEM((tt, ML_WIDTH), BF16), pltpu.VMEM((tt, ML_WIDTH), BF16),
               pltpu.VMEM((tt, ML_WIDTH), F32), pltpu.VMEM((tt, ML_WIDTH), BF16),
               pltpu.VMEM((nh, dh, dh), F32), pltpu.VMEM((nh, 1, dh), F32), pltpu.VMEM((1, LANES), F32)]
    return pl.pallas_call(
        functools.partial(_odd_kernel, tt=tt, lc=lc),
        grid=(b, nt), in_specs=in_specs, out_specs=out_specs, out_shape=out_shape,
        scratch_shapes=scratch,
        compiler_params=pltpu.CompilerParams(dimension_semantics=("parallel", "arbitrary"),
                                             vmem_limit_bytes=V7X_VMEM_LIMIT_BYTES),
        name="odd_layer",
    )(x, *state, gpre, gpost, w_qkv, w_oz, w_g, b_g, gain, w_out)


PROMPT_TILE = 512
ATT_BLOCK_CHUNKS = 2
ML_PROMPT_CHUNK = 256


def kernel(x_prompt, x_sample, cache_pool, cache_k, cache_v, state_C, state_n, state_m, norm_pre, norm_post,
           w_in_even, w_pool_mix, pool_scale, rel_bias, w_out_even, w_in_odd, b_gate_odd, mlstm_norm, w_out_odd):
    depth = norm_pre.shape[0]
    bp, dec_seq = x_prompt.shape[0], x_sample.shape[1]
    bs = x_sample.shape[0]
    nh, dh = ML_HEADS, ML_HEAD_DIM
    xp, xs = x_prompt, x_sample
    outs = {name: [] for name in ("pool_p", "k_p", "v_p", "C_p", "n_p", "m_p",
                                  "pool_s", "k_s", "v_s", "C_s", "n_s", "m_s")}
    for layer in range(depth):
        gpre = norm_pre[layer][None, :]
        gpost = norm_post[layer][None, :]
        if layer % 2 == 0:
            e = layer // 2
            w_in = w_in_even[e].astype(BF16)
            w_mix = w_pool_mix[e].astype(BF16)
            w_out = w_out_even[e].astype(BF16)
            scale = pool_scale[e][None, :]
            bias_p = _expand_bias(rel_bias[e], CHUNK, ATT_BLOCK_CHUNKS)
            bias_s = _expand_bias(rel_bias[e], dec_seq, 1)
            xp, pp, kp, vp = _even_layer(xp, None, gpre, gpost, w_in, w_mix, scale, bias_p, w_out,
                                         tt=PROMPT_TILE, rb=ATT_BLOCK_CHUNKS * CHUNK, pos0=0)
            hist = (jnp.pad(cache_pool[e], ((0, 0), (POOL_HIST_ROWS - POOL_HIST, 0), (0, 0))),
                    cache_k[e].reshape(bs, BAND, ATT_WIDTH), cache_v[e].reshape(bs, BAND, ATT_WIDTH))
            xs, ps, ks, vs = _even_layer(xs, hist, gpre, gpost, w_in, w_mix, scale, bias_s, w_out,
                                         tt=dec_seq, rb=dec_seq, pos0=PAST_LEN)
            outs["pool_p"].append(pp[:, POOL_HIST_ROWS - POOL_HIST:])
            outs["k_p"].append(kp.reshape(bp, -1, ATT_HEADS, ATT_HEAD_DIM))
            outs["v_p"].append(vp.reshape(bp, -1, ATT_HEADS, ATT_HEAD_DIM))
            outs["pool_s"].append(ps[:, POOL_HIST_ROWS - POOL_HIST:])
            outs["k_s"].append(ks.reshape(bs, -1, ATT_HEADS, ATT_HEAD_DIM))
            outs["v_s"].append(vs.reshape(bs, -1, ATT_HEADS, ATT_HEAD_DIM))
        else:
            o = layer // 2
            w = w_in_odd[o]
            w_qkv = w[:, :3 * ML_WIDTH].astype(BF16)
            w_oz = w[:, 3 * ML_WIDTH:5 * ML_WIDTH].astype(BF16)
            pad_h = ((0, 0), (0, LANES - nh))
            w_g = jnp.concatenate([jnp.pad(w[:, 5 * ML_WIDTH:5 * ML_WIDTH + nh], pad_h),
                                   jnp.pad(w[:, 5 * ML_WIDTH + nh:], pad_h)], axis=1).astype(BF16)
            b_g = jnp.concatenate([jnp.pad(b_gate_odd[o][:nh], (0, LANES - nh)),
                                   jnp.pad(b_gate_odd[o][nh:], (0, LANES - nh))])[None, :]
            gain = mlstm_norm[o][None, :]
            w_out = w_out_odd[o].astype(BF16)
            zero_state = (jnp.zeros((bp, nh, dh, dh), F32), jnp.zeros((bp, nh, 1, dh), F32),
                          jnp.zeros((bp, 1, LANES), F32))
            xp, cp, np_, mp = _odd_layer(xp, zero_state, gpre, gpost, w_qkv, w_oz, w_g, b_g, gain, w_out,
                                         tt=PROMPT_TILE, lc=ML_PROMPT_CHUNK)
            state = (state_C[o], state_n[o][:, :, None, :],
                     jnp.pad(state_m[o], ((0, 0), (0, LANES - nh)))[:, None, :])
            xs, cs, ns, ms = _odd_layer(xs, state, gpre, gpost, w_qkv, w_oz, w_g, b_g, gain, w_out,
                                        tt=dec_seq, lc=dec_seq)
            outs["C_p"].append(cp)
            outs["n_p"].append(np_[:, :, 0, :])
            outs["m_p"].append(mp[:, 0, :nh])
            outs["C_s"].append(cs)
            outs["n_s"].append(ns[:, :, 0, :])
            outs["m_s"].append(ms[:, 0, :nh])
    return (xp, xs,
            jnp.stack(outs["pool_p"]), jnp.stack(outs["k_p"]), jnp.stack(outs["v_p"]),
            jnp.stack(outs["C_p"]), jnp.stack(outs["n_p"]), jnp.stack(outs["m_p"]),
            jnp.stack(outs["pool_s"]), jnp.stack(outs["k_s"]), jnp.stack(outs["v_s"]),
            jnp.stack(outs["C_s"]), jnp.stack(outs["n_s"]), jnp.stack(outs["m_s"]))
```

```python
import functools

import jax
import jax.numpy as jnp
from jax import lax
from jax.experimental import pallas as pl
from jax.experimental.pallas import tpu as pltpu

D_MODEL = 1024
PAST_LEN = 4096
CHUNK = 64
POOL_WINDOWS = (2, 4, 8, 16)
POOL_GROUPS = len(POOL_WINDOWS)
POOL_WIDTH = D_MODEL // 2
POOL_GROUP_DIM = POOL_WIDTH // POOL_GROUPS
POOL_HIST = max(POOL_WINDOWS) - 1
POOL_HIST_ROWS = 16
ATT_WIDTH = D_MODEL // 2
ATT_HEADS = 8
ATT_HEAD_DIM = ATT_WIDTH // ATT_HEADS
HEAD_PAIRS = ATT_HEADS // 2
BAND = 8 * CHUNK
REL_CLIP = 128
REL_TABLE = 2 * REL_CLIP + 1
REL_TABLE_PAD = 384
ML_WIDTH = D_MODEL
ML_HEADS = 4
ML_HEAD_DIM = ML_WIDTH // ML_HEADS
EVEN_IN = 2 * POOL_WIDTH + 4 * ATT_WIDTH
RMS_EPS = 1e-6
LN_EPS = 1e-6

LANES = 128
V7X_VMEM_LIMIT_BYTES = 56 * 1024 * 1024

BF16 = jnp.bfloat16
F32 = jnp.float32


def _dot(a, b):
    return jnp.dot(a, b, preferred_element_type=F32)


def _dot_nt(a, b):
    return lax.dot_general(a, b, (((1,), (1,)), ((), ())), preferred_element_type=F32)


def _dot_tn(a, b):
    return lax.dot_general(a, b, (((0,), (0,)), ((), ())), preferred_element_type=F32)


def _split3(x):
    hi = x.astype(BF16)
    r1 = x - hi.astype(F32)
    mid = r1.astype(BF16)
    lo = (r1 - mid.astype(F32)).astype(BF16)
    return hi, mid, lo


def _rmsnorm(x, g):
    return x * lax.rsqrt(jnp.mean(x * x, axis=-1, keepdims=True) + RMS_EPS) * g


def _silu(x):
    return x * jax.nn.sigmoid(x)


def _bias_kernel(tab_ref, out_ref, *, lc, nb):
    rb = nb * lc
    kb = BAND + rb
    n_dist = rb - 1 + kb
    gw = (n_dist + LANES - 1) // LANES * LANES
    parts = _split3(tab_ref[...])
    dist = lax.broadcasted_iota(jnp.int32, (REL_TABLE_PAD, gw), 1) - (rb - 1)
    ti = lax.broadcasted_iota(jnp.int32, (REL_TABLE_PAD, gw), 0)
    onehot = jnp.where(ti == jnp.clip(BAND - dist, -REL_CLIP, REL_CLIP) + REL_CLIP, 1.0, 0.0).astype(BF16)
    by_dist = _dot(parts[0], onehot) + _dot(parts[1], onehot) + _dot(parts[2], onehot)
    key_chunk = lax.broadcasted_iota(jnp.int32, (ATT_HEADS, kb), 1) // lc

    def row(qi, carry):
        shift = lax.rem(gw - (rb - 1 - qi), gw)
        acc = pltpu.roll(by_dist, shift, axis=1)[:, 0:kb]
        q_chunk = qi // lc
        in_band = (key_chunk >= q_chunk) & (key_chunk <= q_chunk + BAND // lc)
        out_ref[qi] = jnp.where(in_band, acc, -jnp.inf)
        return carry

    lax.fori_loop(0, rb, row, 0)


def _expand_bias(table, lc, nb):
    rb = nb * lc
    kb = BAND + rb
    tab = jnp.pad(table.astype(F32), ((0, 0), (0, REL_TABLE_PAD - REL_TABLE)))
    out = pl.pallas_call(
        functools.partial(_bias_kernel, lc=lc, nb=nb),
        out_shape=jax.ShapeDtypeStruct((rb, ATT_HEADS, kb), F32),
        name="rel_bias_expand",
    )(tab)
    return out.transpose(1, 0, 2).reshape(HEAD_PAIRS, 2 * rb, kb)


def _even_kernel(*refs, tt, rb, has_hist, pos0):
    if has_hist:
        (x_ref, hu_ref, hk_ref, hv_ref, gpre_ref, gpost_ref, win_ref, wmix_ref, scale_ref,
         bias_ref, wout_ref, y_ref, pool_ref, kout_ref, vout_ref,
         uext, kext, vext, q_sc, gate_sc, mixed_sc, s_sc) = refs
    else:
        (x_ref, gpre_ref, gpost_ref, win_ref, wmix_ref, scale_ref,
         bias_ref, wout_ref, y_ref, pool_ref, kout_ref, vout_ref,
         uext, kext, vext, q_sc, gate_sc, mixed_sc, s_sc) = refs
    t = pl.program_id(1)
    hr = POOL_HIST_ROWS
    kb = BAND + rb

    @pl.when(t == 0)
    def _():
        if has_hist:
            uext[0:hr, :] = hu_ref[0]
            kext[0:BAND, :] = hk_ref[0].astype(BF16)
            vext[0:BAND, :] = hv_ref[0].astype(BF16)
        else:
            uext[0:hr, :] = jnp.zeros((hr, POOL_WIDTH), F32)
            kext[0:BAND, :] = jnp.zeros((BAND, ATT_WIDTH), BF16)
            vext[0:BAND, :] = jnp.zeros((BAND, ATT_WIDTH), BF16)

    @pl.when(t > 0)
    def _():
        uext[0:hr, :] = uext[tt:tt + hr, :]
        kext[0:BAND, :] = kext[tt:tt + BAND, :]
        vext[0:BAND, :] = vext[tt:tt + BAND, :]

    x = x_ref[0]
    h = _rmsnorm(x, gpre_ref[...]).astype(BF16)
    p_, a_ = POOL_WIDTH, ATT_WIDTH
    def project(part):
        cols = slice(part * a_, (part + 1) * a_)
        res = _dot(h, win_ref[:, cols])
        if part == 0:
            uext[hr:hr + tt, :] = res
        elif part == 1:
            q_sc[...] = (res * (ATT_HEAD_DIM ** -0.5)).astype(BF16)
        elif part == 2:
            kext[BAND:BAND + tt, :] = res.astype(BF16)
            kout_ref[0] = res
        elif part == 3:
            vext[BAND:BAND + tt, :] = res.astype(BF16)
            vout_ref[0] = res
        else:
            gate_sc[:, (part - 4) * a_:(part - 3) * a_] = _silu(res)

    pos = pos0 + t * tt + lax.broadcasted_iota(jnp.int32, (tt, 1), 0)

    def pool_group(g):
        w = POOL_WINDOWS[g]
        ln = slice(g * POOL_GROUP_DIM, (g + 1) * POOL_GROUP_DIM)
        tok = uext[hr:hr + tt, ln]
        win_sum = tok
        for back in range(1, w):
            win_sum = win_sum + uext[hr - back:hr - back + tt, ln]
        count = jnp.minimum(pos + 1, w).astype(F32)
        pooled = win_sum / count - tok
        mixed = _dot(pooled.astype(BF16), wmix_ref[g]) * scale_ref[:, ln]
        mixed_sc[:, ln] = (mixed * gate_sc[:, ln]).astype(BF16)

    assert p_ == a_ and POOL_GROUPS == 4
    project(0)
    project(4)
    for g, part in enumerate((1, 2, 3, 5)):
        project(part)
        pool_group(g)

    lane = lax.broadcasted_iota(jnp.int32, (rb, LANES), 1)
    even_head = lane < ATT_HEAD_DIM

    def attend(sequence_start):
        units = [(blk, p) for blk in range(tt // rb) for p in range(HEAD_PAIRS)]

        def band(blk):
            r0 = blk * rb
            k0 = BAND if sequence_start else r0
            c0 = BAND - r0 if sequence_start else 0
            return r0, k0, c0, kb - c0

        def stage_scores(i):
            blk, p = units[i]
            r0, k0, c0, kw = band(blk)
            ln = slice(p * LANES, (p + 1) * LANES)
            qp = q_sc[r0:r0 + rb, ln]
            zero = jnp.zeros_like(qp)
            q2 = jnp.concatenate([jnp.where(even_head, qp, zero), jnp.where(even_head, zero, qp)], axis=0)
            s_sc[i % 2, :, 0:kw] = _dot_nt(q2, kext[k0:k0 + kw, ln]) + bias_ref[p, :, c0:c0 + kw]

        def finish(i):
            blk, p = units[i]
            r0, k0, c0, kw = band(blk)
            ln = slice(p * LANES, (p + 1) * LANES)
            row_max = jnp.max(s_sc[i % 2, :, 0:kw], axis=-1, keepdims=True)
            e = jnp.exp(s_sc[i % 2, :, 0:kw] - row_max)
            denom = jnp.sum(e, axis=-1, keepdims=True)
            o2 = _dot(e.astype(BF16), vext[k0:k0 + kw, ln]) / denom
            o = jnp.where(even_head, o2[0:rb], o2[rb:2 * rb])
            mo = slice(POOL_WIDTH + p * LANES, POOL_WIDTH + (p + 1) * LANES)
            mixed_sc[r0:r0 + rb, mo] = (o * gate_sc[r0:r0 + rb, mo]).astype(BF16)

        stage_scores(0)
        for i in range(len(units)):
            if i + 1 < len(units):
                stage_scores(i + 1)
            finish(i)

    if has_hist:
        attend(False)
    else:
        pl.when(t == 0)(functools.partial(attend, True))
        pl.when(t > 0)(functools.partial(attend, False))

    y = _dot(mixed_sc[...], wout_ref[...])
    y_ref[0] = x_ref[0] + _rmsnorm(y, gpost_ref[...])
    pool_ref[0] = uext[tt:tt + hr, :]


def _even_layer(x, hist, gpre, gpost, w_in, w_mix, scale, bias, w_out, *, tt, rb, pos0):
    b, t_len, d = x.shape
    has_hist = hist is not None
    nt = t_len // tt
    assert t_len % tt == 0 and tt % rb == 0
    assert has_hist or rb % LANES == 0, "band start at a sequence start must stay lane aligned in the bias"
    keep = min(BAND, t_len)
    assert keep == tt, "key/value cache rows must be exactly the last time tile"
    assert nt == 1 or tt >= BAND
    kb = BAND + rb

    def whole(shape):
        return pl.BlockSpec(shape, lambda i, j: (0,) * len(shape))

    def per_seq(shape):
        return pl.BlockSpec((1,) + shape, lambda i, j: (i, 0, 0))

    in_specs = [pl.BlockSpec((1, tt, d), lambda i, j: (i, j, 0))]
    args = [x]
    if has_hist:
        in_specs += [per_seq((POOL_HIST_ROWS, POOL_WIDTH)), per_seq((BAND, ATT_WIDTH)), per_seq((BAND, ATT_WIDTH))]
        args += list(hist)
    in_specs += [whole((1, d)), whole((1, d)), whole((d, EVEN_IN)),
                 whole((POOL_GROUPS, POOL_GROUP_DIM, POOL_GROUP_DIM)), whole((1, POOL_WIDTH)),
                 whole((HEAD_PAIRS, 2 * rb, kb)), whole((POOL_WIDTH + ATT_WIDTH, d))]
    args += [gpre, gpost, w_in, w_mix, scale, bias, w_out]
    out_shape = (jax.ShapeDtypeStruct((b, t_len, d), F32),
                 jax.ShapeDtypeStruct((b, POOL_HIST_ROWS, POOL_WIDTH), F32),
                 jax.ShapeDtypeStruct((b, keep, ATT_WIDTH), F32),
                 jax.ShapeDtypeStruct((b, keep, ATT_WIDTH), F32))
    out_specs = (pl.BlockSpec((1, tt, d), lambda i, j: (i, j, 0)),
                 per_seq((POOL_HIST_ROWS, POOL_WIDTH)), per_seq((keep, ATT_WIDTH)), per_seq((keep, ATT_WIDTH)))
    scratch = [pltpu.VMEM((POOL_HIST_ROWS + tt, POOL_WIDTH), F32),
               pltpu.VMEM((BAND + tt, ATT_WIDTH), BF16),
               pltpu.VMEM((BAND + tt, ATT_WIDTH), BF16),
               pltpu.VMEM((tt, ATT_WIDTH), BF16),
               pltpu.VMEM((tt, POOL_WIDTH + ATT_WIDTH), F32),
               pltpu.VMEM((tt, POOL_WIDTH + ATT_WIDTH), BF16),
               pltpu.VMEM((2, 2 * rb, kb), F32)]
    return pl.pallas_call(
        functools.partial(_even_kernel, tt=tt, rb=rb, has_hist=has_hist, pos0=pos0),
        grid=(b, nt), in_specs=in_specs, out_specs=out_specs, out_shape=out_shape,
        scratch_shapes=scratch,
        compiler_params=pltpu.CompilerParams(dimension_semantics=("parallel", "arbitrary"),
                                             vmem_limit_bytes=V7X_VMEM_LIMIT_BYTES),
        name="even_layer_hist" if has_hist else "even_layer",
    )(*args)


def _odd_kernel(x_ref, c0_ref, n0_ref, m0_ref, gpre_ref, gpost_ref, wqkv_ref, woz_ref, wg_ref, bg_ref,
                gain_ref, wout_ref, y_ref, cout_ref, nout_ref, mout_ref,
                q_sc, k_sc, v_sc, oz_sc, mixed_sc, c_sc, n_sc, m_sc, *, tt, lc):
    t = pl.program_id(1)
    n_chunks = tt // lc
    w_, dh, nh = ML_WIDTH, ML_HEAD_DIM, ML_HEADS

    @pl.when(t == 0)
    def _():
        c_sc[...] = c0_ref[0]
        n_sc[...] = n0_ref[0]
        m_sc[...] = m0_ref[0]

    x = x_ref[0]
    h = _rmsnorm(x, gpre_ref[...]).astype(BF16)

    def project(c, part):
        rows = slice(c * lc, (c + 1) * lc)
        hc = h[rows]
        if part == 0:
            q_sc[rows, :] = _dot(hc, wqkv_ref[:, 0:w_]).astype(BF16)
        elif part == 1:
            k_sc[rows, :] = (_dot(hc, wqkv_ref[:, w_:2 * w_]) * (dh ** -0.5)).astype(BF16)
        elif part == 2:
            v_sc[rows, :] = _dot(hc, wqkv_ref[:, 2 * w_:3 * w_]).astype(BF16)
        else:
            oz_sc[rows, :] = (jax.nn.sigmoid(_dot(hc, woz_ref[:, 0:w_]))
                              * _silu(_dot(hc, woz_ref[:, w_:2 * w_])))

    gates = _dot(h, wg_ref[...]) + bg_ref[...]
    project(0, 0)
    ig = gates[:, 0:LANES]
    lf = jax.nn.log_sigmoid(gates[:, LANES:2 * LANES])
    ri = lax.broadcasted_iota(jnp.int32, (tt, tt), 0)
    ci = lax.broadcasted_iota(jnp.int32, (tt, tt), 1)
    tri = jnp.where(((ri // lc) == (ci // lc)) & (ci <= ri), 1.0, 0.0).astype(BF16)
    lf3 = _split3(lf)
    bcol = _dot(tri, lf3[0]) + _dot(tri, lf3[1]) + _dot(tri, lf3[2])
    project(0, 1)
    sel = jnp.where(lax.broadcasted_iota(jnp.int32, (8, LANES), 0)
                    == lax.broadcasted_iota(jnp.int32, (8, LANES), 1), 1.0, 0.0).astype(BF16)
    b3 = _split3(bcol)
    g3 = _split3(ig)
    brow = _dot_nt(sel, b3[0]) + _dot_nt(sel, b3[1]) + _dot_nt(sel, b3[2])
    igrow = _dot_nt(sel, g3[0]) + _dot_nt(sel, g3[1]) + _dot_nt(sel, g3[2])
    project(0, 2)

    causal = (lax.broadcasted_iota(jnp.int32, (lc, lc), 1) <= lax.broadcasted_iota(jnp.int32, (lc, lc), 0))
    m_prev_all = m_sc[...]
    chunk_gates = []
    for c in range(n_chunks):
        rows = slice(c * lc, (c + 1) * lc)
        b_all = bcol[rows]
        b_last_all = b_all[lc - 1:lc]
        g_all = b_last_all - b_all + ig[rows]
        m_new_all = jnp.maximum(b_last_all + m_prev_all, jnp.max(g_all, axis=0, keepdims=True))
        chunk_gates.append(dict(b=b_all, inter=b_all + m_prev_all,
                                decay=jnp.exp(b_last_all + m_prev_all - m_new_all),
                                wgt=jnp.exp(g_all - m_new_all)))
        m_prev_all = m_new_all
    m_sc[...] = m_prev_all

    def unit(c, hd):
        rows = slice(c * lc, (c + 1) * lc)
        ln = slice(hd * dh, (hd + 1) * dh)
        gts = chunk_gates[c]
        b_c = gts["b"][:, hd:hd + 1]
        b_r = brow[hd:hd + 1, rows]
        ig_r = igrow[hd:hd + 1, rows]
        inter = gts["inter"][:, hd:hd + 1]
        dmat = jnp.where(causal, b_c - b_r + ig_r, -jnp.inf)
        m_t = jnp.maximum(inter, jnp.max(dmat, axis=-1, keepdims=True))
        a = jnp.exp(inter - m_t)
        q = q_sc[rows, ln]
        k = k_sc[rows, ln]
        v = v_sc[rows, ln]
        s = _dot_nt(q, k) * jnp.exp(dmat - m_t)
        c_old = c_sc[hd]
        n_old = n_sc[hd]
        num = a * _dot_nt(q, c_old.astype(BF16)) + _dot(s.astype(BF16), v)
        qn = jnp.sum(q.astype(F32) * n_old, axis=-1, keepdims=True)
        den = a * qn + jnp.sum(s, axis=-1, keepdims=True)
        hv = num / jnp.maximum(jnp.abs(den), jnp.exp(-m_t))
        mu = jnp.mean(hv, axis=-1, keepdims=True)
        dlt = hv - mu
        var = jnp.mean(dlt * dlt, axis=-1, keepdims=True)
        hn = dlt * lax.rsqrt(var + LN_EPS) * gain_ref[:, ln]
        mixed_sc[rows, ln] = (hn * oz_sc[rows, ln]).astype(BF16)
        decay = gts["decay"][:, hd:hd + 1]
        wgt = gts["wgt"][:, hd:hd + 1]
        vw = (v.astype(F32) * wgt).astype(BF16)
        c_sc[hd] = decay * c_old + _dot_tn(vw, k)
        n_sc[hd] = decay * n_old + jnp.sum(wgt * k.astype(F32), axis=0, keepdims=True)

    def out_project(c):
        rows = slice(c * lc, (c + 1) * lc)
        y = _dot(mixed_sc[rows, :], wout_ref[...])
        y_ref[0, rows, :] = x_ref[0, rows, :] + _rmsnorm(y, gpost_ref[...])

    n_parts = 4
    project(0, 3)
    for c in range(n_chunks):
        if c > 0:
            out_project(c - 1)
        for hd in range(nh):
            if c + 1 < n_chunks:
                for part in range(hd * n_parts // nh, (hd + 1) * n_parts // nh):
                    project(c + 1, part)
            unit(c, hd)
    out_project(n_chunks - 1)
    cout_ref[0] = c_sc[...]
    nout_ref[0] = n_sc[...]
    mout_ref[0] = m_sc[...]


def _odd_layer(x, state, gpre, gpost, w_qkv, w_oz, w_g, b_g, gain, w_out, *, tt, lc):
    b, t_len, d = x.shape
    nt = t_len // tt
    assert t_len % tt == 0 and tt % lc == 0
    nh, dh = ML_HEADS, ML_HEAD_DIM

    def whole(shape):
        return pl.BlockSpec(shape, lambda i, j: (0,) * len(shape))

    def per_seq(shape):
        return pl.BlockSpec((1,) + shape, lambda i, j: (i,) + (0,) * len(shape))

    st_specs = [per_seq((nh, dh, dh)), per_seq((nh, 1, dh)), per_seq((1, LANES))]
    in_specs = ([pl.BlockSpec((1, tt, d), lambda i, j: (i, j, 0))] + st_specs +
                [whole((1, d)), whole((1, d)), whole((d, 3 * ML_WIDTH)), whole((d, 2 * ML_WIDTH)),
                 whole((d, 2 * LANES)), whole((1, 2 * LANES)), whole((1, ML_WIDTH)), whole((ML_WIDTH, d))])
    out_shape = (jax.ShapeDtypeStruct((b, t_len, d), F32),
                 jax.ShapeDtypeStruct((b, nh, dh, dh), F32),
                 jax.ShapeDtypeStruct((b, nh, 1, dh), F32),
                 jax.ShapeDtypeStruct((b, 1, LANES), F32))
    out_specs = (pl.BlockSpec((1, tt, d), lambda i, j: (i, j, 0)), *st_specs)
    scratch = [pltpu.VMEM((tt, ML_WIDTH), BF16), pltpu.VMEM((tt, ML_WIDTH), BF16), pltpu.VMEM((tt, ML_WIDTH), BF16),
               pltpu.VMEM((tt, ML_WIDTH), F32), pltpu.VMEM((tt, ML_WIDTH), BF16),
               pltpu.VMEM((nh, dh, dh), F32), pltpu.VMEM((nh, 1, dh), F32), pltpu.VMEM((1, LANES), F32)]
    return pl.pallas_call(
        functools.partial(_odd_kernel, tt=tt, lc=lc),
        grid=(b, nt), in_specs=in_specs, out_specs=out_specs, out_shape=out_shape,
        scratch_shapes=scratch,
        compiler_params=pltpu.CompilerParams(dimension_semantics=("parallel", "arbitrary"),
                                             vmem_limit_bytes=V7X_VMEM_LIMIT_BYTES),
        name="odd_layer",
    )(x, *state, gpre, gpost, w_qkv, w_oz, w_g, b_g, gain, w_out)


PROMPT_TILE = 512
ATT_BLOCK_CHUNKS = 2
ML_PROMPT_CHUNK = 256


def kernel(x_prompt, x_sample, cache_pool, cache_k, cache_v, state_C, state_n, state_m, norm_pre, norm_post,
           w_in_even, w_pool_mix, pool_scale, rel_bias, w_out_even, w_in_odd, b_gate_odd, mlstm_norm, w_out_odd):
    depth = norm_pre.shape[0]
    bp, dec_seq = x_prompt.shape[0], x_sample.shape[1]
    bs = x_sample.shape[0]
    nh, dh = ML_HEADS, ML_HEAD_DIM
    xp, xs = x_prompt, x_sample
    outs = {name: [] for name in ("pool_p", "k_p", "v_p", "C_p", "n_p", "m_p",
                                  "pool_s", "k_s", "v_s", "C_s", "n_s", "m_s")}
    for layer in range(depth):
        gpre = norm_pre[layer][None, :]
        gpost = norm_post[layer][None, :]
        if layer % 2 == 0:
            e = layer // 2
            w_in = w_in_even[e].astype(BF16)
            w_mix = w_pool_mix[e].astype(BF16)
            w_out = w_out_even[e].astype(BF16)
            scale = pool_scale[e][None, :]
            bias_p = _expand_bias(rel_bias[e], CHUNK, ATT_BLOCK_CHUNKS)
            bias_s = _expand_bias(rel_bias[e], dec_seq, 1)
            xp, pp, kp, vp = _even_layer(xp, None, gpre, gpost, w_in, w_mix, scale, bias_p, w_out,
                                         tt=PROMPT_TILE, rb=ATT_BLOCK_CHUNKS * CHUNK, pos0=0)
            hist = (jnp.pad(cache_pool[e], ((0, 0), (POOL_HIST_ROWS - POOL_HIST, 0), (0, 0))),
                    cache_k[e].reshape(bs, BAND, ATT_WIDTH), cache_v[e].reshape(bs, BAND, ATT_WIDTH))
            xs, ps, ks, vs = _even_layer(xs, hist, gpre, gpost, w_in, w_mix, scale, bias_s, w_out,
                                         tt=dec_seq, rb=dec_seq, pos0=PAST_LEN)
            outs["pool_p"].append(pp[:, POOL_HIST_ROWS - POOL_HIST:])
            outs["k_p"].append(kp.reshape(bp, -1, ATT_HEADS, ATT_HEAD_DIM))
            outs["v_p"].append(vp.reshape(bp, -1, ATT_HEADS, ATT_HEAD_DIM))
            outs["pool_s"].append(ps[:, POOL_HIST_ROWS - POOL_HIST:])
            outs["k_s"].append(ks.reshape(bs, -1, ATT_HEADS, ATT_HEAD_DIM))
            outs["v_s"].append(vs.reshape(bs, -1, ATT_HEADS, ATT_HEAD_DIM))
        else:
            o = layer // 2
            w = w_in_odd[o]
            w_qkv = w[:, :3 * ML_WIDTH].astype(BF16)
            w_oz = w[:, 3 * ML_WIDTH:5 * ML_WIDTH].astype(BF16)
            pad_h = ((0, 0), (0, LANES - nh))
            w_g = jnp.concatenate([jnp.pad(w[:, 5 * ML_WIDTH:5 * ML_WIDTH + nh], pad_h),
                                   jnp.pad(w[:, 5 * ML_WIDTH + nh:], pad_h)], axis=1).astype(BF16)
            b_g = jnp.concatenate([jnp.pad(b_gate_odd[o][:nh], (0, LANES - nh)),
                                   jnp.pad(b_gate_odd[o][nh:], (0, LANES - nh))])[None, :]
            gain = mlstm_norm[o][None, :]
            w_out = w_out_odd[o].astype(BF16)
            zero_state = (jnp.zeros((bp, nh, dh, dh), F32), jnp.zeros((bp, nh, 1, dh), F32),
                          jnp.zeros((bp, 1, LANES), F32))
            xp, cp, np_, mp = _odd_layer(xp, zero_state, gpre, gpost, w_qkv, w_oz, w_g, b_g, gain, w_out,
                                         tt=PROMPT_TILE, lc=ML_PROMPT_CHUNK)
            state = (state_C[o], state_n[o][:, :, None, :],
                     jnp.pad(state_m[o], ((0, 0), (0, LANES - nh)))[:, None, :])
            xs, cs, ns, ms = _odd_layer(xs, state, gpre, gpost, w_qkv, w_oz, w_g, b_g, gain, w_out,
                                        tt=dec_seq, lc=dec_seq)
            outs["C_p"].append(cp)
            outs["n_p"].append(np_[:, :, 0, :])
            outs["m_p"].append(mp[:, 0, :nh])
            outs["C_s"].append(cs)
            outs["n_s"].append(ns[:, :, 0, :])
            outs["m_s"].append(ms[:, 0, :nh])
    return (xp, xs,
            jnp.stack(outs["pool_p"]), jnp.stack(outs["k_p"]), jnp.stack(outs["v_p"]),
            jnp.stack(outs["C_p"]), jnp.stack(outs["n_p"]), jnp.stack(outs["m_p"]),
            jnp.stack(outs["pool_s"]), jnp.stack(outs["k_s"]), jnp.stack(outs["v_s"]),
            jnp.stack(outs["C_s"]), jnp.stack(outs["n_s"]), jnp.stack(outs["m_s"]))
```

```python
import functools

import jax
import jax.numpy as jnp
from jax import lax
from jax.experimental import pallas as pl
from jax.experimental.pallas import tpu as pltpu

D_MODEL = 1024
PAST_LEN = 4096
CHUNK = 64
POOL_WINDOWS = (2, 4, 8, 16)
POOL_GROUPS = len(POOL_WINDOWS)
POOL_WIDTH = D_MODEL // 2
POOL_GROUP_DIM = POOL_WIDTH // POOL_GROUPS
POOL_HIST = max(POOL_WINDOWS) - 1
POOL_HIST_ROWS = 16
ATT_WIDTH = D_MODEL // 2
ATT_HEADS = 8
ATT_HEAD_DIM = ATT_WIDTH // ATT_HEADS
HEAD_PAIRS = ATT_HEADS // 2
BAND = 8 * CHUNK
REL_CLIP = 128
REL_TABLE = 2 * REL_CLIP + 1
REL_TABLE_PAD = 384
ML_WIDTH = D_MODEL
ML_HEADS = 4
ML_HEAD_DIM = ML_WIDTH // ML_HEADS
EVEN_IN = 2 * POOL_WIDTH + 4 * ATT_WIDTH
RMS_EPS = 1e-6
LN_EPS = 1e-6

LANES = 128
CAST_BLOCK_COLS = 1024
V7X_VMEM_BYTES = 64 * 1024 * 1024
V7X_VMEM_LIMIT_BYTES = V7X_VMEM_BYTES * 7 // 8

BF16 = jnp.bfloat16
F32 = jnp.float32


def _dot(a, b):
    return jnp.dot(a, b, preferred_element_type=F32)


def _dot_nt(a, b):
    return lax.dot_general(a, b, (((1,), (1,)), ((), ())), preferred_element_type=F32)


def _dot_tn(a, b):
    return lax.dot_general(a, b, (((0,), (0,)), ((), ())), preferred_element_type=F32)


def _split3(x):
    hi = x.astype(BF16)
    r1 = x - hi.astype(F32)
    mid = r1.astype(BF16)
    lo = (r1 - mid.astype(F32)).astype(BF16)
    return hi, mid, lo


def _rmsnorm(x, g):
    return x * lax.rsqrt(jnp.mean(x * x, axis=-1, keepdims=True) + RMS_EPS) * g


def _silu(x):
    return x * jax.nn.sigmoid(x)


def _cast_kernel(src_ref, dst_ref):
    dst_ref[...] = src_ref[...].astype(dst_ref.dtype)


def _cast_columns_bf16(w, n_cols):
    rows = w.shape[0]
    block = CAST_BLOCK_COLS
    assert n_cols % block == 0 and n_cols <= w.shape[1]
    spec = pl.BlockSpec((rows, block), lambda j: (0, j))
    return pl.pallas_call(
        _cast_kernel, grid=(n_cols // block,), in_specs=[spec], out_specs=spec,
        out_shape=jax.ShapeDtypeStruct((rows, n_cols), BF16), name="cast_bf16",
    )(w)


def _bias_kernel(tab_ref, out_ref, *, lc, nb):
    rb = nb * lc
    kb = BAND + rb
    n_dist = rb - 1 + kb
    gw = (n_dist + LANES - 1) // LANES * LANES
    parts = _split3(tab_ref[...])
    dist = lax.broadcasted_iota(jnp.int32, (REL_TABLE_PAD, gw), 1) - (rb - 1)
    ti = lax.broadcasted_iota(jnp.int32, (REL_TABLE_PAD, gw), 0)
    onehot = jnp.where(ti == jnp.clip(BAND - dist, -REL_CLIP, REL_CLIP) + REL_CLIP, 1.0, 0.0).astype(BF16)
    by_dist = _dot(parts[0], onehot) + _dot(parts[1], onehot) + _dot(parts[2], onehot)
    key_chunk = lax.broadcasted_iota(jnp.int32, (ATT_HEADS, kb), 1) // lc

    def row(qi, carry):
        shift = lax.rem(gw - (rb - 1 - qi), gw)
        acc = pltpu.roll(by_dist, shift, axis=1)[:, 0:kb]
        q_chunk = qi // lc
        in_band = (key_chunk >= q_chunk) & (key_chunk <= q_chunk + BAND // lc)
        out_ref[qi] = jnp.where(in_band, acc, -jnp.inf)
        return carry

    lax.fori_loop(0, rb, row, 0)


def _expand_bias(table, lc, nb):
    rb = nb * lc
    kb = BAND + rb
    tab = jnp.pad(table.astype(F32), ((0, 0), (0, REL_TABLE_PAD - REL_TABLE)))
    out = pl.pallas_call(
        functools.partial(_bias_kernel, lc=lc, nb=nb),
        out_shape=jax.ShapeDtypeStruct((rb, ATT_HEADS, kb), F32),
        name="rel_bias_expand",
    )(tab)
    return out.transpose(1, 0, 2).reshape(HEAD_PAIRS, 2 * rb, kb)


def _even_kernel(*refs, tt, rb, has_hist, pos0):
    if has_hist:
        (x_ref, hu_ref, hk_ref, hv_ref, gpre_ref, gpost_ref, win_ref, wmix_ref, scale_ref,
         bias_ref, wout_ref, y_ref, pool_ref, kout_ref, vout_ref,
         uext, kext, vext, q_sc, gate_sc, mixed_sc, s_sc) = refs
    else:
        (x_ref, gpre_ref, gpost_ref, win_ref, wmix_ref, scale_ref,
         bias_ref, wout_ref, y_ref, pool_ref, kout_ref, vout_ref,
         uext, kext, vext, q_sc, gate_sc, mixed_sc, s_sc) = refs
    t = pl.program_id(1)
    hr = POOL_HIST_ROWS
    kb = BAND + rb

    @pl.when(t == 0)
    def _():
        if has_hist:
            uext[0:hr, :] = hu_ref[0]
            kext[0:BAND, :] = hk_ref[0].astype(BF16)
            vext[0:BAND, :] = hv_ref[0].astype(BF16)
        else:
            uext[0:hr, :] = jnp.zeros((hr, POOL_WIDTH), F32)
            kext[0:BAND, :] = jnp.zeros((BAND, ATT_WIDTH), BF16)
            vext[0:BAND, :] = jnp.zeros((BAND, ATT_WIDTH), BF16)

    @pl.when(t > 0)
    def _():
        uext[0:hr, :] = uext[tt:tt + hr, :]
        kext[0:BAND, :] = kext[tt:tt + BAND, :]
        vext[0:BAND, :] = vext[tt:tt + BAND, :]

    x = x_ref[0]
    h = _rmsnorm(x, gpre_ref[...]).astype(BF16)
    p_, a_ = POOL_WIDTH, ATT_WIDTH

    def project(part):
        cols = slice(part * a_, (part + 1) * a_)
        res = _dot(h, win_ref[:, cols])
        if part == 0:
            uext[hr:hr + tt, :] = res
        elif part == 1:
            q_sc[...] = (res * (ATT_HEAD_DIM ** -0.5)).astype(BF16)
        elif part == 2:
            kext[BAND:BAND + tt, :] = res.astype(BF16)
            kout_ref[0] = res
        elif part == 3:
            vext[BAND:BAND + tt, :] = res.astype(BF16)
            vout_ref[0] = res
        else:
            gate_sc[:, (part - 4) * a_:(part - 3) * a_] = _silu(res)

    pos = pos0 + t * tt + lax.broadcasted_iota(jnp.int32, (tt, 1), 0)

    def pool_group(g):
        w = POOL_WINDOWS[g]
        ln = slice(g * POOL_GROUP_DIM, (g + 1) * POOL_GROUP_DIM)
        tok = uext[hr:hr + tt, ln]
        win_sum = tok
        for back in range(1, w):
            win_sum = win_sum + uext[hr - back:hr - back + tt, ln]
        count = jnp.minimum(pos + 1, w).astype(F32)
        pooled = win_sum / count - tok
        mixed = _dot(pooled.astype(BF16), wmix_ref[g]) * scale_ref[:, ln]
        mixed_sc[:, ln] = (mixed * gate_sc[:, ln]).astype(BF16)

    assert p_ == a_ and POOL_GROUPS == 4
    project(0)
    project(4)
    for g, part in enumerate((1, 2, 3, 5)):
        project(part)
        pool_group(g)

    lane = lax.broadcasted_iota(jnp.int32, (rb, LANES), 1)
    even_head = lane < ATT_HEAD_DIM

    def attend(sequence_start):
        units = [(blk, p) for blk in range(tt // rb) for p in range(HEAD_PAIRS)]

        def band(blk):
            r0 = blk * rb
            k0 = BAND if sequence_start else r0
            c0 = BAND - r0 if sequence_start else 0
            return r0, k0, c0, kb - c0

        def stage_scores(i):
            blk, p = units[i]
            r0, k0, c0, kw = band(blk)
            ln = slice(p * LANES, (p + 1) * LANES)
            qp = q_sc[r0:r0 + rb, ln]
            zero = jnp.zeros_like(qp)
            q2 = jnp.concatenate([jnp.where(even_head, qp, zero), jnp.where(even_head, zero, qp)], axis=0)
            s_sc[i % 2, :, 0:kw] = _dot_nt(q2, kext[k0:k0 + kw, ln]) + bias_ref[p, :, c0:c0 + kw]

        def finish(i):
            blk, p = units[i]
            r0, k0, c0, kw = band(blk)
            ln = slice(p * LANES, (p + 1) * LANES)
            row_max = jnp.max(s_sc[i % 2, :, 0:kw], axis=-1, keepdims=True)
            e = jnp.exp(s_sc[i % 2, :, 0:kw] - row_max)
            denom = jnp.sum(e, axis=-1, keepdims=True)
            o2 = _dot(e.astype(BF16), vext[k0:k0 + kw, ln]) / denom
            o = jnp.where(even_head, o2[0:rb], o2[rb:2 * rb])
            mo = slice(POOL_WIDTH + p * LANES, POOL_WIDTH + (p + 1) * LANES)
            mixed_sc[r0:r0 + rb, mo] = (o * gate_sc[r0:r0 + rb, mo]).astype(BF16)

        stage_scores(0)
        for i in range(len(units)):
            if i + 1 < len(units):
                stage_scores(i + 1)
            finish(i)

    if has_hist:
        attend(False)
    else:
        pl.when(t == 0)(functools.partial(attend, True))
        pl.when(t > 0)(functools.partial(attend, False))

    y = _dot(mixed_sc[...], wout_ref[...])
    y_ref[0] = x_ref[0] + _rmsnorm(y, gpost_ref[...])
    pool_ref[0] = uext[tt:tt + hr, :]


def _even_layer(x, hist, gpre, gpost, w_in, w_mix, scale, bias, w_out, *, tt, rb, pos0):
    b, t_len, d = x.shape
    has_hist = hist is not None
    nt = t_len // tt
    assert t_len % tt == 0 and tt % rb == 0
    assert has_hist or rb % LANES == 0, "band start at a sequence start must stay lane aligned in the bias"
    keep = min(BAND, t_len)
    assert keep == tt, "key/value cache rows must be exactly the last time tile"
    assert nt == 1 or tt >= BAND
    kb = BAND + rb

    def whole(shape):
        return pl.BlockSpec(shape, lambda i, j: (0,) * len(shape))

    def per_seq(shape):
        return pl.BlockSpec((1,) + shape, lambda i, j: (i, 0, 0))

    in_specs = [pl.BlockSpec((1, tt, d), lambda i, j: (i, j, 0))]
    args = [x]
    if has_hist:
        in_specs += [per_seq((POOL_HIST_ROWS, POOL_WIDTH)), per_seq((BAND, ATT_WIDTH)), per_seq((BAND, ATT_WIDTH))]
        args += list(hist)
    in_specs += [whole((1, d)), whole((1, d)), whole((d, EVEN_IN)),
                 whole((POOL_GROUPS, POOL_GROUP_DIM, POOL_GROUP_DIM)), whole((1, POOL_WIDTH)),
                 whole((HEAD_PAIRS, 2 * rb, kb)), whole((POOL_WIDTH + ATT_WIDTH, d))]
    args += [gpre, gpost, w_in, w_mix, scale, bias, w_out]
    out_shape = (jax.ShapeDtypeStruct((b, t_len, d), F32),
                 jax.ShapeDtypeStruct((b, POOL_HIST_ROWS, POOL_WIDTH), F32),
                 jax.ShapeDtypeStruct((b, keep, ATT_WIDTH), F32),
                 jax.ShapeDtypeStruct((b, keep, ATT_WIDTH), F32))
    out_specs = (pl.BlockSpec((1, tt, d), lambda i, j: (i, j, 0)),
                 per_seq((POOL_HIST_ROWS, POOL_WIDTH)), per_seq((keep, ATT_WIDTH)), per_seq((keep, ATT_WIDTH)))
    scratch = [pltpu.VMEM((POOL_HIST_ROWS + tt, POOL_WIDTH), F32),
               pltpu.VMEM((BAND + tt, ATT_WIDTH), BF16),
               pltpu.VMEM((BAND + tt, ATT_WIDTH), BF16),
               pltpu.VMEM((tt, ATT_WIDTH), BF16),
               pltpu.VMEM((tt, POOL_WIDTH + ATT_WIDTH), F32),
               pltpu.VMEM((tt, POOL_WIDTH + ATT_WIDTH), BF16),
               pltpu.VMEM((2, 2 * rb, kb), F32)]
    return pl.pallas_call(
        functools.partial(_even_kernel, tt=tt, rb=rb, has_hist=has_hist, pos0=pos0),
        grid=(b, nt), in_specs=in_specs, out_specs=out_specs, out_shape=out_shape,
        scratch_shapes=scratch,
        compiler_params=pltpu.CompilerParams(dimension_semantics=("parallel", "arbitrary"),
                                             vmem_limit_bytes=V7X_VMEM_LIMIT_BYTES),
        name="even_layer_hist" if has_hist else "even_layer",
    )(*args)


def _odd_kernel(*refs, tt, lc, has_state):
    if has_state:
        x_ref, c0_ref, n0_ref, m0_ref = refs[0:4]
        refs = refs[4:]
    else:
        x_ref = refs[0]
        refs = refs[1:]
    (gpre_ref, gpost_ref, w_ref, wg_ref, bg_ref, gain_ref, wout_ref,
     y_ref, cout_ref, nout_ref, mout_ref, q_sc, k_sc, v_sc, oz_sc, mixed_sc, c_sc, n_sc, m_sc) = refs
    t = pl.program_id(1)
    n_chunks = tt // lc
    w_, dh, nh = ML_WIDTH, ML_HEAD_DIM, ML_HEADS

    @pl.when(t == 0)
    def _():
        if has_state:
            c_sc[...] = c0_ref[0]
            n_sc[...] = n0_ref[0]
            m_sc[...] = m0_ref[0]
        else:
            c_sc[...] = jnp.zeros(c_sc.shape, F32)
            n_sc[...] = jnp.zeros(n_sc.shape, F32)
            m_sc[...] = jnp.zeros(m_sc.shape, F32)

    h = [_rmsnorm(x_ref[0, c * lc:(c + 1) * lc, :], gpre_ref[...]).astype(BF16) for c in range(n_chunks)]

    def project(c, part):
        rows = slice(c * lc, (c + 1) * lc)
        if part == 0:
            q_sc[rows, :] = _dot(h[c], w_ref[:, 0:w_]).astype(BF16)
        elif part == 1:
            k_sc[rows, :] = (_dot(h[c], w_ref[:, w_:2 * w_]) * (dh ** -0.5)).astype(BF16)
        elif part == 2:
            v_sc[rows, :] = _dot(h[c], w_ref[:, 2 * w_:3 * w_]).astype(BF16)
        else:
            oz_sc[rows, :] = (jax.nn.sigmoid(_dot(h[c], w_ref[:, 3 * w_:4 * w_]))
                              * _silu(_dot(h[c], w_ref[:, 4 * w_:5 * w_])))

    tri = jnp.where(lax.broadcasted_iota(jnp.int32, (lc, lc), 1) <= lax.broadcasted_iota(jnp.int32, (lc, lc), 0),
                    1.0, 0.0).astype(BF16)
    sel = jnp.where(lax.broadcasted_iota(jnp.int32, (8, LANES), 0)
                    == lax.broadcasted_iota(jnp.int32, (8, LANES), 1), 1.0, 0.0).astype(BF16)
    causal = (lax.broadcasted_iota(jnp.int32, (lc, lc), 1) <= lax.broadcasted_iota(jnp.int32, (lc, lc), 0))

    slabs = [(0, part) for part in range(4)]
    m_prev_all = m_sc[...]
    chunk_gates = []
    for c in range(n_chunks):
        gates = _dot(h[c], wg_ref[...]) + bg_ref[...]
        if slabs:
            project(*slabs.pop(0))
        ig = gates[:, 0:LANES]
        lf3 = _split3(jax.nn.log_sigmoid(gates[:, LANES:2 * LANES]))
        b_all = _dot(tri, lf3[0]) + _dot(tri, lf3[1]) + _dot(tri, lf3[2])
        if slabs:
            project(*slabs.pop(0))
        b3 = _split3(b_all)
        g3 = _split3(ig)
        brow = _dot_nt(sel, b3[0]) + _dot_nt(sel, b3[1]) + _dot_nt(sel, b3[2])
        igrow = _dot_nt(sel, g3[0]) + _dot_nt(sel, g3[1]) + _dot_nt(sel, g3[2])
        b_last_all = b_all[lc - 1:lc]
        g_all = b_last_all - b_all + ig
        m_new_all = jnp.maximum(b_last_all + m_prev_all, jnp.max(g_all, axis=0, keepdims=True))
        chunk_gates.append(dict(b=b_all, inter=b_all + m_prev_all, brow=brow, igrow=igrow,
                                decay=jnp.exp(b_last_all + m_prev_all - m_new_all),
                                wgt=jnp.exp(g_all - m_new_all)))
        m_prev_all = m_new_all
    m_sc[...] = m_prev_all
    while slabs:
        project(*slabs.pop(0))

    def unit(c, hd):
        rows = slice(c * lc, (c + 1) * lc)
        ln = slice(hd * dh, (hd + 1) * dh)
        gts = chunk_gates[c]
        b_c = gts["b"][:, hd:hd + 1]
        b_r = gts["brow"][hd:hd + 1, :]
        ig_r = gts["igrow"][hd:hd + 1, :]
        inter = gts["inter"][:, hd:hd + 1]
        dmat = jnp.where(causal, b_c - b_r + ig_r, -jnp.inf)
        m_t = jnp.maximum(inter, jnp.max(dmat, axis=-1, keepdims=True))
        a = jnp.exp(inter - m_t)
        q = q_sc[rows, ln]
        k = k_sc[rows, ln]
        v = v_sc[rows, ln]
        s = _dot_nt(q, k) * jnp.exp(dmat - m_t)
        c_old = c_sc[hd]
        n_old = n_sc[hd]
        num = a * _dot_nt(q, c_old.astype(BF16)) + _dot(s.astype(BF16), v)
        qn = jnp.sum(q.astype(F32) * n_old, axis=-1, keepdims=True)
        den = a * qn + jnp.sum(s, axis=-1, keepdims=True)
        hv = num / jnp.maximum(jnp.abs(den), jnp.exp(-m_t))
        mu = jnp.mean(hv, axis=-1, keepdims=True)
        dlt = hv - mu
        var = jnp.mean(dlt * dlt, axis=-1, keepdims=True)
        hn = dlt * lax.rsqrt(var + LN_EPS) * gain_ref[:, ln]
        mixed_sc[rows, ln] = (hn * oz_sc[rows, ln]).astype(BF16)
        decay = gts["decay"][:, hd:hd + 1]
        wgt = gts["wgt"][:, hd:hd + 1]
        vw = (v.astype(F32) * wgt).astype(BF16)
        c_sc[hd] = decay * c_old + _dot_tn(vw, k)
        n_sc[hd] = decay * n_old + jnp.sum(wgt * k.astype(F32), axis=0, keepdims=True)

    def out_project(c):
        rows = slice(c * lc, (c + 1) * lc)
        y = _dot(mixed_sc[rows, :], wout_ref[...])
        y_ref[0, rows, :] = x_ref[0, rows, :] + _rmsnorm(y, gpost_ref[...])

    n_parts = 4
    for c in range(n_chunks):
        if c > 0:
            out_project(c - 1)
        for hd in range(nh):
            if c + 1 < n_chunks:
                for part in range(hd * n_parts // nh, (hd + 1) * n_parts // nh):
                    project(c + 1, part)
            unit(c, hd)
    out_project(n_chunks - 1)
    cout_ref[0] = c_sc[...]
    nout_ref[0] = n_sc[...]
    mout_ref[0] = m_sc[...]


def _odd_layer(x, state, gpre, gpost, w_main, w_g, b_g, gain, w_out, *, tt, lc):
    b, t_len, d = x.shape
    nt = t_len // tt
    assert t_len % tt == 0 and tt % lc == 0
    nh, dh = ML_HEADS, ML_HEAD_DIM
    has_state = state is not None

    def whole(shape):
        return pl.BlockSpec(shape, lambda i, j: (0,) * len(shape))

    def per_seq(shape):
        return pl.BlockSpec((1,) + shape, lambda i, j: (i,) + (0,) * len(shape))

    st_specs = [per_seq((nh, dh, dh)), per_seq((nh, 1, dh)), per_seq((1, LANES))]
    in_specs = [pl.BlockSpec((1, tt, d), lambda i, j: (i, j, 0))]
    args = [x]
    if has_state:
        in_specs += st_specs
        args += list(state)
    in_specs += [whole((1, d)), whole((1, d)), whole((d, 5 * ML_WIDTH)),
                 whole((d, 2 * LANES)), whole((1, 2 * LANES)), whole((1, ML_WIDTH)), whole((ML_WIDTH, d))]
    args += [gpre, gpost, w_main, w_g, b_g, gain, w_out]
    out_shape = (jax.ShapeDtypeStruct((b, t_len, d), F32),
                 jax.ShapeDtypeStruct((b, nh, dh, dh), F32),
                 jax.ShapeDtypeStruct((b, nh, 1, dh), F32),
                 jax.ShapeDtypeStruct((b, 1, LANES), F32))
    out_specs = (pl.BlockSpec((1, tt, d), lambda i, j: (i, j, 0)), *st_specs)
    scratch = [pltpu.VMEM((tt, ML_WIDTH), BF16), pltpu.VMEM((tt, ML_WIDTH), BF16), pltpu.VMEM((tt, ML_WIDTH), BF16),
               pltpu.VMEM((tt, ML_WIDTH), F32), pltpu.VMEM((tt, ML_WIDTH), BF16),
               pltpu.VMEM((nh, dh, dh), F32), pltpu.VMEM((nh, 1, dh), F32), pltpu.VMEM((1, LANES), F32)]
    return pl.pallas_call(
        functools.partial(_odd_kernel, tt=tt, lc=lc, has_state=has_state),
        grid=(b, nt), in_specs=in_specs, out_specs=out_specs, out_shape=out_shape,
        scratch_shapes=scratch,
        compiler_params=pltpu.CompilerParams(dimension_semantics=("parallel", "arbitrary"),
                                             vmem_limit_bytes=V7X_VMEM_LIMIT_BYTES),
        name="odd_layer_state" if has_state else "odd_layer",
    )(*args)


PROMPT_TILE = 512
ATT_BLOCK_CHUNKS = 2
ML_PROMPT_CHUNK = 256


def kernel(x_prompt, x_sample, cache_pool, cache_k, cache_v, state_C, state_n, state_m, norm_pre, norm_post,
           w_in_even, w_pool_mix, pool_scale, rel_bias, w_out_even, w_in_odd, b_gate_odd, mlstm_norm, w_out_odd):
    depth = norm_pre.shape[0]
    bp, dec_seq = x_prompt.shape[0], x_sample.shape[1]
    bs = x_sample.shape[0]
    nh = ML_HEADS
    xp, xs = x_prompt, x_sample
    outs = {name: [] for name in ("pool_p", "k_p", "v_p", "C_p", "n_p", "m_p",
                                  "pool_s", "k_s", "v_s", "C_s", "n_s", "m_s")}
    for layer in range(depth):
        gpre = norm_pre[layer][None, :]
        gpost = norm_post[layer][None, :]
        if layer % 2 == 0:
            e = layer // 2
            w_in = _cast_columns_bf16(w_in_even[e], EVEN_IN)
            w_mix = w_pool_mix[e].astype(BF16)
            w_out = w_out_even[e].astype(BF16)
            scale = pool_scale[e][None, :]
            bias_p = _expand_bias(rel_bias[e], CHUNK, ATT_BLOCK_CHUNKS)
            bias_s = _expand_bias(rel_bias[e], dec_seq, 1)
            xp, pp, kp, vp = _even_layer(xp, None, gpre, gpost, w_in, w_mix, scale, bias_p, w_out,
                                         tt=PROMPT_TILE, rb=ATT_BLOCK_CHUNKS * CHUNK, pos0=0)
            hist = (jnp.pad(cache_pool[e], ((0, 0), (POOL_HIST_ROWS - POOL_HIST, 0), (0, 0))),
                    cache_k[e].reshape(bs, BAND, ATT_WIDTH), cache_v[e].reshape(bs, BAND, ATT_WIDTH))
            xs, ps, ks, vs = _even_layer(xs, hist, gpre, gpost, w_in, w_mix, scale, bias_s, w_out,
                                         tt=dec_seq, rb=dec_seq, pos0=PAST_LEN)
            outs["pool_p"].append(pp[:, POOL_HIST_ROWS - POOL_HIST:])
            outs["k_p"].append(kp.reshape(bp, -1, ATT_HEADS, ATT_HEAD_DIM))
            outs["v_p"].append(vp.reshape(bp, -1, ATT_HEADS, ATT_HEAD_DIM))
            outs["pool_s"].append(ps[:, POOL_HIST_ROWS - POOL_HIST:])
            outs["k_s"].append(ks.reshape(bs, -1, ATT_HEADS, ATT_HEAD_DIM))
            outs["v_s"].append(vs.reshape(bs, -1, ATT_HEADS, ATT_HEAD_DIM))
        else:
            o = layer // 2
            w = w_in_odd[o]
            w_main = _cast_columns_bf16(w, 5 * ML_WIDTH)
            pad_h = ((0, 0), (0, LANES - nh))
            w_g = jnp.concatenate([jnp.pad(w[:, 5 * ML_WIDTH:5 * ML_WIDTH + nh], pad_h),
                                   jnp.pad(w[:, 5 * ML_WIDTH + nh:], pad_h)], axis=1).astype(BF16)
            b_g = jnp.concatenate([jnp.pad(b_gate_odd[o][:nh], (0, LANES - nh)),
                                   jnp.pad(b_gate_odd[o][nh:], (0, LANES - nh))])[None, :]
            gain = mlstm_norm[o][None, :]
            w_out = w_out_odd[o].astype(BF16)
            xp, cp, np_, mp = _odd_layer(xp, None, gpre, gpost, w_main, w_g, b_g, gain, w_out,
                                         tt=PROMPT_TILE, lc=ML_PROMPT_CHUNK)
            state = (state_C[o], state_n[o][:, :, None, :],
                     jnp.pad(state_m[o], ((0, 0), (0, LANES - nh)))[:, None, :])
            xs, cs, ns, ms = _odd_layer(xs, state, gpre, gpost, w_main, w_g, b_g, gain, w_out,
                                        tt=dec_seq, lc=dec_seq)
            outs["C_p"].append(cp)
            outs["n_p"].append(np_[:, :, 0, :])
            outs["m_p"].append(mp[:, 0, :nh])
            outs["C_s"].append(cs)
            outs["n_s"].append(ns[:, :, 0, :])
            outs["m_s"].append(ms[:, 0, :nh])
    return (xp, xs,
            jnp.stack(outs["pool_p"]), jnp.stack(outs["k_p"]), jnp.stack(outs["v_p"]),
            jnp.stack(outs["C_p"]), jnp.stack(outs["n_p"]), jnp.stack(outs["m_p"]),
            jnp.stack(outs["pool_s"]), jnp.stack(outs["k_s"]), jnp.stack(outs["v_s"]),
            jnp.stack(outs["C_s"]), jnp.stack(outs["n_s"]), jnp.stack(outs["m_s"]))
```

```python
import functools

import jax
import jax.numpy as jnp
from jax import lax
from jax.experimental import pallas as pl
from jax.experimental.pallas import tpu as pltpu

D_MODEL = 1024
PAST_LEN = 4096
CHUNK = 64
POOL_WINDOWS = (2, 4, 8, 16)
POOL_GROUPS = len(POOL_WINDOWS)
POOL_WIDTH = D_MODEL // 2
POOL_GROUP_DIM = POOL_WIDTH // POOL_GROUPS
POOL_HIST = max(POOL_WINDOWS) - 1
POOL_HIST_ROWS = 16
ATT_WIDTH = D_MODEL // 2
ATT_HEADS = 8
ATT_HEAD_DIM = ATT_WIDTH // ATT_HEADS
HEAD_PAIRS = ATT_HEADS // 2
BAND = 8 * CHUNK
REL_CLIP = 128
REL_TABLE = 2 * REL_CLIP + 1
REL_TABLE_PAD = 384
ML_WIDTH = D_MODEL
ML_HEADS = 4
ML_HEAD_DIM = ML_WIDTH // ML_HEADS
EVEN_IN = 2 * POOL_WIDTH + 4 * ATT_WIDTH
RMS_EPS = 1e-6
LN_EPS = 1e-6

LANES = 128
CAST_BLOCK = 1024
V7X_VMEM_BYTES = 64 * 1024 * 1024
V7X_VMEM_LIMIT_BYTES = V7X_VMEM_BYTES * 7 // 8

BF16 = jnp.bfloat16
F32 = jnp.float32


def _dot(a, b):
    return jnp.dot(a, b, preferred_element_type=F32)


def _dot_nt(a, b):
    return lax.dot_general(a, b, (((1,), (1,)), ((), ())), preferred_element_type=F32)


def _dot_tn(a, b):
    return lax.dot_general(a, b, (((0,), (0,)), ((), ())), preferred_element_type=F32)


def _split3(x):
    hi = x.astype(BF16)
    r1 = x - hi.astype(F32)
    mid = r1.astype(BF16)
    lo = (r1 - mid.astype(F32)).astype(BF16)
    return hi, mid, lo


def _rmsnorm(x, g):
    return x * lax.rsqrt(jnp.mean(x * x, axis=-1, keepdims=True) + RMS_EPS) * g


def _silu(x):
    return x * jax.nn.sigmoid(x)


def _cast_kernel(src_ref, dst_ref):
    dst_ref[...] = src_ref[...].astype(dst_ref.dtype)


def _cast_range_bf16(w, start, n, axis, block):
    assert n % block == 0 and start % block == 0 and start + n <= w.shape[axis]
    first = start // block
    if axis == 0:
        in_spec = pl.BlockSpec((block, w.shape[1]), lambda j: (first + j, 0))
        out_spec = pl.BlockSpec((block, w.shape[1]), lambda j: (j, 0))
        shape = (n, w.shape[1])
    else:
        in_spec = pl.BlockSpec((w.shape[0], block), lambda j: (0, first + j))
        out_spec = pl.BlockSpec((w.shape[0], block), lambda j: (0, j))
        shape = (w.shape[0], n)
    return pl.pallas_call(
        _cast_kernel, grid=(n // block,), in_specs=[in_spec], out_specs=out_spec,
        out_shape=jax.ShapeDtypeStruct(shape, BF16), name="cast_bf16",
    )(w)


def _bias_kernel(tab_ref, out_ref, *, lc, nb):
    rb = nb * lc
    kb = BAND + rb
    n_dist = rb - 1 + kb
    gw = (n_dist + LANES - 1) // LANES * LANES
    parts = _split3(tab_ref[...])
    dist = lax.broadcasted_iota(jnp.int32, (REL_TABLE_PAD, gw), 1) - (rb - 1)
    ti = lax.broadcasted_iota(jnp.int32, (REL_TABLE_PAD, gw), 0)
    onehot = jnp.where(ti == jnp.clip(BAND - dist, -REL_CLIP, REL_CLIP) + REL_CLIP, 1.0, 0.0).astype(BF16)
    by_dist = _dot(parts[0], onehot) + _dot(parts[1], onehot) + _dot(parts[2], onehot)
    key_chunk = lax.broadcasted_iota(jnp.int32, (ATT_HEADS, kb), 1) // lc

    def row(qi, carry):
        shift = lax.rem(gw - (rb - 1 - qi), gw)
        acc = pltpu.roll(by_dist, shift, axis=1)[:, 0:kb]
        q_chunk = qi // lc
        in_band = (key_chunk >= q_chunk) & (key_chunk <= q_chunk + BAND // lc)
        out_ref[qi] = jnp.where(in_band, acc, -jnp.inf)
        return carry

    lax.fori_loop(0, rb, row, 0)


def _expand_bias(table, lc, nb):
    rb = nb * lc
    kb = BAND + rb
    tab = jnp.pad(table.astype(F32), ((0, 0), (0, REL_TABLE_PAD - REL_TABLE)))
    out = pl.pallas_call(
        functools.partial(_bias_kernel, lc=lc, nb=nb),
        out_shape=jax.ShapeDtypeStruct((rb, ATT_HEADS, kb), F32),
        name="rel_bias_expand",
    )(tab)
    return out.transpose(1, 0, 2).reshape(HEAD_PAIRS, 2 * rb, kb)


def _even_kernel(*refs, tt, rb, has_hist, pos0):
    if has_hist:
        (x_ref, hu_ref, hk_ref, hv_ref, gpre_ref, gpost_ref, win_ref, wmix_ref, scale_ref,
         bias_ref, wout_ref, y_ref, pool_ref, kout_ref, vout_ref,
         uext, kext, vext, q_sc, gate_sc, mixed_sc, s_sc) = refs
    else:
        (x_ref, gpre_ref, gpost_ref, win_ref, wmix_ref, scale_ref,
         bias_ref, wout_ref, y_ref, pool_ref, kout_ref, vout_ref,
         uext, kext, vext, q_sc, gate_sc, mixed_sc, s_sc) = refs
    t = pl.program_id(1)
    hr = POOL_HIST_ROWS
    kb = BAND + rb

    @pl.when(t == 0)
    def _():
        if has_hist:
            uext[0:hr, :] = hu_ref[0]
            kext[0:BAND, :] = hk_ref[0].astype(BF16)
            vext[0:BAND, :] = hv_ref[0].astype(BF16)
        else:
            uext[0:hr, :] = jnp.zeros((hr, POOL_WIDTH), F32)
            kext[0:BAND, :] = jnp.zeros((BAND, ATT_WIDTH), BF16)
            vext[0:BAND, :] = jnp.zeros((BAND, ATT_WIDTH), BF16)

    @pl.when(t > 0)
    def _():
        uext[0:hr, :] = uext[tt:tt + hr, :]
        kext[0:BAND, :] = kext[tt:tt + BAND, :]
        vext[0:BAND, :] = vext[tt:tt + BAND, :]

    x = x_ref[0]
    h = _rmsnorm(x, gpre_ref[...]).astype(BF16)
    p_, a_ = POOL_WIDTH, ATT_WIDTH

    def project(part):
        cols = slice(part * a_, (part + 1) * a_)
        res = _dot(h, win_ref[:, cols])
        if part == 0:
            uext[hr:hr + tt, :] = res
        elif part == 1:
            q_sc[...] = (res * (ATT_HEAD_DIM ** -0.5)).astype(BF16)
        elif part == 2:
            kext[BAND:BAND + tt, :] = res.astype(BF16)
            kout_ref[0] = res
        elif part == 3:
            vext[BAND:BAND + tt, :] = res.astype(BF16)
            vout_ref[0] = res
        else:
            gate_sc[:, (part - 4) * a_:(part - 3) * a_] = _silu(res)

    pos = pos0 + t * tt + lax.broadcasted_iota(jnp.int32, (tt, 1), 0)

    def pool_group(g):
        w = POOL_WINDOWS[g]
        ln = slice(g * POOL_GROUP_DIM, (g + 1) * POOL_GROUP_DIM)
        tok = uext[hr:hr + tt, ln]
        win_sum = tok
        for back in range(1, w):
            win_sum = win_sum + uext[hr - back:hr - back + tt, ln]
        count = jnp.minimum(pos + 1, w).astype(F32)
        pooled = win_sum / count - tok
        mixed = _dot(pooled.astype(BF16), wmix_ref[g]) * scale_ref[:, ln]
        mixed_sc[:, ln] = (mixed * gate_sc[:, ln]).astype(BF16)

    assert p_ == a_ and POOL_GROUPS == 4
    project(0)
    project(4)
    for g, part in enumerate((1, 2, 3, 5)):
        project(part)
        pool_group(g)

    lane = lax.broadcasted_iota(jnp.int32, (rb, LANES), 1)
    even_head = lane < ATT_HEAD_DIM

    def attend(sequence_start):
        units = [(blk, p) for blk in range(tt // rb) for p in range(HEAD_PAIRS)]

        def band(blk):
            r0 = blk * rb
            k0 = BAND if sequence_start else r0
            c0 = BAND - r0 if sequence_start else 0
            return r0, k0, c0, kb - c0

        def stage_scores(i):
            blk, p = units[i]
            r0, k0, c0, kw = band(blk)
            ln = slice(p * LANES, (p + 1) * LANES)
            qp = q_sc[r0:r0 + rb, ln]
            zero = jnp.zeros_like(qp)
            q2 = jnp.concatenate([jnp.where(even_head, qp, zero), jnp.where(even_head, zero, qp)], axis=0)
            s_sc[i % 2, :, 0:kw] = _dot_nt(q2, kext[k0:k0 + kw, ln]) + bias_ref[p, :, c0:c0 + kw]

        def finish(i):
            blk, p = units[i]
            r0, k0, c0, kw = band(blk)
            ln = slice(p * LANES, (p + 1) * LANES)
            row_max = jnp.max(s_sc[i % 2, :, 0:kw], axis=-1, keepdims=True)
            e = jnp.exp(s_sc[i % 2, :, 0:kw] - row_max)
            denom = jnp.sum(e, axis=-1, keepdims=True)
            o2 = _dot(e.astype(BF16), vext[k0:k0 + kw, ln]) / denom
            o = jnp.where(even_head, o2[0:rb], o2[rb:2 * rb])
            mo = slice(POOL_WIDTH + p * LANES, POOL_WIDTH + (p + 1) * LANES)
            mixed_sc[r0:r0 + rb, mo] = (o * gate_sc[r0:r0 + rb, mo]).astype(BF16)

        stage_scores(0)
        for i in range(len(units)):
            if i + 1 < len(units):
                stage_scores(i + 1)
            finish(i)

    if has_hist:
        attend(False)
    else:
        pl.when(t == 0)(functools.partial(attend, True))
        pl.when(t > 0)(functools.partial(attend, False))

    y = _dot(mixed_sc[...], wout_ref[...])
    y_ref[0] = x_ref[0] + _rmsnorm(y, gpost_ref[...])
    pool_ref[0] = uext[tt:tt + hr, :]


def _even_layer(x, hist, gpre, gpost, w_in, w_mix, scale, bias, w_out, *, tt, rb, pos0):
    b, t_len, d = x.shape
    has_hist = hist is not None
    nt = t_len // tt
    assert t_len % tt == 0 and tt % rb == 0
    assert has_hist or rb % LANES == 0, "band start at a sequence start must stay lane aligned in the bias"
    keep = min(BAND, t_len)
    assert keep == tt, "key/value cache rows must be exactly the last time tile"
    assert nt == 1 or tt >= BAND
    kb = BAND + rb

    def whole(shape):
        return pl.BlockSpec(shape, lambda i, j: (0,) * len(shape))

    def per_seq(shape):
        return pl.BlockSpec((1,) + shape, lambda i, j: (i, 0, 0))

    in_specs = [pl.BlockSpec((1, tt, d), lambda i, j: (i, j, 0))]
    args = [x]
    if has_hist:
        in_specs += [per_seq((POOL_HIST_ROWS, POOL_WIDTH)), per_seq((BAND, ATT_WIDTH)), per_seq((BAND, ATT_WIDTH))]
        args += list(hist)
    in_specs += [whole((1, d)), whole((1, d)), whole((d, EVEN_IN)),
                 whole((POOL_GROUPS, POOL_GROUP_DIM, POOL_GROUP_DIM)), whole((1, POOL_WIDTH)),
                 whole((HEAD_PAIRS, 2 * rb, kb)), whole((POOL_WIDTH + ATT_WIDTH, d))]
    args += [gpre, gpost, w_in, w_mix, scale, bias, w_out]
    out_shape = (jax.ShapeDtypeStruct((b, t_len, d), F32),
                 jax.ShapeDtypeStruct((b, POOL_HIST_ROWS, POOL_WIDTH), F32),
                 jax.ShapeDtypeStruct((b, keep, ATT_WIDTH), F32),
                 jax.ShapeDtypeStruct((b, keep, ATT_WIDTH), F32))
    out_specs = (pl.BlockSpec((1, tt, d), lambda i, j: (i, j, 0)),
                 per_seq((POOL_HIST_ROWS, POOL_WIDTH)), per_seq((keep, ATT_WIDTH)), per_seq((keep, ATT_WIDTH)))
    scratch = [pltpu.VMEM((POOL_HIST_ROWS + tt, POOL_WIDTH), F32),
               pltpu.VMEM((BAND + tt, ATT_WIDTH), BF16),
               pltpu.VMEM((BAND + tt, ATT_WIDTH), BF16),
               pltpu.VMEM((tt, ATT_WIDTH), BF16),
               pltpu.VMEM((tt, POOL_WIDTH + ATT_WIDTH), F32),
               pltpu.VMEM((tt, POOL_WIDTH + ATT_WIDTH), BF16),
               pltpu.VMEM((2, 2 * rb, kb), F32)]
    return pl.pallas_call(
        functools.partial(_even_kernel, tt=tt, rb=rb, has_hist=has_hist, pos0=pos0),
        grid=(b, nt), in_specs=in_specs, out_specs=out_specs, out_shape=out_shape,
        scratch_shapes=scratch,
        compiler_params=pltpu.CompilerParams(dimension_semantics=("parallel", "arbitrary"),
                                             vmem_limit_bytes=V7X_VMEM_LIMIT_BYTES),
        name="even_layer_hist" if has_hist else "even_layer",
    )(*args)


def _odd_kernel(*refs, tt, lc, has_state):
    if has_state:
        x_ref, c0_ref, n0_ref, m0_ref = refs[0:4]
        refs = refs[4:]
    else:
        x_ref = refs[0]
        refs = refs[1:]
    (gpre_ref, gpost_ref, w_ref, wg_ref, bg_ref, gain_ref, wout_ref,
     y_ref, cout_ref, nout_ref, mout_ref, q_sc, k_sc, v_sc, oz_sc, mixed_sc, c_sc, n_sc, m_sc) = refs
    t = pl.program_id(1)
    n_chunks = tt // lc
    w_, dh, nh = ML_WIDTH, ML_HEAD_DIM, ML_HEADS

    @pl.when(t == 0)
    def _():
        if has_state:
            c_sc[...] = c0_ref[0]
            n_sc[...] = n0_ref[0]
            m_sc[...] = m0_ref[0]
        else:
            c_sc[...] = jnp.zeros(c_sc.shape, F32)
            n_sc[...] = jnp.zeros(n_sc.shape, F32)
            m_sc[...] = jnp.zeros(m_sc.shape, F32)

    h = [_rmsnorm(x_ref[0, c * lc:(c + 1) * lc, :], gpre_ref[...]).astype(BF16) for c in range(n_chunks)]

    def project(c, part):
        rows = slice(c * lc, (c + 1) * lc)
        if part == 0:
            q_sc[rows, :] = _dot_nt(h[c], w_ref[0:w_, :]).astype(BF16)
        elif part == 1:
            k_sc[rows, :] = (_dot_nt(h[c], w_ref[w_:2 * w_, :]) * (dh ** -0.5)).astype(BF16)
        elif part == 2:
            v_sc[rows, :] = _dot_nt(h[c], w_ref[2 * w_:3 * w_, :]).astype(BF16)
        else:
            oz_sc[rows, :] = (jax.nn.sigmoid(_dot_nt(h[c], w_ref[3 * w_:4 * w_, :]))
                              * _silu(_dot_nt(h[c], w_ref[4 * w_:5 * w_, :])))

    tri = jnp.where(lax.broadcasted_iota(jnp.int32, (lc, lc), 1) <= lax.broadcasted_iota(jnp.int32, (lc, lc), 0),
                    1.0, 0.0).astype(BF16)
    sel = jnp.where(lax.broadcasted_iota(jnp.int32, (8, LANES), 0)
                    == lax.broadcasted_iota(jnp.int32, (8, LANES), 1), 1.0, 0.0).astype(BF16)
    causal = (lax.broadcasted_iota(jnp.int32, (lc, lc), 1) <= lax.broadcasted_iota(jnp.int32, (lc, lc), 0))

    slabs = [(0, part) for part in range(4)]
    m_prev_all = m_sc[...]
    chunk_gates = []
    for c in range(n_chunks):
        gates = _dot_nt(h[c], wg_ref[...]) + bg_ref[...]
        if slabs:
            project(*slabs.pop(0))
        ig = gates[:, 0:LANES]
        lf3 = _split3(jax.nn.log_sigmoid(gates[:, LANES:2 * LANES]))
        b_all = _dot(tri, lf3[0]) + _dot(tri, lf3[1]) + _dot(tri, lf3[2])
        if slabs:
            project(*slabs.pop(0))
        b3 = _split3(b_all)
        g3 = _split3(ig)
        brow = _dot_nt(sel, b3[0]) + _dot_nt(sel, b3[1]) + _dot_nt(sel, b3[2])
        igrow = _dot_nt(sel, g3[0]) + _dot_nt(sel, g3[1]) + _dot_nt(sel, g3[2])
        b_last_all = b_all[lc - 1:lc]
        g_all = b_last_all - b_all + ig
        m_new_all = jnp.maximum(b_last_all + m_prev_all, jnp.max(g_all, axis=0, keepdims=True))
        chunk_gates.append(dict(b=b_all, inter=b_all + m_prev_all, brow=brow, igrow=igrow,
                                decay=jnp.exp(b_last_all + m_prev_all - m_new_all),
                                wgt=jnp.exp(g_all - m_new_all)))
        m_prev_all = m_new_all
    m_sc[...] = m_prev_all
    while slabs:
        project(*slabs.pop(0))

    def unit(c, hd):
        rows = slice(c * lc, (c + 1) * lc)
        ln = slice(hd * dh, (hd + 1) * dh)
        gts = chunk_gates[c]
        b_c = gts["b"][:, hd:hd + 1]
        b_r = gts["brow"][hd:hd + 1, :]
        ig_r = gts["igrow"][hd:hd + 1, :]
        inter = gts["inter"][:, hd:hd + 1]
        dmat = jnp.where(causal, b_c - b_r + ig_r, -jnp.inf)
        m_t = jnp.maximum(inter, jnp.max(dmat, axis=-1, keepdims=True))
        a = jnp.exp(inter - m_t)
        q = q_sc[rows, ln]
        k = k_sc[rows, ln]
        v = v_sc[rows, ln]
        s = _dot_nt(q, k) * jnp.exp(dmat - m_t)
        c_old = c_sc[hd]
        n_old = n_sc[hd]
        num = a * _dot_nt(q, c_old.astype(BF16)) + _dot(s.astype(BF16), v)
        qn = jnp.sum(q.astype(F32) * n_old, axis=-1, keepdims=True)
        den = a * qn + jnp.sum(s, axis=-1, keepdims=True)
        hv = num / jnp.maximum(jnp.abs(den), jnp.exp(-m_t))
        mu = jnp.mean(hv, axis=-1, keepdims=True)
        dlt = hv - mu
        var = jnp.mean(dlt * dlt, axis=-1, keepdims=True)
        hn = dlt * lax.rsqrt(var + LN_EPS) * gain_ref[:, ln]
        mixed_sc[rows, ln] = (hn * oz_sc[rows, ln]).astype(BF16)
        decay = gts["decay"][:, hd:hd + 1]
        wgt = gts["wgt"][:, hd:hd + 1]
        vw = (v.astype(F32) * wgt).astype(BF16)
        c_sc[hd] = decay * c_old + _dot_tn(vw, k)
        n_sc[hd] = decay * n_old + jnp.sum(wgt * k.astype(F32), axis=0, keepdims=True)

    def out_project(c):
        rows = slice(c * lc, (c + 1) * lc)
        y = _dot(mixed_sc[rows, :], wout_ref[...])
        y_ref[0, rows, :] = x_ref[0, rows, :] + _rmsnorm(y, gpost_ref[...])

    n_parts = 4
    for c in range(n_chunks):
        if c > 0:
            out_project(c - 1)
        for hd in range(nh):
            if c + 1 < n_chunks:
                for part in range(hd * n_parts // nh, (hd + 1) * n_parts // nh):
                    project(c + 1, part)
            unit(c, hd)
    out_project(n_chunks - 1)
    cout_ref[0] = c_sc[...]
    nout_ref[0] = n_sc[...]
    mout_ref[0] = m_sc[...]


def _odd_layer(x, state, gpre, gpost, w_main, w_g, b_g, gain, w_out, *, tt, lc):
    b, t_len, d = x.shape
    nt = t_len // tt
    assert t_len % tt == 0 and tt % lc == 0
    nh, dh = ML_HEADS, ML_HEAD_DIM
    has_state = state is not None

    def whole(shape):
        return pl.BlockSpec(shape, lambda i, j: (0,) * len(shape))

    def per_seq(shape):
        return pl.BlockSpec((1,) + shape, lambda i, j: (i,) + (0,) * len(shape))

    st_specs = [per_seq((nh, dh, dh)), per_seq((nh, 1, dh)), per_seq((1, LANES))]
    in_specs = [pl.BlockSpec((1, tt, d), lambda i, j: (i, j, 0))]
    args = [x]
    if has_state:
        in_specs += st_specs
        args += list(state)
    in_specs += [whole((1, d)), whole((1, d)), whole((5 * ML_WIDTH, d)),
                 whole((2 * LANES, d)), whole((1, 2 * LANES)), whole((1, ML_WIDTH)), whole((ML_WIDTH, d))]
    args += [gpre, gpost, w_main, w_g, b_g, gain, w_out]
    out_shape = (jax.ShapeDtypeStruct((b, t_len, d), F32),
                 jax.ShapeDtypeStruct((b, nh, dh, dh), F32),
                 jax.ShapeDtypeStruct((b, nh, 1, dh), F32),
                 jax.ShapeDtypeStruct((b, 1, LANES), F32))
    out_specs = (pl.BlockSpec((1, tt, d), lambda i, j: (i, j, 0)), *st_specs)
    scratch = [pltpu.VMEM((tt, ML_WIDTH), BF16), pltpu.VMEM((tt, ML_WIDTH), BF16), pltpu.VMEM((tt, ML_WIDTH), BF16),
               pltpu.VMEM((tt, ML_WIDTH), F32), pltpu.VMEM((tt, ML_WIDTH), BF16),
               pltpu.VMEM((nh, dh, dh), F32), pltpu.VMEM((nh, 1, dh), F32), pltpu.VMEM((1, LANES), F32)]
    return pl.pallas_call(
        functools.partial(_odd_kernel, tt=tt, lc=lc, has_state=has_state),
        grid=(b, nt), in_specs=in_specs, out_specs=out_specs, out_shape=out_shape,
        scratch_shapes=scratch,
        compiler_params=pltpu.CompilerParams(dimension_semantics=("parallel", "arbitrary"),
                                             vmem_limit_bytes=V7X_VMEM_LIMIT_BYTES),
        name="odd_layer_state" if has_state else "odd_layer",
    )(*args)


PROMPT_TILE = 512
ATT_BLOCK_CHUNKS = 2
ML_PROMPT_CHUNK = 256


def kernel(x_prompt, x_sample, cache_pool, cache_k, cache_v, state_C, state_n, state_m, norm_pre, norm_post,
           w_in_even, w_pool_mix, pool_scale, rel_bias, w_out_even, w_in_odd, b_gate_odd, mlstm_norm, w_out_odd):
    depth = norm_pre.shape[0]
    bp, dec_seq = x_prompt.shape[0], x_sample.shape[1]
    bs = x_sample.shape[0]
    nh = ML_HEADS
    xp, xs = x_prompt, x_sample
    outs = {name: [] for name in ("pool_p", "k_p", "v_p", "C_p", "n_p", "m_p",
                                  "pool_s", "k_s", "v_s", "C_s", "n_s", "m_s")}
    for layer in range(depth):
        gpre = norm_pre[layer][None, :]
        gpost = norm_post[layer][None, :]
        if layer % 2 == 0:
            e = layer // 2
            w_in = _cast_range_bf16(w_in_even[e], 0, EVEN_IN, axis=1, block=CAST_BLOCK)
            w_mix = w_pool_mix[e].astype(BF16)
            w_out = w_out_even[e].astype(BF16)
            scale = pool_scale[e][None, :]
            bias_p = _expand_bias(rel_bias[e], CHUNK, ATT_BLOCK_CHUNKS)
            bias_s = _expand_bias(rel_bias[e], dec_seq, 1)
            xp, pp, kp, vp = _even_layer(xp, None, gpre, gpost, w_in, w_mix, scale, bias_p, w_out,
                                         tt=PROMPT_TILE, rb=ATT_BLOCK_CHUNKS * CHUNK, pos0=0)
            hist = (jnp.pad(cache_pool[e], ((0, 0), (POOL_HIST_ROWS - POOL_HIST, 0), (0, 0))),
                    cache_k[e].reshape(bs, BAND, ATT_WIDTH), cache_v[e].reshape(bs, BAND, ATT_WIDTH))
            xs, ps, ks, vs = _even_layer(xs, hist, gpre, gpost, w_in, w_mix, scale, bias_s, w_out,
                                         tt=dec_seq, rb=dec_seq, pos0=PAST_LEN)
            outs["pool_p"].append(pp[:, POOL_HIST_ROWS - POOL_HIST:])
            outs["k_p"].append(kp.reshape(bp, -1, ATT_HEADS, ATT_HEAD_DIM))
            outs["v_p"].append(vp.reshape(bp, -1, ATT_HEADS, ATT_HEAD_DIM))
            outs["pool_s"].append(ps[:, POOL_HIST_ROWS - POOL_HIST:])
            outs["k_s"].append(ks.reshape(bs, -1, ATT_HEADS, ATT_HEAD_DIM))
            outs["v_s"].append(vs.reshape(bs, -1, ATT_HEADS, ATT_HEAD_DIM))
        else:
            o = layer // 2
            wt = jnp.swapaxes(w_in_odd[o], 0, 1)
            w_main = _cast_range_bf16(wt, 0, 5 * ML_WIDTH, axis=0, block=CAST_BLOCK)
            w_gates = _cast_range_bf16(wt, 5 * ML_WIDTH, 2 * nh, axis=0, block=2 * nh)
            pad_h = ((0, LANES - nh), (0, 0))
            w_g = jnp.concatenate([jnp.pad(w_gates[:nh], pad_h), jnp.pad(w_gates[nh:], pad_h)], axis=0)
            b_g = jnp.concatenate([jnp.pad(b_gate_odd[o][:nh], (0, LANES - nh)),
                                   jnp.pad(b_gate_odd[o][nh:], (0, LANES - nh))])[None, :]
            gain = mlstm_norm[o][None, :]
            w_out = w_out_odd[o].astype(BF16)
            xp, cp, np_, mp = _odd_layer(xp, None, gpre, gpost, w_main, w_g, b_g, gain, w_out,
                                         tt=PROMPT_TILE, lc=ML_PROMPT_CHUNK)
            state = (state_C[o], state_n[o][:, :, None, :],
                     jnp.pad(state_m[o], ((0, 0), (0, LANES - nh)))[:, None, :])
            xs, cs, ns, ms = _odd_layer(xs, state, gpre, gpost, w_main, w_g, b_g, gain, w_out,
                                        tt=dec_seq, lc=dec_seq)
            outs["C_p"].append(cp)
            outs["n_p"].append(np_[:, :, 0, :])
            outs["m_p"].append(mp[:, 0, :nh])
            outs["C_s"].append(cs)
            outs["n_s"].append(ns[:, :, 0, :])
            outs["m_s"].append(ms[:, 0, :nh])
    return (xp, xs,
            jnp.stack(outs["pool_p"]), jnp.stack(outs["k_p"]), jnp.stack(outs["v_p"]),
            jnp.stack(outs["C_p"]), jnp.stack(outs["n_p"]), jnp.stack(outs["m_p"]),
            jnp.stack(outs["pool_s"]), jnp.stack(outs["k_s"]), jnp.stack(outs["v_s"]),
            jnp.stack(outs["C_s"]), jnp.stack(outs["n_s"]), jnp.stack(outs["m_s"]))
```

```python
import functools

import jax
import jax.numpy as jnp
from jax import lax
from jax.experimental import pallas as pl
from jax.experimental.pallas import tpu as pltpu

D_MODEL = 1024
PAST_LEN = 4096
CHUNK = 64
POOL_WINDOWS = (2, 4, 8, 16)
POOL_GROUPS = len(POOL_WINDOWS)
POOL_WIDTH = D_MODEL // 2
POOL_GROUP_DIM = POOL_WIDTH // POOL_GROUPS
POOL_HIST = max(POOL_WINDOWS) - 1
POOL_HIST_ROWS = 16
ATT_WIDTH = D_MODEL // 2
ATT_HEADS = 8
ATT_HEAD_DIM = ATT_WIDTH // ATT_HEADS
HEAD_PAIRS = ATT_HEADS // 2
BAND = 8 * CHUNK
REL_CLIP = 128
REL_TABLE = 2 * REL_CLIP + 1
REL_TABLE_PAD = 384
ML_WIDTH = D_MODEL
ML_HEADS = 4
ML_HEAD_DIM = ML_WIDTH // ML_HEADS
EVEN_IN = 2 * POOL_WIDTH + 4 * ATT_WIDTH
RMS_EPS = 1e-6
LN_EPS = 1e-6

LANES = 128
CAST_BLOCK = 1024
V7X_VMEM_BYTES = 64 * 1024 * 1024
V7X_VMEM_LIMIT_BYTES = V7X_VMEM_BYTES * 7 // 8

BF16 = jnp.bfloat16
F32 = jnp.float32


def _dot(a, b):
    return jnp.dot(a, b, preferred_element_type=F32)


def _dot_nt(a, b):
    return lax.dot_general(a, b, (((1,), (1,)), ((), ())), preferred_element_type=F32)


def _dot_tn(a, b):
    return lax.dot_general(a, b, (((0,), (0,)), ((), ())), preferred_element_type=F32)


def _split3(x):
    hi = x.astype(BF16)
    r1 = x - hi.astype(F32)
    mid = r1.astype(BF16)
    lo = (r1 - mid.astype(F32)).astype(BF16)
    return hi, mid, lo


def _rmsnorm(x, g):
    return x * lax.rsqrt(jnp.mean(x * x, axis=-1, keepdims=True) + RMS_EPS) * g


def _silu(x):
    return x * jax.nn.sigmoid(x)


def _cast_kernel(src_ref, dst_ref):
    dst_ref[...] = src_ref[...].astype(dst_ref.dtype)


def _cast_range_bf16(w, start, n, axis, block):
    assert n % block == 0 and start % block == 0 and start + n <= w.shape[axis]
    first = start // block
    if axis == 0:
        in_spec = pl.BlockSpec((block, w.shape[1]), lambda j: (first + j, 0))
        out_spec = pl.BlockSpec((block, w.shape[1]), lambda j: (j, 0))
        shape = (n, w.shape[1])
    else:
        in_spec = pl.BlockSpec((w.shape[0], block), lambda j: (0, first + j))
        out_spec = pl.BlockSpec((w.shape[0], block), lambda j: (0, j))
        shape = (w.shape[0], n)
    return pl.pallas_call(
        _cast_kernel, grid=(n // block,), in_specs=[in_spec], out_specs=out_spec,
        out_shape=jax.ShapeDtypeStruct(shape, BF16), name="cast_bf16",
    )(w)


def _bias_kernel(tab_ref, out_ref, *, lc, nb):
    rb = nb * lc
    kb = BAND + rb
    n_dist = rb - 1 + kb
    gw = (n_dist + LANES - 1) // LANES * LANES
    parts = _split3(tab_ref[...])
    dist = lax.broadcasted_iota(jnp.int32, (REL_TABLE_PAD, gw), 1) - (rb - 1)
    ti = lax.broadcasted_iota(jnp.int32, (REL_TABLE_PAD, gw), 0)
    onehot = jnp.where(ti == jnp.clip(BAND - dist, -REL_CLIP, REL_CLIP) + REL_CLIP, 1.0, 0.0).astype(BF16)
    by_dist = _dot(parts[0], onehot) + _dot(parts[1], onehot) + _dot(parts[2], onehot)
    key_chunk = lax.broadcasted_iota(jnp.int32, (ATT_HEADS, kb), 1) // lc

    def row(qi, carry):
        shift = lax.rem(gw - (rb - 1 - qi), gw)
        acc = pltpu.roll(by_dist, shift, axis=1)[:, 0:kb]
        q_chunk = qi // lc
        in_band = (key_chunk >= q_chunk) & (key_chunk <= q_chunk + BAND // lc)
        out_ref[qi] = jnp.where(in_band, acc, -jnp.inf)
        return carry

    lax.fori_loop(0, rb, row, 0)


def _expand_bias(table, lc, nb):
    rb = nb * lc
    kb = BAND + rb
    tab = jnp.pad(table.astype(F32), ((0, 0), (0, REL_TABLE_PAD - REL_TABLE)))
    out = pl.pallas_call(
        functools.partial(_bias_kernel, lc=lc, nb=nb),
        out_shape=jax.ShapeDtypeStruct((rb, ATT_HEADS, kb), F32),
        name="rel_bias_expand",
    )(tab)
    return out.transpose(1, 0, 2).reshape(HEAD_PAIRS, 2 * rb, kb)


def _even_kernel(*refs, tt, rb, ns, has_hist, pos0):
    if has_hist:
        (x_ref, hu_ref, hk_ref, hv_ref, gpre_ref, gpost_ref, win_ref, wmix_ref, scale_ref,
         bias_ref, wout_ref, y_ref, pool_ref, kout_ref, vout_ref,
         uext, kext, vext, q_sc, gate_sc, mixed_sc, s_sc) = refs
    else:
        (x_ref, gpre_ref, gpost_ref, win_ref, wmix_ref, scale_ref,
         bias_ref, wout_ref, y_ref, pool_ref, kout_ref, vout_ref,
         uext, kext, vext, q_sc, gate_sc, mixed_sc, s_sc) = refs
    t = pl.program_id(1)
    hr = POOL_HIST_ROWS
    kb = BAND + rb
    seqs = range(ns)

    def seq_rows(s):
        return slice(s * tt, (s + 1) * tt)

    @pl.when(t == 0)
    def _():
        for s in seqs:
            if has_hist:
                uext[s, 0:hr, :] = hu_ref[s]
                kext[s, 0:BAND, :] = hk_ref[s].astype(BF16)
                vext[s, 0:BAND, :] = hv_ref[s].astype(BF16)
            else:
                uext[s, 0:hr, :] = jnp.zeros((hr, POOL_WIDTH), F32)
                kext[s, 0:BAND, :] = jnp.zeros((BAND, ATT_WIDTH), BF16)
                vext[s, 0:BAND, :] = jnp.zeros((BAND, ATT_WIDTH), BF16)

    @pl.when(t > 0)
    def _():
        for s in seqs:
            uext[s, 0:hr, :] = uext[s, tt:tt + hr, :]
            kext[s, 0:BAND, :] = kext[s, tt:tt + BAND, :]
            vext[s, 0:BAND, :] = vext[s, tt:tt + BAND, :]

    def load_x():
        return x_ref[0] if ns == 1 else jnp.concatenate([x_ref[s] for s in seqs], axis=0)

    h = _rmsnorm(load_x(), gpre_ref[...]).astype(BF16)
    p_, a_ = POOL_WIDTH, ATT_WIDTH

    def project(part):
        cols = slice(part * a_, (part + 1) * a_)
        res = _dot(h, win_ref[:, cols])
        if part == 0:
            for s in seqs:
                uext[s, hr:hr + tt, :] = res[seq_rows(s)]
        elif part == 1:
            q_sc[...] = (res * (ATT_HEAD_DIM ** -0.5)).astype(BF16)
        elif part == 2:
            for s in seqs:
                kext[s, BAND:BAND + tt, :] = res[seq_rows(s)].astype(BF16)
                kout_ref[s] = res[seq_rows(s)]
        elif part == 3:
            for s in seqs:
                vext[s, BAND:BAND + tt, :] = res[seq_rows(s)].astype(BF16)
                vout_ref[s] = res[seq_rows(s)]
        else:
            gate_sc[:, (part - 4) * a_:(part - 3) * a_] = _silu(res)

    pos = pos0 + t * tt + lax.broadcasted_iota(jnp.int32, (tt, 1), 0)

    def pool_group(g):
        w = POOL_WINDOWS[g]
        ln = slice(g * POOL_GROUP_DIM, (g + 1) * POOL_GROUP_DIM)
        count = jnp.minimum(pos + 1, w).astype(F32)
        pooled = []
        for s in seqs:
            tok = uext[s, hr:hr + tt, ln]
            win_sum = tok
            for back in range(1, w):
                win_sum = win_sum + uext[s, hr - back:hr - back + tt, ln]
            pooled.append(win_sum / count - tok)
        pooled = pooled[0] if ns == 1 else jnp.concatenate(pooled, axis=0)
        mixed = _dot(pooled.astype(BF16), wmix_ref[g]) * scale_ref[:, ln]
        mixed_sc[:, ln] = (mixed * gate_sc[:, ln]).astype(BF16)

    assert p_ == a_ and POOL_GROUPS == 4
    project(0)
    project(4)
    for g, part in enumerate((1, 2, 3, 5)):
        project(part)
        pool_group(g)

    lane = lax.broadcasted_iota(jnp.int32, (rb, LANES), 1)
    even_head = lane < ATT_HEAD_DIM

    def attend(sequence_start):
        units = [(s, blk, p) for s in seqs for blk in range(tt // rb) for p in range(HEAD_PAIRS)]

        def band(blk):
            r0 = blk * rb
            k0 = BAND if sequence_start else r0
            c0 = BAND - r0 if sequence_start else 0
            return r0, k0, c0, kb - c0

        def stage_scores(i):
            s, blk, p = units[i]
            r0, k0, c0, kw = band(blk)
            ln = slice(p * LANES, (p + 1) * LANES)
            qp = q_sc[s * tt + r0:s * tt + r0 + rb, ln]
            zero = jnp.zeros_like(qp)
            q2 = jnp.concatenate([jnp.where(even_head, qp, zero), jnp.where(even_head, zero, qp)], axis=0)
            s_sc[i % 2, :, 0:kw] = _dot_nt(q2, kext[s, k0:k0 + kw, ln]) + bias_ref[p, :, c0:c0 + kw]

        def finish(i):
            s, blk, p = units[i]
            r0, k0, c0, kw = band(blk)
            ln = slice(p * LANES, (p + 1) * LANES)
            row_max = jnp.max(s_sc[i % 2, :, 0:kw], axis=-1, keepdims=True)
            e = jnp.exp(s_sc[i % 2, :, 0:kw] - row_max)
            denom = jnp.sum(e, axis=-1, keepdims=True)
            o2 = _dot(e.astype(BF16), vext[s, k0:k0 + kw, ln]) / denom
            o = jnp.where(even_head, o2[0:rb], o2[rb:2 * rb])
            rows = slice(s * tt + r0, s * tt + r0 + rb)
            mo = slice(POOL_WIDTH + p * LANES, POOL_WIDTH + (p + 1) * LANES)
            mixed_sc[rows, mo] = (o * gate_sc[rows, mo]).astype(BF16)

        stage_scores(0)
        for i in range(len(units)):
            if i + 1 < len(units):
                stage_scores(i + 1)
            finish(i)

    if has_hist:
        attend(False)
    else:
        pl.when(t == 0)(functools.partial(attend, True))
        pl.when(t > 0)(functools.partial(attend, False))

    y = load_x() + _rmsnorm(_dot(mixed_sc[...], wout_ref[...]), gpost_ref[...])
    for s in seqs:
        y_ref[s] = y[seq_rows(s)]
        pool_ref[s] = uext[s, tt:tt + hr, :]


def _even_layer(x, hist, gpre, gpost, w_in, w_mix, scale, bias, w_out, *, tt, rb, ns, pos0):
    b, t_len, d = x.shape
    has_hist = hist is not None
    nt = t_len // tt
    assert t_len % tt == 0 and tt % rb == 0 and b % ns == 0
    assert has_hist or rb % LANES == 0, "band start at a sequence start must stay lane aligned in the bias"
    keep = min(BAND, t_len)
    assert keep == tt, "key/value cache rows must be exactly the last time tile"
    assert nt == 1 or tt >= BAND
    kb = BAND + rb

    def whole(shape):
        return pl.BlockSpec(shape, lambda i, j: (0,) * len(shape))

    def per_seq(shape):
        return pl.BlockSpec((ns,) + shape, lambda i, j: (i, 0, 0))

    in_specs = [pl.BlockSpec((ns, tt, d), lambda i, j: (i, j, 0))]
    args = [x]
    if has_hist:
        in_specs += [per_seq((POOL_HIST_ROWS, POOL_WIDTH)), per_seq((BAND, ATT_WIDTH)), per_seq((BAND, ATT_WIDTH))]
        args += list(hist)
    in_specs += [whole((1, d)), whole((1, d)), whole((d, EVEN_IN)),
                 whole((POOL_GROUPS, POOL_GROUP_DIM, POOL_GROUP_DIM)), whole((1, POOL_WIDTH)),
                 whole((HEAD_PAIRS, 2 * rb, kb)), whole((POOL_WIDTH + ATT_WIDTH, d))]
    args += [gpre, gpost, w_in, w_mix, scale, bias, w_out]
    out_shape = (jax.ShapeDtypeStruct((b, t_len, d), F32),
                 jax.ShapeDtypeStruct((b, POOL_HIST_ROWS, POOL_WIDTH), F32),
                 jax.ShapeDtypeStruct((b, keep, ATT_WIDTH), F32),
                 jax.ShapeDtypeStruct((b, keep, ATT_WIDTH), F32))
    out_specs = (pl.BlockSpec((ns, tt, d), lambda i, j: (i, j, 0)),
                 per_seq((POOL_HIST_ROWS, POOL_WIDTH)), per_seq((keep, ATT_WIDTH)), per_seq((keep, ATT_WIDTH)))
    scratch = [pltpu.VMEM((ns, POOL_HIST_ROWS + tt, POOL_WIDTH), F32),
               pltpu.VMEM((ns, BAND + tt, ATT_WIDTH), BF16),
               pltpu.VMEM((ns, BAND + tt, ATT_WIDTH), BF16),
               pltpu.VMEM((ns * tt, ATT_WIDTH), BF16),
               pltpu.VMEM((ns * tt, POOL_WIDTH + ATT_WIDTH), F32),
               pltpu.VMEM((ns * tt, POOL_WIDTH + ATT_WIDTH), BF16),
               pltpu.VMEM((2, 2 * rb, kb), F32)]
    return pl.pallas_call(
        functools.partial(_even_kernel, tt=tt, rb=rb, ns=ns, has_hist=has_hist, pos0=pos0),
        grid=(b // ns, nt), in_specs=in_specs, out_specs=out_specs, out_shape=out_shape,
        scratch_shapes=scratch,
        compiler_params=pltpu.CompilerParams(dimension_semantics=("parallel", "arbitrary"),
                                             vmem_limit_bytes=V7X_VMEM_LIMIT_BYTES),
        name="even_layer_hist" if has_hist else "even_layer",
    )(*args)


def _odd_kernel(*refs, tt, lc, ns, has_state):
    if has_state:
        x_ref, c0_ref, n0_ref, m0_ref = refs[0:4]
        refs = refs[4:]
    else:
        x_ref = refs[0]
        refs = refs[1:]
    (gpre_ref, gpost_ref, w_ref, wg_ref, bg_ref, gain_ref, wout_ref,
     y_ref, cout_ref, nout_ref, mout_ref, q_sc, k_sc, v_sc, oz_sc, mixed_sc, c_sc, n_sc, m_sc) = refs
    t = pl.program_id(1)
    w_, dh, nh = ML_WIDTH, ML_HEAD_DIM, ML_HEADS
    if ns == 1:
        groups = [[(0, c * lc)] for c in range(tt // lc)]
    else:
        assert tt == lc
        groups = [[(s, s * tt) for s in range(ns)]]
    n_groups = len(groups)
    g_rows = lc * len(groups[0])

    def group_rows(g):
        return slice(g * g_rows, (g + 1) * g_rows)

    @pl.when(t == 0)
    def _():
        if has_state:
            c_sc[...] = c0_ref[...]
            n_sc[...] = n0_ref[...]
            m_sc[...] = m0_ref[...]
        else:
            c_sc[...] = jnp.zeros(c_sc.shape, F32)
            n_sc[...] = jnp.zeros(n_sc.shape, F32)
            m_sc[...] = jnp.zeros(m_sc.shape, F32)

    def x_rows(g):
        if ns == 1:
            return x_ref[0, group_rows(g), :]
        return jnp.concatenate([x_ref[s] for s in range(ns)], axis=0)

    h = [_rmsnorm(x_rows(g), gpre_ref[...]).astype(BF16) for g in range(n_groups)]

    def project(g, part):
        rows = group_rows(g)
        if part == 0:
            q_sc[rows, :] = _dot_nt(h[g], w_ref[0:w_, :]).astype(BF16)
        elif part == 1:
            k_sc[rows, :] = (_dot_nt(h[g], w_ref[w_:2 * w_, :]) * (dh ** -0.5)).astype(BF16)
        elif part == 2:
            v_sc[rows, :] = _dot_nt(h[g], w_ref[2 * w_:3 * w_, :]).astype(BF16)
        else:
            oz_sc[rows, :] = (jax.nn.sigmoid(_dot_nt(h[g], w_ref[3 * w_:4 * w_, :]))
                              * _silu(_dot_nt(h[g], w_ref[4 * w_:5 * w_, :])))

    tri = jnp.where(lax.broadcasted_iota(jnp.int32, (lc, lc), 1) <= lax.broadcasted_iota(jnp.int32, (lc, lc), 0),
                    1.0, 0.0).astype(BF16)
    sel = jnp.where(lax.broadcasted_iota(jnp.int32, (8, LANES), 0)
                    == lax.broadcasted_iota(jnp.int32, (8, LANES), 1), 1.0, 0.0).astype(BF16)
    causal = (lax.broadcasted_iota(jnp.int32, (lc, lc), 1) <= lax.broadcasted_iota(jnp.int32, (lc, lc), 0))

    slabs = [(0, part) for part in range(4)]
    m_prev = [m_sc[s] for s in range(ns)]
    seg_gates = {}
    for g, segments in enumerate(groups):
        gates = _dot_nt(h[g], wg_ref[...]) + bg_ref[...]
        if slabs:
            project(*slabs.pop(0))
        for s, r0 in segments:
            lo = r0 - g * g_rows
            ig = gates[lo:lo + lc, 0:LANES]
            lf3 = _split3(jax.nn.log_sigmoid(gates[lo:lo + lc, LANES:2 * LANES]))
            b_all = _dot(tri, lf3[0]) + _dot(tri, lf3[1]) + _dot(tri, lf3[2])
            if slabs and (s, r0) == segments[0]:
                project(*slabs.pop(0))
            b3 = _split3(b_all)
            g3 = _split3(ig)
            brow = _dot_nt(sel, b3[0]) + _dot_nt(sel, b3[1]) + _dot_nt(sel, b3[2])
            igrow = _dot_nt(sel, g3[0]) + _dot_nt(sel, g3[1]) + _dot_nt(sel, g3[2])
            b_last_all = b_all[lc - 1:lc]
            g_all = b_last_all - b_all + ig
            m_new = jnp.maximum(b_last_all + m_prev[s], jnp.max(g_all, axis=0, keepdims=True))
            seg_gates[(s, r0)] = dict(b=b_all, inter=b_all + m_prev[s], brow=brow, igrow=igrow,
                                      decay=jnp.exp(b_last_all + m_prev[s] - m_new),
                                      wgt=jnp.exp(g_all - m_new))
            m_prev[s] = m_new
    for s in range(ns):
        m_sc[s] = m_prev[s]
    while slabs:
        project(*slabs.pop(0))

    def unit(seg, hd):
        s, r0 = seg
        rows = slice(r0, r0 + lc)
        ln = slice(hd * dh, (hd + 1) * dh)
        gts = seg_gates[seg]
        b_c = gts["b"][:, hd:hd + 1]
        b_r = gts["brow"][hd:hd + 1, :]
        ig_r = gts["igrow"][hd:hd + 1, :]
        inter = gts["inter"][:, hd:hd + 1]
        dmat = jnp.where(causal, b_c - b_r + ig_r, -jnp.inf)
        m_t = jnp.maximum(inter, jnp.max(dmat, axis=-1, keepdims=True))
        a = jnp.exp(inter - m_t)
        q = q_sc[rows, ln]
        k = k_sc[rows, ln]
        v = v_sc[rows, ln]
        sc = _dot_nt(q, k) * jnp.exp(dmat - m_t)
        c_old = c_sc[s, hd]
        n_old = n_sc[s, hd]
        num = a * _dot_nt(q, c_old.astype(BF16)) + _dot(sc.astype(BF16), v)
        qn = jnp.sum(q.astype(F32) * n_old, axis=-1, keepdims=True)
        den = a * qn + jnp.sum(sc, axis=-1, keepdims=True)
        hv = num / jnp.maximum(jnp.abs(den), jnp.exp(-m_t))
        mu = jnp.mean(hv, axis=-1, keepdims=True)
        dlt = hv - mu
        var = jnp.mean(dlt * dlt, axis=-1, keepdims=True)
        hn = dlt * lax.rsqrt(var + LN_EPS) * gain_ref[:, ln]
        mixed_sc[rows, ln] = (hn * oz_sc[rows, ln]).astype(BF16)
        decay = gts["decay"][:, hd:hd + 1]
        wgt = gts["wgt"][:, hd:hd + 1]
        vw = (v.astype(F32) * wgt).astype(BF16)
        c_sc[s, hd] = decay * c_old + _dot_tn(vw, k)
        n_sc[s, hd] = decay * n_old + jnp.sum(wgt * k.astype(F32), axis=0, keepdims=True)

    def out_project(g):
        res = x_rows(g) + _rmsnorm(_dot(mixed_sc[group_rows(g), :], wout_ref[...]), gpost_ref[...])
        if ns == 1:
            y_ref[0, group_rows(g), :] = res
        else:
            for s in range(ns):
                y_ref[s] = res[s * tt:(s + 1) * tt]

    n_parts = 4
    for g, segments in enumerate(groups):
        if g > 0:
            out_project(g - 1)
        for i, (seg, hd) in enumerate((seg, hd) for seg in segments for hd in range(nh)):
            if g + 1 < n_groups and i < n_parts:
                project(g + 1, i)
            unit(seg, hd)
    out_project(n_groups - 1)
    cout_ref[...] = c_sc[...]
    nout_ref[...] = n_sc[...]
    mout_ref[...] = m_sc[...]


def _odd_layer(x, state, gpre, gpost, w_main, w_g, b_g, gain, w_out, *, tt, lc, ns):
    b, t_len, d = x.shape
    nt = t_len // tt
    assert t_len % tt == 0 and tt % lc == 0 and b % ns == 0
    nh, dh = ML_HEADS, ML_HEAD_DIM
    has_state = state is not None

    def whole(shape):
        return pl.BlockSpec(shape, lambda i, j: (0,) * len(shape))

    def per_seq(shape):
        return pl.BlockSpec((ns,) + shape, lambda i, j: (i,) + (0,) * len(shape))

    st_specs = [per_seq((nh, dh, dh)), per_seq((nh, 1, dh)), per_seq((1, LANES))]
    in_specs = [pl.BlockSpec((ns, tt, d), lambda i, j: (i, j, 0))]
    args = [x]
    if has_state:
        in_specs += st_specs
        args += list(state)
    in_specs += [whole((1, d)), whole((1, d)), whole((5 * ML_WIDTH, d)),
                 whole((2 * LANES, d)), whole((1, 2 * LANES)), whole((1, ML_WIDTH)), whole((ML_WIDTH, d))]
    args += [gpre, gpost, w_main, w_g, b_g, gain, w_out]
    out_shape = (jax.ShapeDtypeStruct((b, t_len, d), F32),
                 jax.ShapeDtypeStruct((b, nh, dh, dh), F32),
                 jax.ShapeDtypeStruct((b, nh, 1, dh), F32),
                 jax.ShapeDtypeStruct((b, 1, LANES), F32))
    out_specs = (pl.BlockSpec((ns, tt, d), lambda i, j: (i, j, 0)), *st_specs)
    rows = ns * tt
    scratch = [pltpu.VMEM((rows, ML_WIDTH), BF16), pltpu.VMEM((rows, ML_WIDTH), BF16),
               pltpu.VMEM((rows, ML_WIDTH), BF16), pltpu.VMEM((rows, ML_WIDTH), F32),
               pltpu.VMEM((rows, ML_WIDTH), BF16),
               pltpu.VMEM((ns, nh, dh, dh), F32), pltpu.VMEM((ns, nh, 1, dh), F32),
               pltpu.VMEM((ns, 1, LANES), F32)]
    return pl.pallas_call(
        functools.partial(_odd_kernel, tt=tt, lc=lc, ns=ns, has_state=has_state),
        grid=(b // ns, nt), in_specs=in_specs, out_specs=out_specs, out_shape=out_shape,
        scratch_shapes=scratch,
        compiler_params=pltpu.CompilerParams(dimension_semantics=("parallel", "arbitrary"),
                                             vmem_limit_bytes=V7X_VMEM_LIMIT_BYTES),
        name="odd_layer_state" if has_state else "odd_layer",
    )(*args)


PROMPT_TILE = 512
ATT_BLOCK_CHUNKS = 2
ML_PROMPT_CHUNK = 256
SAMPLE_SEQS_PER_STEP = 4


def kernel(x_prompt, x_sample, cache_pool, cache_k, cache_v, state_C, state_n, state_m, norm_pre, norm_post,
           w_in_even, w_pool_mix, pool_scale, rel_bias, w_out_even, w_in_odd, b_gate_odd, mlstm_norm, w_out_odd):
    depth = norm_pre.shape[0]
    bp, dec_seq = x_prompt.shape[0], x_sample.shape[1]
    bs = x_sample.shape[0]
    nh = ML_HEADS
    xp, xs = x_prompt, x_sample
    outs = {name: [] for name in ("pool_p", "k_p", "v_p", "C_p", "n_p", "m_p",
                                  "pool_s", "k_s", "v_s", "C_s", "n_s", "m_s")}
    for layer in range(depth):
        gpre = norm_pre[layer][None, :]
        gpost = norm_post[layer][None, :]
        if layer % 2 == 0:
            e = layer // 2
            w_in = _cast_range_bf16(w_in_even[e], 0, EVEN_IN, axis=1, block=CAST_BLOCK)
            w_mix = w_pool_mix[e].astype(BF16)
            w_out = w_out_even[e].astype(BF16)
            scale = pool_scale[e][None, :]
            bias_p = _expand_bias(rel_bias[e], CHUNK, ATT_BLOCK_CHUNKS)
            bias_s = _expand_bias(rel_bias[e], dec_seq, 1)
            xp, pp, kp, vp = _even_layer(xp, None, gpre, gpost, w_in, w_mix, scale, bias_p, w_out,
                                         tt=PROMPT_TILE, rb=ATT_BLOCK_CHUNKS * CHUNK, ns=1, pos0=0)
            hist = (jnp.pad(cache_pool[e], ((0, 0), (POOL_HIST_ROWS - POOL_HIST, 0), (0, 0))),
                    cache_k[e].reshape(bs, BAND, ATT_WIDTH), cache_v[e].reshape(bs, BAND, ATT_WIDTH))
            xs, ps, ks, vs = _even_layer(xs, hist, gpre, gpost, w_in, w_mix, scale, bias_s, w_out,
                                         tt=dec_seq, rb=dec_seq, ns=SAMPLE_SEQS_PER_STEP, pos0=PAST_LEN)
            outs["pool_p"].append(pp[:, POOL_HIST_ROWS - POOL_HIST:])
            outs["k_p"].append(kp.reshape(bp, -1, ATT_HEADS, ATT_HEAD_DIM))
            outs["v_p"].append(vp.reshape(bp, -1, ATT_HEADS, ATT_HEAD_DIM))
            outs["pool_s"].append(ps[:, POOL_HIST_ROWS - POOL_HIST:])
            outs["k_s"].append(ks.reshape(bs, -1, ATT_HEADS, ATT_HEAD_DIM))
            outs["v_s"].append(vs.reshape(bs, -1, ATT_HEADS, ATT_HEAD_DIM))
        else:
            o = layer // 2
            wt = jnp.swapaxes(w_in_odd[o], 0, 1)
            w_main = _cast_range_bf16(wt, 0, 5 * ML_WIDTH, axis=0, block=CAST_BLOCK)
            w_gates = _cast_range_bf16(wt, 5 * ML_WIDTH, 2 * nh, axis=0, block=2 * nh)
            pad_h = ((0, LANES - nh), (0, 0))
            w_g = jnp.concatenate([jnp.pad(w_gates[:nh], pad_h), jnp.pad(w_gates[nh:], pad_h)], axis=0)
            b_g = jnp.concatenate([jnp.pad(b_gate_odd[o][:nh], (0, LANES - nh)),
                                   jnp.pad(b_gate_odd[o][nh:], (0, LANES - nh))])[None, :]
            gain = mlstm_norm[o][None, :]
            w_out = w_out_odd[o].astype(BF16)
            xp, cp, np_, mp = _odd_layer(xp, None, gpre, gpost, w_main, w_g, b_g, gain, w_out,
                                         tt=PROMPT_TILE, lc=ML_PROMPT_CHUNK, ns=1)
            state = (state_C[o], state_n[o][:, :, None, :],
                     jnp.pad(state_m[o], ((0, 0), (0, LANES - nh)))[:, None, :])
            xs, cs, ns, ms = _odd_layer(xs, state, gpre, gpost, w_main, w_g, b_g, gain, w_out,
                                        tt=dec_seq, lc=dec_seq, ns=SAMPLE_SEQS_PER_STEP)
            outs["C_p"].append(cp)
            outs["n_p"].append(np_[:, :, 0, :])
            outs["m_p"].append(mp[:, 0, :nh])
            outs["C_s"].append(cs)
            outs["n_s"].append(ns[:, :, 0, :])
            outs["m_s"].append(ms[:, 0, :nh])
    return (xp, xs,
            jnp.stack(outs["pool_p"]), jnp.stack(outs["k_p"]), jnp.stack(outs["v_p"]),
            jnp.stack(outs["C_p"]), jnp.stack(outs["n_p"]), jnp.stack(outs["m_p"]),
            jnp.stack(outs["pool_s"]), jnp.stack(outs["k_s"]), jnp.stack(outs["v_s"]),
            jnp.stack(outs["C_s"]), jnp.stack(outs["n_s"]), jnp.stack(outs["m_s"]))
```

```python
import functools

import jax
import jax.numpy as jnp
from jax import lax
from jax.experimental import pallas as pl
from jax.experimental.pallas import tpu as pltpu

D_MODEL = 1024
PAST_LEN = 4096
CHUNK = 64
POOL_WINDOWS = (2, 4, 8, 16)
POOL_GROUPS = len(POOL_WINDOWS)
POOL_WIDTH = D_MODEL // 2
POOL_GROUP_DIM = POOL_WIDTH // POOL_GROUPS
POOL_HIST = max(POOL_WINDOWS) - 1
POOL_HIST_ROWS = 16
ATT_WIDTH = D_MODEL // 2
ATT_HEADS = 8
ATT_HEAD_DIM = ATT_WIDTH // ATT_HEADS
HEAD_PAIRS = ATT_HEADS // 2
BAND = 8 * CHUNK
REL_CLIP = 128
REL_TABLE = 2 * REL_CLIP + 1
REL_TABLE_PAD = 384
ML_WIDTH = D_MODEL
ML_HEADS = 4
ML_HEAD_DIM = ML_WIDTH // ML_HEADS
EVEN_IN = 2 * POOL_WIDTH + 4 * ATT_WIDTH
RMS_EPS = 1e-6
LN_EPS = 1e-6
LOG2_E = 1.4426950408889634

LANES = 128
CAST_BLOCK = 1024
V7X_VMEM_BYTES = 64 * 1024 * 1024
V7X_VMEM_LIMIT_BYTES = V7X_VMEM_BYTES * 7 // 8

BF16 = jnp.bfloat16
F32 = jnp.float32


def _dot(a, b):
    return jnp.dot(a, b, preferred_element_type=F32)


def _dot_nt(a, b):
    return lax.dot_general(a, b, (((1,), (1,)), ((), ())), preferred_element_type=F32)


def _dot_tn(a, b):
    return lax.dot_general(a, b, (((0,), (0,)), ((), ())), preferred_element_type=F32)


def _split3(x):
    hi = x.astype(BF16)
    r1 = x - hi.astype(F32)
    mid = r1.astype(BF16)
    lo = (r1 - mid.astype(F32)).astype(BF16)
    return hi, mid, lo


def _rmsnorm(x, g):
    return x * lax.rsqrt(jnp.mean(x * x, axis=-1, keepdims=True) + RMS_EPS) * g


def _silu(x):
    return x * jax.nn.sigmoid(x)


def _cast_kernel(src_ref, dst_ref):
    dst_ref[...] = src_ref[...].astype(dst_ref.dtype)


def _cast_range_bf16(w, start, n, axis, block):
    assert n % block == 0 and start % block == 0 and start + n <= w.shape[axis]
    first = start // block
    if axis == 0:
        in_spec = pl.BlockSpec((block, w.shape[1]), lambda j: (first + j, 0))
        out_spec = pl.BlockSpec((block, w.shape[1]), lambda j: (j, 0))
        shape = (n, w.shape[1])
    else:
        in_spec = pl.BlockSpec((w.shape[0], block), lambda j: (0, first + j))
        out_spec = pl.BlockSpec((w.shape[0], block), lambda j: (0, j))
        shape = (w.shape[0], n)
    return pl.pallas_call(
        _cast_kernel, grid=(n // block,), in_specs=[in_spec], out_specs=out_spec,
        out_shape=jax.ShapeDtypeStruct(shape, BF16), name="cast_bf16",
    )(w)


def _bias_kernel(tab_ref, out_ref, *, lc, nb):
    rb = nb * lc
    kb = BAND + rb
    n_dist = rb - 1 + kb
    gw = (n_dist + LANES - 1) // LANES * LANES
    parts = _split3(tab_ref[...])
    dist = lax.broadcasted_iota(jnp.int32, (REL_TABLE_PAD, gw), 1) - (rb - 1)
    ti = lax.broadcasted_iota(jnp.int32, (REL_TABLE_PAD, gw), 0)
    onehot = jnp.where(ti == jnp.clip(BAND - dist, -REL_CLIP, REL_CLIP) + REL_CLIP, 1.0, 0.0).astype(BF16)
    by_dist = _dot(parts[0], onehot) + _dot(parts[1], onehot) + _dot(parts[2], onehot)
    key_chunk = lax.broadcasted_iota(jnp.int32, (ATT_HEADS, kb), 1) // lc

    def row(qi, carry):
        shift = lax.rem(gw - (rb - 1 - qi), gw)
        acc = pltpu.roll(by_dist, shift, axis=1)[:, 0:kb]
        q_chunk = qi // lc
        in_band = (key_chunk >= q_chunk) & (key_chunk <= q_chunk + BAND // lc)
        out_ref[qi] = jnp.where(in_band, acc * LOG2_E, -jnp.inf)
        return carry

    lax.fori_loop(0, rb, row, 0)


def _expand_bias(table, lc, nb):
    rb = nb * lc
    kb = BAND + rb
    tab = jnp.pad(table.astype(F32), ((0, 0), (0, REL_TABLE_PAD - REL_TABLE)))
    out = pl.pallas_call(
        functools.partial(_bias_kernel, lc=lc, nb=nb),
        out_shape=jax.ShapeDtypeStruct((rb, ATT_HEADS, kb), F32),
        name="rel_bias_expand",
    )(tab)
    return out.transpose(1, 0, 2).reshape(HEAD_PAIRS, 2 * rb, kb)


def _even_kernel(*refs, tt, rb, ns, has_hist, pos0):
    if has_hist:
        (x_ref, hu_ref, hk_ref, hv_ref, gpre_ref, gpost_ref, win_ref, wmix_ref, scale_ref,
         bias_ref, wout_ref, y_ref, pool_ref, kout_ref, vout_ref,
         uext, kext, vext, q_sc, gate_sc, mixed_sc, s_sc) = refs
    else:
        (x_ref, gpre_ref, gpost_ref, win_ref, wmix_ref, scale_ref,
         bias_ref, wout_ref, y_ref, pool_ref, kout_ref, vout_ref,
         uext, kext, vext, q_sc, gate_sc, mixed_sc, s_sc) = refs
    t = pl.program_id(1)
    hr = POOL_HIST_ROWS
    kb = BAND + rb
    seqs = range(ns)

    def seq_rows(s):
        return slice(s * tt, (s + 1) * tt)

    @pl.when(t == 0)
    def _():
        for s in seqs:
            if has_hist:
                uext[s, 0:hr, :] = hu_ref[s]
                kext[s, 0:BAND, :] = hk_ref[s].astype(BF16)
                vext[s, 0:BAND, :] = hv_ref[s].astype(BF16)
            else:
                uext[s, 0:hr, :] = jnp.zeros((hr, POOL_WIDTH), F32)
                kext[s, 0:BAND, :] = jnp.zeros((BAND, ATT_WIDTH), BF16)
                vext[s, 0:BAND, :] = jnp.zeros((BAND, ATT_WIDTH), BF16)

    @pl.when(t > 0)
    def _():
        for s in seqs:
            uext[s, 0:hr, :] = uext[s, tt:tt + hr, :]
            kext[s, 0:BAND, :] = kext[s, tt:tt + BAND, :]
            vext[s, 0:BAND, :] = vext[s, tt:tt + BAND, :]

    def load_x():
        return x_ref[0] if ns == 1 else jnp.concatenate([x_ref[s] for s in seqs], axis=0)

    h = _rmsnorm(load_x(), gpre_ref[...]).astype(BF16)
    p_, a_ = POOL_WIDTH, ATT_WIDTH

    def project(part):
        cols = slice(part * a_, (part + 1) * a_)
        res = _dot(h, win_ref[:, cols])
        if part == 0:
            for s in seqs:
                uext[s, hr:hr + tt, :] = res[seq_rows(s)]
        elif part == 1:
            q_sc[...] = (res * (ATT_HEAD_DIM ** -0.5 * LOG2_E)).astype(BF16)
        elif part == 2:
            for s in seqs:
                kext[s, BAND:BAND + tt, :] = res[seq_rows(s)].astype(BF16)
                kout_ref[s] = res[seq_rows(s)]
        elif part == 3:
            for s in seqs:
                vext[s, BAND:BAND + tt, :] = res[seq_rows(s)].astype(BF16)
                vout_ref[s] = res[seq_rows(s)]
        else:
            gate_sc[:, (part - 4) * a_:(part - 3) * a_] = _silu(res)

    pos = pos0 + t * tt + lax.broadcasted_iota(jnp.int32, (tt, 1), 0)

    def pool_group(g):
        w = POOL_WINDOWS[g]
        ln = slice(g * POOL_GROUP_DIM, (g + 1) * POOL_GROUP_DIM)
        count = jnp.minimum(pos + 1, w).astype(F32)
        pooled = []
        for s in seqs:
            tok = uext[s, hr:hr + tt, ln]
            win_sum = tok
            for back in range(1, w):
                win_sum = win_sum + uext[s, hr - back:hr - back + tt, ln]
            pooled.append(win_sum / count - tok)
        pooled = pooled[0] if ns == 1 else jnp.concatenate(pooled, axis=0)
        mixed = _dot(pooled.astype(BF16), wmix_ref[g]) * scale_ref[:, ln]
        mixed_sc[:, ln] = (mixed * gate_sc[:, ln]).astype(BF16)

    assert p_ == a_ and POOL_GROUPS == 4
    project(0)
    project(4)
    for g, part in enumerate((1, 2, 3, 5)):
        project(part)
        pool_group(g)

    lane = lax.broadcasted_iota(jnp.int32, (rb, LANES), 1)
    even_head = lane < ATT_HEAD_DIM

    def attend(sequence_start):
        units = [(s, blk, p) for s in seqs for blk in range(tt // rb) for p in range(HEAD_PAIRS)]

        def band(blk):
            r0 = blk * rb
            k0 = BAND if sequence_start else r0
            c0 = BAND - r0 if sequence_start else 0
            return r0, k0, c0, kb - c0

        def stage_scores(i):
            s, blk, p = units[i]
            r0, k0, c0, kw = band(blk)
            ln = slice(p * LANES, (p + 1) * LANES)
            qp = q_sc[s * tt + r0:s * tt + r0 + rb, ln]
            zero = jnp.zeros_like(qp)
            q2 = jnp.concatenate([jnp.where(even_head, qp, zero), jnp.where(even_head, zero, qp)], axis=0)
            s_sc[i % 2, :, 0:kw] = _dot_nt(q2, kext[s, k0:k0 + kw, ln]) + bias_ref[p, :, c0:c0 + kw]

        def finish(i):
            s, blk, p = units[i]
            r0, k0, c0, kw = band(blk)
            ln = slice(p * LANES, (p + 1) * LANES)
            row_max = jnp.max(s_sc[i % 2, :, 0:kw], axis=-1, keepdims=True)
            e = jnp.exp2(s_sc[i % 2, :, 0:kw] - row_max)
            denom = jnp.sum(e, axis=-1, keepdims=True)
            o2 = _dot(e.astype(BF16), vext[s, k0:k0 + kw, ln]) / denom
            o = jnp.where(even_head, o2[0:rb], o2[rb:2 * rb])
            rows = slice(s * tt + r0, s * tt + r0 + rb)
            mo = slice(POOL_WIDTH + p * LANES, POOL_WIDTH + (p + 1) * LANES)
            mixed_sc[rows, mo] = (o * gate_sc[rows, mo]).astype(BF16)

        stage_scores(0)
        for i in range(len(units)):
            if i + 1 < len(units):
                stage_scores(i + 1)
            finish(i)

    if has_hist:
        attend(False)
    else:
        pl.when(t == 0)(functools.partial(attend, True))
        pl.when(t > 0)(functools.partial(attend, False))

    y = load_x() + _rmsnorm(_dot(mixed_sc[...], wout_ref[...]), gpost_ref[...])
    for s in seqs:
        y_ref[s] = y[seq_rows(s)]
        pool_ref[s] = uext[s, tt:tt + hr, :]


def _even_layer(x, hist, gpre, gpost, w_in, w_mix, scale, bias, w_out, *, tt, rb, ns, pos0):
    b, t_len, d = x.shape
    has_hist = hist is not None
    nt = t_len // tt
    assert t_len % tt == 0 and tt % rb == 0 and b % ns == 0
    assert has_hist or rb % LANES == 0, "band start at a sequence start must stay lane aligned in the bias"
    keep = min(BAND, t_len)
    assert keep == tt, "key/value cache rows must be exactly the last time tile"
    assert nt == 1 or tt >= BAND
    kb = BAND + rb

    def whole(shape):
        return pl.BlockSpec(shape, lambda i, j: (0,) * len(shape))

    def per_seq(shape):
        return pl.BlockSpec((ns,) + shape, lambda i, j: (i, 0, 0))

    in_specs = [pl.BlockSpec((ns, tt, d), lambda i, j: (i, j, 0))]
    args = [x]
    if has_hist:
        in_specs += [per_seq((POOL_HIST_ROWS, POOL_WIDTH)), per_seq((BAND, ATT_WIDTH)), per_seq((BAND, ATT_WIDTH))]
        args += list(hist)
    in_specs += [whole((1, d)), whole((1, d)), whole((d, EVEN_IN)),
                 whole((POOL_GROUPS, POOL_GROUP_DIM, POOL_GROUP_DIM)), whole((1, POOL_WIDTH)),
                 whole((HEAD_PAIRS, 2 * rb, kb)), whole((POOL_WIDTH + ATT_WIDTH, d))]
    args += [gpre, gpost, w_in, w_mix, scale, bias, w_out]
    out_shape = (jax.ShapeDtypeStruct((b, t_len, d), F32),
                 jax.ShapeDtypeStruct((b, POOL_HIST_ROWS, POOL_WIDTH), F32),
                 jax.ShapeDtypeStruct((b, keep, ATT_WIDTH), F32),
                 jax.ShapeDtypeStruct((b, keep, ATT_WIDTH), F32))
    out_specs = (pl.BlockSpec((ns, tt, d), lambda i, j: (i, j, 0)),
                 per_seq((POOL_HIST_ROWS, POOL_WIDTH)), per_seq((keep, ATT_WIDTH)), per_seq((keep, ATT_WIDTH)))
    scratch = [pltpu.VMEM((ns, POOL_HIST_ROWS + tt, POOL_WIDTH), F32),
               pltpu.VMEM((ns, BAND + tt, ATT_WIDTH), BF16),
               pltpu.VMEM((ns, BAND + tt, ATT_WIDTH), BF16),
               pltpu.VMEM((ns * tt, ATT_WIDTH), BF16),
               pltpu.VMEM((ns * tt, POOL_WIDTH + ATT_WIDTH), F32),
               pltpu.VMEM((ns * tt, POOL_WIDTH + ATT_WIDTH), BF16),
               pltpu.VMEM((2, 2 * rb, kb), F32)]
    return pl.pallas_call(
        functools.partial(_even_kernel, tt=tt, rb=rb, ns=ns, has_hist=has_hist, pos0=pos0),
        grid=(b // ns, nt), in_specs=in_specs, out_specs=out_specs, out_shape=out_shape,
        scratch_shapes=scratch,
        compiler_params=pltpu.CompilerParams(dimension_semantics=("parallel", "arbitrary"),
                                             vmem_limit_bytes=V7X_VMEM_LIMIT_BYTES),
        name="even_layer_hist" if has_hist else "even_layer",
    )(*args)


def _odd_kernel(*refs, tt, lc, ns, has_state):
    if has_state:
        x_ref, c0_ref, n0_ref, m0_ref = refs[0:4]
        refs = refs[4:]
    else:
        x_ref = refs[0]
        refs = refs[1:]
    (gpre_ref, gpost_ref, w_ref, wg_ref, bg_ref, gain_ref, wout_ref,
     y_ref, cout_ref, nout_ref, mout_ref, q_sc, k_sc, v_sc, oz_sc, mixed_sc, c_sc, n_sc, m_sc) = refs
    t = pl.program_id(1)
    w_, dh, nh = ML_WIDTH, ML_HEAD_DIM, ML_HEADS
    if ns == 1:
        groups = [[(0, c * lc)] for c in range(tt // lc)]
    else:
        assert tt == lc
        groups = [[(s, s * tt) for s in range(ns)]]
    n_groups = len(groups)
    g_rows = lc * len(groups[0])

    def group_rows(g):
        return slice(g * g_rows, (g + 1) * g_rows)

    @pl.when(t == 0)
    def _():
        if has_state:
            c_sc[...] = c0_ref[...]
            n_sc[...] = n0_ref[...]
            m_sc[...] = m0_ref[...]
        else:
            c_sc[...] = jnp.zeros(c_sc.shape, F32)
            n_sc[...] = jnp.zeros(n_sc.shape, F32)
            m_sc[...] = jnp.zeros(m_sc.shape, F32)

    def x_rows(g):
        if ns == 1:
            return x_ref[0, group_rows(g), :]
        return jnp.concatenate([x_ref[s] for s in range(ns)], axis=0)

    h = [_rmsnorm(x_rows(g), gpre_ref[...]).astype(BF16) for g in range(n_groups)]

    def project(g, part):
        rows = group_rows(g)
        if part == 0:
            q_sc[rows, :] = _dot_nt(h[g], w_ref[0:w_, :]).astype(BF16)
        elif part == 1:
            k_sc[rows, :] = (_dot_nt(h[g], w_ref[w_:2 * w_, :]) * (dh ** -0.5)).astype(BF16)
        elif part == 2:
            v_sc[rows, :] = _dot_nt(h[g], w_ref[2 * w_:3 * w_, :]).astype(BF16)
        else:
            oz_sc[rows, :] = (jax.nn.sigmoid(_dot_nt(h[g], w_ref[3 * w_:4 * w_, :]))
                              * _silu(_dot_nt(h[g], w_ref[4 * w_:5 * w_, :])))

    tri = jnp.where(lax.broadcasted_iota(jnp.int32, (lc, lc), 1) <= lax.broadcasted_iota(jnp.int32, (lc, lc), 0),
                    1.0, 0.0).astype(BF16)
    sel = jnp.where(lax.broadcasted_iota(jnp.int32, (8, LANES), 0)
                    == lax.broadcasted_iota(jnp.int32, (8, LANES), 1), 1.0, 0.0).astype(BF16)
    causal = (lax.broadcasted_iota(jnp.int32, (lc, lc), 1) <= lax.broadcasted_iota(jnp.int32, (lc, lc), 0))

    slabs = [(0, part) for part in range(4)]
    m_prev = [m_sc[s] for s in range(ns)]
    seg_gates = {}
    for g, segments in enumerate(groups):
        gates = _dot_nt(h[g], wg_ref[...]) + bg_ref[...]
        if slabs:
            project(*slabs.pop(0))
        for s, r0 in segments:
            lo = r0 - g * g_rows
            ig = gates[lo:lo + lc, 0:LANES]
            lf3 = _split3(jax.nn.log_sigmoid(gates[lo:lo + lc, LANES:2 * LANES]))
            b_all = _dot(tri, lf3[0]) + _dot(tri, lf3[1]) + _dot(tri, lf3[2])
            if slabs and (s, r0) == segments[0]:
                project(*slabs.pop(0))
            b3 = _split3(b_all)
            g3 = _split3(ig)
            brow = _dot_nt(sel, b3[0]) + _dot_nt(sel, b3[1]) + _dot_nt(sel, b3[2])
            igrow = _dot_nt(sel, g3[0]) + _dot_nt(sel, g3[1]) + _dot_nt(sel, g3[2])
            b_last_all = b_all[lc - 1:lc]
            g_all = b_last_all - b_all + ig
            m_new = jnp.maximum(b_last_all + m_prev[s], jnp.max(g_all, axis=0, keepdims=True))
            seg_gates[(s, r0)] = dict(b=b_all, inter=b_all + m_prev[s], brow=brow, igrow=igrow,
                                      decay=jnp.exp(b_last_all + m_prev[s] - m_new),
                                      wgt=jnp.exp(g_all - m_new))
            m_prev[s] = m_new
    for s in range(ns):
        m_sc[s] = m_prev[s]
    while slabs:
        project(*slabs.pop(0))

    def unit(seg, hd):
        s, r0 = seg
        rows = slice(r0, r0 + lc)
        ln = slice(hd * dh, (hd + 1) * dh)
        gts = seg_gates[seg]
        b_c = gts["b"][:, hd:hd + 1]
        b_r = gts["brow"][hd:hd + 1, :]
        ig_r = gts["igrow"][hd:hd + 1, :]
        inter = gts["inter"][:, hd:hd + 1]
        dmat = jnp.where(causal, b_c - b_r + ig_r, -jnp.inf)
        m_t = jnp.maximum(inter, jnp.max(dmat, axis=-1, keepdims=True))
        a = jnp.exp(inter - m_t)
        q = q_sc[rows, ln]
        k = k_sc[rows, ln]
        v = v_sc[rows, ln]
        sc = _dot_nt(q, k) * jnp.exp(dmat - m_t)
        c_old = c_sc[s, hd]
        n_old = n_sc[s, hd]
        num = a * _dot_nt(q, c_old.astype(BF16)) + _dot(sc.astype(BF16), v)
        qn = jnp.sum(q.astype(F32) * n_old, axis=-1, keepdims=True)
        den = a * qn + jnp.sum(sc, axis=-1, keepdims=True)
        hv = num / jnp.maximum(jnp.abs(den), jnp.exp(-m_t))
        mu = jnp.mean(hv, axis=-1, keepdims=True)
        dlt = hv - mu
        var = jnp.mean(dlt * dlt, axis=-1, keepdims=True)
        hn = dlt * lax.rsqrt(var + LN_EPS) * gain_ref[:, ln]
        mixed_sc[rows, ln] = (hn * oz_sc[rows, ln]).astype(BF16)
        decay = gts["decay"][:, hd:hd + 1]
        wgt = gts["wgt"][:, hd:hd + 1]
        vw = (v.astype(F32) * wgt).astype(BF16)
        c_sc[s, hd] = decay * c_old + _dot_tn(vw, k)
        n_sc[s, hd] = decay * n_old + jnp.sum(wgt * k.astype(F32), axis=0, keepdims=True)

    def out_project(g):
        res = x_rows(g) + _rmsnorm(_dot(mixed_sc[group_rows(g), :], wout_ref[...]), gpost_ref[...])
        if ns == 1:
            y_ref[0, group_rows(g), :] = res
        else:
            for s in range(ns):
                y_ref[s] = res[s * tt:(s + 1) * tt]

    n_parts = 4
    for g, segments in enumerate(groups):
        if g > 0:
            out_project(g - 1)
        for i, (seg, hd) in enumerate((seg, hd) for seg in segments for hd in range(nh)):
            if g + 1 < n_groups and i < n_parts:
                project(g + 1, i)
            unit(seg, hd)
    out_project(n_groups - 1)
    cout_ref[...] = c_sc[...]
    nout_ref[...] = n_sc[...]
    mout_ref[...] = m_sc[...]


def _odd_layer(x, state, gpre, gpost, w_main, w_g, b_g, gain, w_out, *, tt, lc, ns):
    b, t_len, d = x.shape
    nt = t_len // tt
    assert t_len % tt == 0 and tt % lc == 0 and b % ns == 0
    nh, dh = ML_HEADS, ML_HEAD_DIM
    has_state = state is not None

    def whole(shape):
        return pl.BlockSpec(shape, lambda i, j: (0,) * len(shape))

    def per_seq(shape):
        return pl.BlockSpec((ns,) + shape, lambda i, j: (i,) + (0,) * len(shape))

    st_specs = [per_seq((nh, dh, dh)), per_seq((nh, 1, dh)), per_seq((1, LANES))]
    in_specs = [pl.BlockSpec((ns, tt, d), lambda i, j: (i, j, 0))]
    args = [x]
    if has_state:
        in_specs += st_specs
        args += list(state)
    in_specs += [whole((1, d)), whole((1, d)), whole((5 * ML_WIDTH, d)),
                 whole((2 * LANES, d)), whole((1, 2 * LANES)), whole((1, ML_WIDTH)), whole((ML_WIDTH, d))]
    args += [gpre, gpost, w_main, w_g, b_g, gain, w_out]
    out_shape = (jax.ShapeDtypeStruct((b, t_len, d), F32),
                 jax.ShapeDtypeStruct((b, nh, dh, dh), F32),
                 jax.ShapeDtypeStruct((b, nh, 1, dh), F32),
                 jax.ShapeDtypeStruct((b, 1, LANES), F32))
    out_specs = (pl.BlockSpec((ns, tt, d), lambda i, j: (i, j, 0)), *st_specs)
    rows = ns * tt
    scratch = [pltpu.VMEM((rows, ML_WIDTH), BF16), pltpu.VMEM((rows, ML_WIDTH), BF16),
               pltpu.VMEM((rows, ML_WIDTH), BF16), pltpu.VMEM((rows, ML_WIDTH), F32),
               pltpu.VMEM((rows, ML_WIDTH), BF16),
               pltpu.VMEM((ns, nh, dh, dh), F32), pltpu.VMEM((ns, nh, 1, dh), F32),
               pltpu.VMEM((ns, 1, LANES), F32)]
    return pl.pallas_call(
        functools.partial(_odd_kernel, tt=tt, lc=lc, ns=ns, has_state=has_state),
        grid=(b // ns, nt), in_specs=in_specs, out_specs=out_specs, out_shape=out_shape,
        scratch_shapes=scratch,
        compiler_params=pltpu.CompilerParams(dimension_semantics=("parallel", "arbitrary"),
                                             vmem_limit_bytes=V7X_VMEM_LIMIT_BYTES),
        name="odd_layer_state" if has_state else "odd_layer",
    )(*args)


PROMPT_TILE = 512
ATT_BLOCK_CHUNKS = 2
ML_PROMPT_CHUNK = 256
SAMPLE_SEQS_PER_STEP = 4


def kernel(x_prompt, x_sample, cache_pool, cache_k, cache_v, state_C, state_n, state_m, norm_pre, norm_post,
           w_in_even, w_pool_mix, pool_scale, rel_bias, w_out_even, w_in_odd, b_gate_odd, mlstm_norm, w_out_odd):
    depth = norm_pre.shape[0]
    bp, dec_seq = x_prompt.shape[0], x_sample.shape[1]
    bs = x_sample.shape[0]
    nh = ML_HEADS
    xp, xs = x_prompt, x_sample
    outs = {name: [] for name in ("pool_p", "k_p", "v_p", "C_p", "n_p", "m_p",
                                  "pool_s", "k_s", "v_s", "C_s", "n_s", "m_s")}
    for layer in range(depth):
        gpre = norm_pre[layer][None, :]
        gpost = norm_post[layer][None, :]
        if layer % 2 == 0:
            e = layer // 2
            w_in = _cast_range_bf16(w_in_even[e], 0, EVEN_IN, axis=1, block=CAST_BLOCK)
            w_mix = w_pool_mix[e].astype(BF16)
            w_out = w_out_even[e].astype(BF16)
            scale = pool_scale[e][None, :]
            bias_p = _expand_bias(rel_bias[e], CHUNK, ATT_BLOCK_CHUNKS)
            bias_s = _expand_bias(rel_bias[e], dec_seq, 1)
            xp, pp, kp, vp = _even_layer(xp, None, gpre, gpost, w_in, w_mix, scale, bias_p, w_out,
                                         tt=PROMPT_TILE, rb=ATT_BLOCK_CHUNKS * CHUNK, ns=1, pos0=0)
            hist = (jnp.pad(cache_pool[e], ((0, 0), (POOL_HIST_ROWS - POOL_HIST, 0), (0, 0))),
                    cache_k[e].reshape(bs, BAND, ATT_WIDTH), cache_v[e].reshape(bs, BAND, ATT_WIDTH))
            xs, ps, ks, vs = _even_layer(xs, hist, gpre, gpost, w_in, w_mix, scale, bias_s, w_out,
                                         tt=dec_seq, rb=dec_seq, ns=SAMPLE_SEQS_PER_STEP, pos0=PAST_LEN)
            outs["pool_p"].append(pp[:, POOL_HIST_ROWS - POOL_HIST:])
            outs["k_p"].append(kp.reshape(bp, -1, ATT_HEADS, ATT_HEAD_DIM))
            outs["v_p"].append(vp.reshape(bp, -1, ATT_HEADS, ATT_HEAD_DIM))
            outs["pool_s"].append(ps[:, POOL_HIST_ROWS - POOL_HIST:])
            outs["k_s"].append(ks.reshape(bs, -1, ATT_HEADS, ATT_HEAD_DIM))
            outs["v_s"].append(vs.reshape(bs, -1, ATT_HEADS, ATT_HEAD_DIM))
        else:
            o = layer // 2
            wt = jnp.swapaxes(w_in_odd[o], 0, 1)
            w_main = _cast_range_bf16(wt, 0, 5 * ML_WIDTH, axis=0, block=CAST_BLOCK)
            w_gates = _cast_range_bf16(wt, 5 * ML_WIDTH, 2 * nh, axis=0, block=2 * nh)
            pad_h = ((0, LANES - nh), (0, 0))
            w_g = jnp.concatenate([jnp.pad(w_gates[:nh], pad_h), jnp.pad(w_gates[nh:], pad_h)], axis=0)
            b_g = jnp.concatenate([jnp.pad(b_gate_odd[o][:nh], (0, LANES - nh)),
                                   jnp.pad(b_gate_odd[o][nh:], (0, LANES - nh))])[None, :]
            gain = mlstm_norm[o][None, :]
            w_out = w_out_odd[o].astype(BF16)
            xp, cp, np_, mp = _odd_layer(xp, None, gpre, gpost, w_main, w_g, b_g, gain, w_out,
                                         tt=PROMPT_TILE, lc=ML_PROMPT_CHUNK, ns=1)
            state = (state_C[o], state_n[o][:, :, None, :],
                     jnp.pad(state_m[o], ((0, 0), (0, LANES - nh)))[:, None, :])
            xs, cs, ns, ms = _odd_layer(xs, state, gpre, gpost, w_main, w_g, b_g, gain, w_out,
                                        tt=dec_seq, lc=dec_seq, ns=SAMPLE_SEQS_PER_STEP)
            outs["C_p"].append(cp)
            outs["n_p"].append(np_[:, :, 0, :])
            outs["m_p"].append(mp[:, 0, :nh])
            outs["C_s"].append(cs)
            outs["n_s"].append(ns[:, :, 0, :])
            outs["m_s"].append(ms[:, 0, :nh])
    return (xp, xs,
            jnp.stack(outs["pool_p"]), jnp.stack(outs["k_p"]), jnp.stack(outs["v_p"]),
            jnp.stack(outs["C_p"]), jnp.stack(outs["n_p"]), jnp.stack(outs["m_p"]),
            jnp.stack(outs["pool_s"]), jnp.stack(outs["k_s"]), jnp.stack(outs["v_s"]),
            jnp.stack(outs["C_s"]), jnp.stack(outs["n_s"]), jnp.stack(outs["m_s"]))
```

```python
import functools

import jax
import jax.numpy as jnp
from jax import lax
from jax.experimental import pallas as pl
from jax.experimental.pallas import tpu as pltpu

D_MODEL = 1024
PAST_LEN = 4096
CHUNK = 64
POOL_WINDOWS = (2, 4, 8, 16)
POOL_GROUPS = len(POOL_WINDOWS)
POOL_WIDTH = D_MODEL // 2
POOL_GROUP_DIM = POOL_WIDTH // POOL_GROUPS
POOL_HIST = max(POOL_WINDOWS) - 1
POOL_HIST_ROWS = 16
ATT_WIDTH = D_MODEL // 2
ATT_HEADS = 8
ATT_HEAD_DIM = ATT_WIDTH // ATT_HEADS
HEAD_PAIRS = ATT_HEADS // 2
BAND = 8 * CHUNK
REL_CLIP = 128
REL_TABLE = 2 * REL_CLIP + 1
REL_TABLE_PAD = 384
ML_WIDTH = D_MODEL
ML_HEADS = 4
ML_HEAD_DIM = ML_WIDTH // ML_HEADS
EVEN_IN = 2 * POOL_WIDTH + 4 * ATT_WIDTH
RMS_EPS = 1e-6
LN_EPS = 1e-6
LOG2_E = 1.4426950408889634

LANES = 128
SCORE_SLOTS = 3
CAST_BLOCK = 512
V7X_VMEM_BYTES = 64 * 1024 * 1024
V7X_VMEM_LIMIT_BYTES = V7X_VMEM_BYTES * 7 // 8

BF16 = jnp.bfloat16
F32 = jnp.float32


def _dot(a, b):
    return jnp.dot(a, b, preferred_element_type=F32)


def _dot_nt(a, b):
    return lax.dot_general(a, b, (((1,), (1,)), ((), ())), preferred_element_type=F32)


def _dot_tn(a, b):
    return lax.dot_general(a, b, (((0,), (0,)), ((), ())), preferred_element_type=F32)


def _split3(x):
    hi = x.astype(BF16)
    r1 = x - hi.astype(F32)
    mid = r1.astype(BF16)
    lo = (r1 - mid.astype(F32)).astype(BF16)
    return hi, mid, lo


def _rmsnorm(x, g):
    return x * lax.rsqrt(jnp.mean(x * x, axis=-1, keepdims=True) + RMS_EPS) * g


def _silu(x):
    return x * jax.nn.sigmoid(x)


def _cast_kernel(src_ref, dst_ref):
    dst_ref[...] = src_ref[...].astype(dst_ref.dtype)


def _cast_range_bf16(w, start, n, axis, block):
    assert n % block == 0 and start % block == 0 and start + n <= w.shape[axis]
    first = start // block
    if axis == 0:
        in_spec = pl.BlockSpec((block, w.shape[1]), lambda j: (first + j, 0))
        out_spec = pl.BlockSpec((block, w.shape[1]), lambda j: (j, 0))
        shape = (n, w.shape[1])
    else:
        in_spec = pl.BlockSpec((w.shape[0], block), lambda j: (0, first + j))
        out_spec = pl.BlockSpec((w.shape[0], block), lambda j: (0, j))
        shape = (w.shape[0], n)
    return pl.pallas_call(
        _cast_kernel, grid=(n // block,), in_specs=[in_spec], out_specs=out_spec,
        out_shape=jax.ShapeDtypeStruct(shape, BF16), name="cast_bf16",
    )(w)


def _bias_kernel(tab_ref, out_ref, *, lc, nb):
    rb = nb * lc
    kb = BAND + rb
    n_dist = rb - 1 + kb
    gw = (n_dist + LANES - 1) // LANES * LANES
    parts = _split3(tab_ref[...])
    dist = lax.broadcasted_iota(jnp.int32, (REL_TABLE_PAD, gw), 1) - (rb - 1)
    ti = lax.broadcasted_iota(jnp.int32, (REL_TABLE_PAD, gw), 0)
    onehot = jnp.where(ti == jnp.clip(BAND - dist, -REL_CLIP, REL_CLIP) + REL_CLIP, 1.0, 0.0).astype(BF16)
    by_dist = _dot(parts[0], onehot) + _dot(parts[1], onehot) + _dot(parts[2], onehot)
    key_chunk = lax.broadcasted_iota(jnp.int32, (ATT_HEADS, kb), 1) // lc

    def row(qi, carry):
        shift = lax.rem(gw - (rb - 1 - qi), gw)
        acc = pltpu.roll(by_dist, shift, axis=1)[:, 0:kb]
        q_chunk = qi // lc
        in_band = (key_chunk >= q_chunk) & (key_chunk <= q_chunk + BAND // lc)
        out_ref[qi] = jnp.where(in_band, acc * LOG2_E, -jnp.inf)
        return carry

    lax.fori_loop(0, rb, row, 0)


def _expand_bias(table, lc, nb):
    rb = nb * lc
    kb = BAND + rb
    tab = jnp.pad(table.astype(F32), ((0, 0), (0, REL_TABLE_PAD - REL_TABLE)))
    out = pl.pallas_call(
        functools.partial(_bias_kernel, lc=lc, nb=nb),
        out_shape=jax.ShapeDtypeStruct((rb, ATT_HEADS, kb), F32),
        name="rel_bias_expand",
    )(tab)
    return out.transpose(1, 0, 2).reshape(HEAD_PAIRS, 2 * rb, kb)


def _even_kernel(*refs, tt, rb, ns, has_hist, pos0):
    if has_hist:
        (x_ref, hu_ref, hk_ref, hv_ref, gpre_ref, gpost_ref, win_ref, wmix_ref, scale_ref,
         bias_ref, wout_ref, y_ref, pool_ref, kout_ref, vout_ref,
         uext, kext, vext, q_sc, gate_sc, mixed_sc, s_sc) = refs
    else:
        (x_ref, gpre_ref, gpost_ref, win_ref, wmix_ref, scale_ref,
         bias_ref, wout_ref, y_ref, pool_ref, kout_ref, vout_ref,
         uext, kext, vext, q_sc, gate_sc, mixed_sc, s_sc) = refs
    t = pl.program_id(1)
    hr = POOL_HIST_ROWS
    kb = BAND + rb
    seqs = range(ns)

    def seq_rows(s):
        return slice(s * tt, (s + 1) * tt)

    @pl.when(t == 0)
    def _():
        for s in seqs:
            if has_hist:
                uext[s, 0:hr, :] = hu_ref[s]
                kext[s, 0:BAND, :] = hk_ref[s].astype(BF16)
                vext[s, 0:BAND, :] = hv_ref[s].astype(BF16)
            else:
                uext[s, 0:hr, :] = jnp.zeros((hr, POOL_WIDTH), F32)
                kext[s, 0:BAND, :] = jnp.zeros((BAND, ATT_WIDTH), BF16)
                vext[s, 0:BAND, :] = jnp.zeros((BAND, ATT_WIDTH), BF16)

    @pl.when(t > 0)
    def _():
        for s in seqs:
            uext[s, 0:hr, :] = uext[s, tt:tt + hr, :]
            kext[s, 0:BAND, :] = kext[s, tt:tt + BAND, :]
            vext[s, 0:BAND, :] = vext[s, tt:tt + BAND, :]

    def load_x():
        return x_ref[0] if ns == 1 else jnp.concatenate([x_ref[s] for s in seqs], axis=0)

    h = _rmsnorm(load_x(), gpre_ref[...]).astype(BF16)
    p_, a_ = POOL_WIDTH, ATT_WIDTH

    def project(part):
        cols = slice(part * a_, (part + 1) * a_)
        res = _dot(h, win_ref[:, cols])
        if part == 0:
            for s in seqs:
                uext[s, hr:hr + tt, :] = res[seq_rows(s)]
        elif part == 1:
            q_sc[...] = (res * (ATT_HEAD_DIM ** -0.5 * LOG2_E)).astype(BF16)
        elif part == 2:
            for s in seqs:
                kext[s, BAND:BAND + tt, :] = res[seq_rows(s)].astype(BF16)
                kout_ref[s] = res[seq_rows(s)]
        elif part == 3:
            for s in seqs:
                vext[s, BAND:BAND + tt, :] = res[seq_rows(s)].astype(BF16)
                vout_ref[s] = res[seq_rows(s)]
        else:
            gate_sc[:, (part - 4) * a_:(part - 3) * a_] = _silu(res)

    pos = pos0 + t * tt + lax.broadcasted_iota(jnp.int32, (tt, 1), 0)

    def pool_group(g):
        w = POOL_WINDOWS[g]
        ln = slice(g * POOL_GROUP_DIM, (g + 1) * POOL_GROUP_DIM)
        count = jnp.minimum(pos + 1, w).astype(F32)
        pooled = []
        for s in seqs:
            tok = uext[s, hr:hr + tt, ln]
            win_sum = tok
            for back in range(1, w):
                win_sum = win_sum + uext[s, hr - back:hr - back + tt, ln]
            pooled.append(win_sum / count - tok)
        pooled = pooled[0] if ns == 1 else jnp.concatenate(pooled, axis=0)
        mixed = _dot(pooled.astype(BF16), wmix_ref[g]) * scale_ref[:, ln]
        mixed_sc[:, ln] = (mixed * gate_sc[:, ln]).astype(BF16)

    assert p_ == a_ and POOL_GROUPS == 4
    project(0)
    project(4)
    for g, part in enumerate((1, 2, 3, 5)):
        project(part)
        pool_group(g)

    lane = lax.broadcasted_iota(jnp.int32, (rb, LANES), 1)
    even_head = lane < ATT_HEAD_DIM

    def attend(sequence_start):
        units = [(s, blk, p) for s in seqs for blk in range(tt // rb) for p in range(HEAD_PAIRS)]

        def band(blk):
            r0 = blk * rb
            k0 = BAND if sequence_start else r0
            c0 = BAND - r0 if sequence_start else 0
            return r0, k0, c0, kb - c0

        def stage_scores(i):
            s, blk, p = units[i]
            r0, k0, c0, kw = band(blk)
            ln = slice(p * LANES, (p + 1) * LANES)
            qp = q_sc[s * tt + r0:s * tt + r0 + rb, ln]
            zero = jnp.zeros_like(qp)
            q2 = jnp.concatenate([jnp.where(even_head, qp, zero), jnp.where(even_head, zero, qp)], axis=0)
            s_sc[i % SCORE_SLOTS, :, 0:kw] = _dot_nt(q2, kext[s, k0:k0 + kw, ln]) + bias_ref[p, :, c0:c0 + kw]

        def finish(i):
            s, blk, p = units[i]
            r0, k0, c0, kw = band(blk)
            ln = slice(p * LANES, (p + 1) * LANES)
            row_max = jnp.max(s_sc[i % SCORE_SLOTS, :, 0:kw], axis=-1, keepdims=True)
            e = jnp.exp2(s_sc[i % SCORE_SLOTS, :, 0:kw] - row_max)
            denom = jnp.sum(e, axis=-1, keepdims=True)
            o2 = _dot(e.astype(BF16), vext[s, k0:k0 + kw, ln]) / denom
            o = jnp.where(even_head, o2[0:rb], o2[rb:2 * rb])
            rows = slice(s * tt + r0, s * tt + r0 + rb)
            mo = slice(POOL_WIDTH + p * LANES, POOL_WIDTH + (p + 1) * LANES)
            mixed_sc[rows, mo] = (o * gate_sc[rows, mo]).astype(BF16)

        ahead = SCORE_SLOTS - 1
        for i in range(min(ahead, len(units))):
            stage_scores(i)
        for i in range(len(units)):
            if i + ahead < len(units):
                stage_scores(i + ahead)
            finish(i)

    if has_hist:
        attend(False)
    else:
        pl.when(t == 0)(functools.partial(attend, True))
        pl.when(t > 0)(functools.partial(attend, False))

    y = load_x() + _rmsnorm(_dot(mixed_sc[...], wout_ref[...]), gpost_ref[...])
    for s in seqs:
        y_ref[s] = y[seq_rows(s)]
        pool_ref[s] = uext[s, tt:tt + hr, :]


def _even_layer(x, hist, gpre, gpost, w_in, w_mix, scale, bias, w_out, *, tt, rb, ns, pos0):
    b, t_len, d = x.shape
    has_hist = hist is not None
    nt = t_len // tt
    assert t_len % tt == 0 and tt % rb == 0 and b % ns == 0
    assert has_hist or rb % LANES == 0, "band start at a sequence start must stay lane aligned in the bias"
    keep = min(BAND, t_len)
    assert keep == tt, "key/value cache rows must be exactly the last time tile"
    assert nt == 1 or tt >= BAND
    kb = BAND + rb

    def whole(shape):
        return pl.BlockSpec(shape, lambda i, j: (0,) * len(shape))

    def per_seq(shape):
        return pl.BlockSpec((ns,) + shape, lambda i, j: (i, 0, 0))

    in_specs = [pl.BlockSpec((ns, tt, d), lambda i, j: (i, j, 0))]
    args = [x]
    if has_hist:
        in_specs += [per_seq((POOL_HIST_ROWS, POOL_WIDTH)), per_seq((BAND, ATT_WIDTH)), per_seq((BAND, ATT_WIDTH))]
        args += list(hist)
    in_specs += [whole((1, d)), whole((1, d)), whole((d, EVEN_IN)),
                 whole((POOL_GROUPS, POOL_GROUP_DIM, POOL_GROUP_DIM)), whole((1, POOL_WIDTH)),
                 whole((HEAD_PAIRS, 2 * rb, kb)), whole((POOL_WIDTH + ATT_WIDTH, d))]
    args += [gpre, gpost, w_in, w_mix, scale, bias, w_out]
    out_shape = (jax.ShapeDtypeStruct((b, t_len, d), F32),
                 jax.ShapeDtypeStruct((b, POOL_HIST_ROWS, POOL_WIDTH), F32),
                 jax.ShapeDtypeStruct((b, keep, ATT_WIDTH), F32),
                 jax.ShapeDtypeStruct((b, keep, ATT_WIDTH), F32))
    out_specs = (pl.BlockSpec((ns, tt, d), lambda i, j: (i, j, 0)),
                 per_seq((POOL_HIST_ROWS, POOL_WIDTH)), per_seq((keep, ATT_WIDTH)), per_seq((keep, ATT_WIDTH)))
    scratch = [pltpu.VMEM((ns, POOL_HIST_ROWS + tt, POOL_WIDTH), F32),
               pltpu.VMEM((ns, BAND + tt, ATT_WIDTH), BF16),
               pltpu.VMEM((ns, BAND + tt, ATT_WIDTH), BF16),
               pltpu.VMEM((ns * tt, ATT_WIDTH), BF16),
               pltpu.VMEM((ns * tt, POOL_WIDTH + ATT_WIDTH), F32),
               pltpu.VMEM((ns * tt, POOL_WIDTH + ATT_WIDTH), BF16),
               pltpu.VMEM((SCORE_SLOTS, 2 * rb, kb), F32)]
    return pl.pallas_call(
        functools.partial(_even_kernel, tt=tt, rb=rb, ns=ns, has_hist=has_hist, pos0=pos0),
        grid=(b // ns, nt), in_specs=in_specs, out_specs=out_specs, out_shape=out_shape,
        scratch_shapes=scratch,
        compiler_params=pltpu.CompilerParams(dimension_semantics=("parallel", "arbitrary"),
                                             vmem_limit_bytes=V7X_VMEM_LIMIT_BYTES),
        name="even_layer_hist" if has_hist else "even_layer",
    )(*args)


def _odd_kernel(*refs, tt, lc, ns, has_state):
    if has_state:
        x_ref, c0_ref, n0_ref, m0_ref = refs[0:4]
        refs = refs[4:]
    else:
        x_ref = refs[0]
        refs = refs[1:]
    (gpre_ref, gpost_ref, w_ref, wg_ref, bg_ref, gain_ref, wout_ref,
     y_ref, cout_ref, nout_ref, mout_ref, q_sc, k_sc, v_sc, oz_sc, mixed_sc, c_sc, n_sc, m_sc) = refs
    t = pl.program_id(1)
    w_, dh, nh = ML_WIDTH, ML_HEAD_DIM, ML_HEADS
    if ns == 1:
        groups = [[(0, c * lc)] for c in range(tt // lc)]
    else:
        assert tt == lc
        groups = [[(s, s * tt) for s in range(ns)]]
    n_groups = len(groups)
    g_rows = lc * len(groups[0])

    def group_rows(g):
        return slice(g * g_rows, (g + 1) * g_rows)

    @pl.when(t == 0)
    def _():
        if has_state:
            c_sc[...] = c0_ref[...]
            n_sc[...] = n0_ref[...]
            m_sc[...] = m0_ref[...]
        else:
            c_sc[...] = jnp.zeros(c_sc.shape, F32)
            n_sc[...] = jnp.zeros(n_sc.shape, F32)
            m_sc[...] = jnp.zeros(m_sc.shape, F32)

    def x_rows(g):
        if ns == 1:
            return x_ref[0, group_rows(g), :]
        return jnp.concatenate([x_ref[s] for s in range(ns)], axis=0)

    h = [_rmsnorm(x_rows(g), gpre_ref[...]).astype(BF16) for g in range(n_groups)]

    def project(g, part):
        rows = group_rows(g)
        if part == 0:
            q_sc[rows, :] = _dot_nt(h[g], w_ref[0:w_, :]).astype(BF16)
        elif part == 1:
            k_sc[rows, :] = (_dot_nt(h[g], w_ref[w_:2 * w_, :]) * (dh ** -0.5)).astype(BF16)
        elif part == 2:
            v_sc[rows, :] = _dot_nt(h[g], w_ref[2 * w_:3 * w_, :]).astype(BF16)
        else:
            oz_sc[rows, :] = (jax.nn.sigmoid(_dot_nt(h[g], w_ref[3 * w_:4 * w_, :]))
                              * _silu(_dot_nt(h[g], w_ref[4 * w_:5 * w_, :])))

    tri = jnp.where(lax.broadcasted_iota(jnp.int32, (lc, lc), 1) <= lax.broadcasted_iota(jnp.int32, (lc, lc), 0),
                    1.0, 0.0).astype(BF16)
    sel = jnp.where(lax.broadcasted_iota(jnp.int32, (8, LANES), 0)
                    == lax.broadcasted_iota(jnp.int32, (8, LANES), 1), 1.0, 0.0).astype(BF16)
    causal = (lax.broadcasted_iota(jnp.int32, (lc, lc), 1) <= lax.broadcasted_iota(jnp.int32, (lc, lc), 0))

    slabs = [(0, part) for part in range(4)]
    m_prev = [m_sc[s] for s in range(ns)]
    seg_gates = {}
    for g, segments in enumerate(groups):
        gates = _dot_nt(h[g], wg_ref[...]) + bg_ref[...]
        if slabs:
            project(*slabs.pop(0))
        for s, r0 in segments:
            lo = r0 - g * g_rows
            ig = gates[lo:lo + lc, 0:LANES]
            lf3 = _split3(jax.nn.log_sigmoid(gates[lo:lo + lc, LANES:2 * LANES]))
            b_all = _dot(tri, lf3[0]) + _dot(tri, lf3[1]) + _dot(tri, lf3[2])
            if slabs and (s, r0) == segments[0]:
                project(*slabs.pop(0))
            b3 = _split3(b_all)
            g3 = _split3(ig)
            brow = _dot_nt(sel, b3[0]) + _dot_nt(sel, b3[1]) + _dot_nt(sel, b3[2])
            igrow = _dot_nt(sel, g3[0]) + _dot_nt(sel, g3[1]) + _dot_nt(sel, g3[2])
            b_last_all = b_all[lc - 1:lc]
            g_all = b_last_all - b_all + ig
            m_new = jnp.maximum(b_last_all + m_prev[s], jnp.max(g_all, axis=0, keepdims=True))
            seg_gates[(s, r0)] = dict(b=b_all, inter=b_all + m_prev[s], brow=brow, igrow=igrow,
                                      decay=jnp.exp(b_last_all + m_prev[s] - m_new),
                                      wgt=jnp.exp(g_all - m_new))
            m_prev[s] = m_new
    for s in range(ns):
        m_sc[s] = m_prev[s]
    while slabs:
        project(*slabs.pop(0))

    def unit(seg, hd):
        s, r0 = seg
        rows = slice(r0, r0 + lc)
        ln = slice(hd * dh, (hd + 1) * dh)
        gts = seg_gates[seg]
        b_c = gts["b"][:, hd:hd + 1]
        b_r = gts["brow"][hd:hd + 1, :]
        ig_r = gts["igrow"][hd:hd + 1, :]
        inter = gts["inter"][:, hd:hd + 1]
        dmat = jnp.where(causal, b_c - b_r + ig_r, -jnp.inf)
        m_t = jnp.maximum(inter, jnp.max(dmat, axis=-1, keepdims=True))
        a = jnp.exp(inter - m_t)
        q = q_sc[rows, ln]
        k = k_sc[rows, ln]
        v = v_sc[rows, ln]
        sc = _dot_nt(q, k) * jnp.exp(dmat - m_t)
        c_old = c_sc[s, hd]
        n_old = n_sc[s, hd]
        num = a * _dot_nt(q, c_old.astype(BF16)) + _dot(sc.astype(BF16), v)
        qn = jnp.sum(q.astype(F32) * n_old, axis=-1, keepdims=True)
        den = a * qn + jnp.sum(sc, axis=-1, keepdims=True)
        hv = num / jnp.maximum(jnp.abs(den), jnp.exp(-m_t))
        mu = jnp.mean(hv, axis=-1, keepdims=True)
        dlt = hv - mu
        var = jnp.mean(dlt * dlt, axis=-1, keepdims=True)
        hn = dlt * lax.rsqrt(var + LN_EPS) * gain_ref[:, ln]
        mixed_sc[rows, ln] = (hn * oz_sc[rows, ln]).astype(BF16)
        decay = gts["decay"][:, hd:hd + 1]
        wgt = gts["wgt"][:, hd:hd + 1]
        vw = (v.astype(F32) * wgt).astype(BF16)
        c_sc[s, hd] = decay * c_old + _dot_tn(vw, k)
        n_sc[s, hd] = decay * n_old + jnp.sum(wgt * k.astype(F32), axis=0, keepdims=True)

    def out_project(g):
        res = x_rows(g) + _rmsnorm(_dot(mixed_sc[group_rows(g), :], wout_ref[...]), gpost_ref[...])
        if ns == 1:
            y_ref[0, group_rows(g), :] = res
        else:
            for s in range(ns):
                y_ref[s] = res[s * tt:(s + 1) * tt]

    n_parts = 4
    for g, segments in enumerate(groups):
        if g > 0:
            out_project(g - 1)
        for i, (seg, hd) in enumerate((seg, hd) for seg in segments for hd in range(nh)):
            if g + 1 < n_groups and i < n_parts:
                project(g + 1, i)
            unit(seg, hd)
    out_project(n_groups - 1)
    cout_ref[...] = c_sc[...]
    nout_ref[...] = n_sc[...]
    mout_ref[...] = m_sc[...]


def _odd_layer(x, state, gpre, gpost, w_main, w_g, b_g, gain, w_out, *, tt, lc, ns):
    b, t_len, d = x.shape
    nt = t_len // tt
    assert t_len % tt == 0 and tt % lc == 0 and b % ns == 0
    nh, dh = ML_HEADS, ML_HEAD_DIM
    has_state = state is not None

    def whole(shape):
        return pl.BlockSpec(shape, lambda i, j: (0,) * len(shape))

    def per_seq(shape):
        return pl.BlockSpec((ns,) + shape, lambda i, j: (i,) + (0,) * len(shape))

    st_specs = [per_seq((nh, dh, dh)), per_seq((nh, 1, dh)), per_seq((1, LANES))]
    in_specs = [pl.BlockSpec((ns, tt, d), lambda i, j: (i, j, 0))]
    args = [x]
    if has_state:
        in_specs += st_specs
        args += list(state)
    in_specs += [whole((1, d)), whole((1, d)), whole((5 * ML_WIDTH, d)),
                 whole((2 * LANES, d)), whole((1, 2 * LANES)), whole((1, ML_WIDTH)), whole((ML_WIDTH, d))]
    args += [gpre, gpost, w_main, w_g, b_g, gain, w_out]
    out_shape = (jax.ShapeDtypeStruct((b, t_len, d), F32),
                 jax.ShapeDtypeStruct((b, nh, dh, dh), F32),
                 jax.ShapeDtypeStruct((b, nh, 1, dh), F32),
                 jax.ShapeDtypeStruct((b, 1, LANES), F32))
    out_specs = (pl.BlockSpec((ns, tt, d), lambda i, j: (i, j, 0)), *st_specs)
    rows = ns * tt
    scratch = [pltpu.VMEM((rows, ML_WIDTH), BF16), pltpu.VMEM((rows, ML_WIDTH), BF16),
               pltpu.VMEM((rows, ML_WIDTH), BF16), pltpu.VMEM((rows, ML_WIDTH), F32),
               pltpu.VMEM((rows, ML_WIDTH), BF16),
               pltpu.VMEM((ns, nh, dh, dh), F32), pltpu.VMEM((ns, nh, 1, dh), F32),
               pltpu.VMEM((ns, 1, LANES), F32)]
    return pl.pallas_call(
        functools.partial(_odd_kernel, tt=tt, lc=lc, ns=ns, has_state=has_state),
        grid=(b // ns, nt), in_specs=in_specs, out_specs=out_specs, out_shape=out_shape,
        scratch_shapes=scratch,
        compiler_params=pltpu.CompilerParams(dimension_semantics=("parallel", "arbitrary"),
                                             vmem_limit_bytes=V7X_VMEM_LIMIT_BYTES),
        name="odd_layer_state" if has_state else "odd_layer",
    )(*args)


PROMPT_TILE = 512
ATT_BLOCK_CHUNKS = 2
ML_PROMPT_CHUNK = 256
SAMPLE_SEQS_PER_STEP = 4


def kernel(x_prompt, x_sample, cache_pool, cache_k, cache_v, state_C, state_n, state_m, norm_pre, norm_post,
           w_in_even, w_pool_mix, pool_scale, rel_bias, w_out_even, w_in_odd, b_gate_odd, mlstm_norm, w_out_odd):
    depth = norm_pre.shape[0]
    bp, dec_seq = x_prompt.shape[0], x_sample.shape[1]
    bs = x_sample.shape[0]
    nh = ML_HEADS
    xp, xs = x_prompt, x_sample
    outs = {name: [] for name in ("pool_p", "k_p", "v_p", "C_p", "n_p", "m_p",
                                  "pool_s", "k_s", "v_s", "C_s", "n_s", "m_s")}
    for layer in range(depth):
        gpre = norm_pre[layer][None, :]
        gpost = norm_post[layer][None, :]
        if layer % 2 == 0:
            e = layer // 2
            w_in = _cast_range_bf16(w_in_even[e], 0, EVEN_IN, axis=1, block=CAST_BLOCK)
            w_mix = w_pool_mix[e].astype(BF16)
            w_out = w_out_even[e].astype(BF16)
            scale = pool_scale[e][None, :]
            bias_p = _expand_bias(rel_bias[e], CHUNK, ATT_BLOCK_CHUNKS)
            bias_s = _expand_bias(rel_bias[e], dec_seq, 1)
            xp, pp, kp, vp = _even_layer(xp, None, gpre, gpost, w_in, w_mix, scale, bias_p, w_out,
                                         tt=PROMPT_TILE, rb=ATT_BLOCK_CHUNKS * CHUNK, ns=1, pos0=0)
            hist = (jnp.pad(cache_pool[e], ((0, 0), (POOL_HIST_ROWS - POOL_HIST, 0), (0, 0))),
                    cache_k[e].reshape(bs, BAND, ATT_WIDTH), cache_v[e].reshape(bs, BAND, ATT_WIDTH))
            xs, ps, ks, vs = _even_layer(xs, hist, gpre, gpost, w_in, w_mix, scale, bias_s, w_out,
                                         tt=dec_seq, rb=dec_seq, ns=SAMPLE_SEQS_PER_STEP, pos0=PAST_LEN)
            outs["pool_p"].append(pp[:, POOL_HIST_ROWS - POOL_HIST:])
            outs["k_p"].append(kp.reshape(bp, -1, ATT_HEADS, ATT_HEAD_DIM))
            outs["v_p"].append(vp.reshape(bp, -1, ATT_HEADS, ATT_HEAD_DIM))
            outs["pool_s"].append(ps[:, POOL_HIST_ROWS - POOL_HIST:])
            outs["k_s"].append(ks.reshape(bs, -1, ATT_HEADS, ATT_HEAD_DIM))
            outs["v_s"].append(vs.reshape(bs, -1, ATT_HEADS, ATT_HEAD_DIM))
        else:
            o = layer // 2
            wt = jnp.swapaxes(w_in_odd[o], 0, 1)
            w_main = _cast_range_bf16(wt, 0, 5 * ML_WIDTH, axis=0, block=CAST_BLOCK)
            w_gates = _cast_range_bf16(wt, 5 * ML_WIDTH, 2 * nh, axis=0, block=2 * nh)
            pad_h = ((0, LANES - nh), (0, 0))
            w_g = jnp.concatenate([jnp.pad(w_gates[:nh], pad_h), jnp.pad(w_gates[nh:], pad_h)], axis=0)
            b_g = jnp.concatenate([jnp.pad(b_gate_odd[o][:nh], (0, LANES - nh)),
                                   jnp.pad(b_gate_odd[o][nh:], (0, LANES - nh))])[None, :]
            gain = mlstm_norm[o][None, :]
            w_out = w_out_odd[o].astype(BF16)
            xp, cp, np_, mp = _odd_layer(xp, None, gpre, gpost, w_main, w_g, b_g, gain, w_out,
                                         tt=PROMPT_TILE, lc=ML_PROMPT_CHUNK, ns=1)
            state = (state_C[o], state_n[o][:, :, None, :],
                     jnp.pad(state_m[o], ((0, 0), (0, LANES - nh)))[:, None, :])
            xs, cs, ns, ms = _odd_layer(xs, state, gpre, gpost, w_main, w_g, b_g, gain, w_out,
                                        tt=dec_seq, lc=dec_seq, ns=SAMPLE_SEQS_PER_STEP)
            outs["C_p"].append(cp)
            outs["n_p"].append(np_[:, :, 0, :])
            outs["m_p"].append(mp[:, 0, :nh])
            outs["C_s"].append(cs)
            outs["n_s"].append(ns[:, :, 0, :])
            outs["m_s"].append(ms[:, 0, :nh])
    return (xp, xs,
            jnp.stack(outs["pool_p"]), jnp.stack(outs["k_p"]), jnp.stack(outs["v_p"]),
            jnp.stack(outs["C_p"]), jnp.stack(outs["n_p"]), jnp.stack(outs["m_p"]),
            jnp.stack(outs["pool_s"]), jnp.stack(outs["k_s"]), jnp.stack(outs["v_s"]),
            jnp.stack(outs["C_s"]), jnp.stack(outs["n_s"]), jnp.stack(outs["m_s"]))
```

```python
import functools

import jax
import jax.numpy as jnp
from jax import lax
from jax.experimental import pallas as pl
from jax.experimental.pallas import tpu as pltpu

D_MODEL = 1024
PAST_LEN = 4096
CHUNK = 64
POOL_WINDOWS = (2, 4, 8, 16)
POOL_GROUPS = len(POOL_WINDOWS)
POOL_WIDTH = D_MODEL // 2
POOL_GROUP_DIM = POOL_WIDTH // POOL_GROUPS
POOL_HIST = max(POOL_WINDOWS) - 1
POOL_HIST_ROWS = 16
ATT_WIDTH = D_MODEL // 2
ATT_HEADS = 8
ATT_HEAD_DIM = ATT_WIDTH // ATT_HEADS
HEAD_PAIRS = ATT_HEADS // 2
BAND = 8 * CHUNK
REL_CLIP = 128
REL_TABLE = 2 * REL_CLIP + 1
REL_TABLE_PAD = 384
ML_WIDTH = D_MODEL
ML_HEADS = 4
ML_HEAD_DIM = ML_WIDTH // ML_HEADS
EVEN_IN = 2 * POOL_WIDTH + 4 * ATT_WIDTH
RMS_EPS = 1e-6
LN_EPS = 1e-6
LOG2_E = 1.4426950408889634

LANES = 128
SCORE_SLOTS = 3
CAST_BLOCK = 512
V7X_VMEM_BYTES = 64 * 1024 * 1024
V7X_VMEM_LIMIT_BYTES = V7X_VMEM_BYTES * 7 // 8

BF16 = jnp.bfloat16
F32 = jnp.float32


def _dot(a, b):
    return jnp.dot(a, b, preferred_element_type=F32)


def _dot_nt(a, b):
    return lax.dot_general(a, b, (((1,), (1,)), ((), ())), preferred_element_type=F32)


def _dot_tn(a, b):
    return lax.dot_general(a, b, (((0,), (0,)), ((), ())), preferred_element_type=F32)


def _split3(x):
    hi = x.astype(BF16)
    r1 = x - hi.astype(F32)
    mid = r1.astype(BF16)
    lo = (r1 - mid.astype(F32)).astype(BF16)
    return hi, mid, lo


def _rmsnorm(x, g):
    return x * lax.rsqrt(jnp.mean(x * x, axis=-1, keepdims=True) + RMS_EPS) * g


def _silu(x):
    return x * jax.nn.sigmoid(x)


def _cast_kernel(src_ref, dst_ref):
    dst_ref[...] = src_ref[...].astype(dst_ref.dtype)


def _cast_range_bf16(w, start, n, axis, block):
    assert n % block == 0 and start % block == 0 and start + n <= w.shape[axis]
    first = start // block
    if axis == 0:
        in_spec = pl.BlockSpec((block, w.shape[1]), lambda j: (first + j, 0))
        out_spec = pl.BlockSpec((block, w.shape[1]), lambda j: (j, 0))
        shape = (n, w.shape[1])
    else:
        in_spec = pl.BlockSpec((w.shape[0], block), lambda j: (0, first + j))
        out_spec = pl.BlockSpec((w.shape[0], block), lambda j: (0, j))
        shape = (w.shape[0], n)
    return pl.pallas_call(
        _cast_kernel, grid=(n // block,), in_specs=[in_spec], out_specs=out_spec,
        out_shape=jax.ShapeDtypeStruct(shape, BF16), name="cast_bf16",
    )(w)


def _bias_kernel(tab_ref, out_ref, *, lc, nb):
    rb = nb * lc
    kb = BAND + rb
    n_dist = rb - 1 + kb
    gw = (n_dist + LANES - 1) // LANES * LANES
    parts = _split3(tab_ref[...])
    dist = lax.broadcasted_iota(jnp.int32, (REL_TABLE_PAD, gw), 1) - (rb - 1)
    ti = lax.broadcasted_iota(jnp.int32, (REL_TABLE_PAD, gw), 0)
    onehot = jnp.where(ti == jnp.clip(BAND - dist, -REL_CLIP, REL_CLIP) + REL_CLIP, 1.0, 0.0).astype(BF16)
    by_dist = _dot(parts[0], onehot) + _dot(parts[1], onehot) + _dot(parts[2], onehot)
    key_chunk = lax.broadcasted_iota(jnp.int32, (ATT_HEADS, kb), 1) // lc

    def row(qi, carry):
        shift = lax.rem(gw - (rb - 1 - qi), gw)
        acc = pltpu.roll(by_dist, shift, axis=1)[:, 0:kb]
        q_chunk = qi // lc
        in_band = (key_chunk >= q_chunk) & (key_chunk <= q_chunk + BAND // lc)
        out_ref[qi] = jnp.where(in_band, acc * LOG2_E, -jnp.inf)
        return carry

    lax.fori_loop(0, rb, row, 0)


def _expand_bias(table, lc, nb):
    rb = nb * lc
    kb = BAND + rb
    tab = jnp.pad(table.astype(F32), ((0, 0), (0, REL_TABLE_PAD - REL_TABLE)))
    out = pl.pallas_call(
        functools.partial(_bias_kernel, lc=lc, nb=nb),
        out_shape=jax.ShapeDtypeStruct((rb, ATT_HEADS, kb), F32),
        name="rel_bias_expand",
    )(tab)
    return out.transpose(1, 0, 2).reshape(HEAD_PAIRS, 2 * rb, kb)


def _even_kernel(*refs, tt, rb, ns, has_hist, pos0):
    if has_hist:
        (x_ref, hu_ref, hk_ref, hv_ref, gpre_ref, gpost_ref, win_ref, wmix_ref, scale_ref,
         bias_ref, wout_ref, y_ref, pool_ref, kout_ref, vout_ref,
         uext, kext, vext, q_sc, gate_sc, mixed_sc, s_sc) = refs
    else:
        (x_ref, gpre_ref, gpost_ref, win_ref, wmix_ref, scale_ref,
         bias_ref, wout_ref, y_ref, pool_ref, kout_ref, vout_ref,
         uext, kext, vext, q_sc, gate_sc, mixed_sc, s_sc) = refs
    t = pl.program_id(1)
    hr = POOL_HIST_ROWS
    kb = BAND + rb
    seqs = range(ns)

    def seq_rows(s):
        return slice(s * tt, (s + 1) * tt)

    @pl.when(t == 0)
    def _():
        for s in seqs:
            if has_hist:
                uext[s, 0:hr, :] = hu_ref[s]
                kext[s, 0:BAND, :] = hk_ref[s]
                vext[s, 0:BAND, :] = hv_ref[s]
            else:
                uext[s, 0:hr, :] = jnp.zeros((hr, POOL_WIDTH), F32)
                kext[s, 0:BAND, :] = jnp.zeros((BAND, ATT_WIDTH), BF16)
                vext[s, 0:BAND, :] = jnp.zeros((BAND, ATT_WIDTH), BF16)

    @pl.when(t > 0)
    def _():
        for s in seqs:
            uext[s, 0:hr, :] = uext[s, tt:tt + hr, :]
            kext[s, 0:BAND, :] = kext[s, tt:tt + BAND, :]
            vext[s, 0:BAND, :] = vext[s, tt:tt + BAND, :]

    def load_x():
        return x_ref[0] if ns == 1 else jnp.concatenate([x_ref[s] for s in seqs], axis=0)

    h = _rmsnorm(load_x(), gpre_ref[...]).astype(BF16)
    p_, a_ = POOL_WIDTH, ATT_WIDTH

    def project(part):
        cols = slice(part * a_, (part + 1) * a_)
        res = _dot(h, win_ref[:, cols])
        if part == 0:
            for s in seqs:
                uext[s, hr:hr + tt, :] = res[seq_rows(s)]
        elif part == 1:
            q_sc[...] = (res * (ATT_HEAD_DIM ** -0.5 * LOG2_E)).astype(BF16)
        elif part == 2:
            for s in seqs:
                kext[s, BAND:BAND + tt, :] = res[seq_rows(s)].astype(BF16)
                kout_ref[s] = res[seq_rows(s)]
        elif part == 3:
            for s in seqs:
                vext[s, BAND:BAND + tt, :] = res[seq_rows(s)].astype(BF16)
                vout_ref[s] = res[seq_rows(s)]
        else:
            gate_sc[:, (part - 4) * a_:(part - 3) * a_] = _silu(res)

    pos = pos0 + t * tt + lax.broadcasted_iota(jnp.int32, (tt, 1), 0)

    def pool_group(g):
        w = POOL_WINDOWS[g]
        ln = slice(g * POOL_GROUP_DIM, (g + 1) * POOL_GROUP_DIM)
        count = jnp.minimum(pos + 1, w).astype(F32)
        pooled = []
        for s in seqs:
            tok = uext[s, hr:hr + tt, ln]
            win_sum = tok
            for back in range(1, w):
                win_sum = win_sum + uext[s, hr - back:hr - back + tt, ln]
            pooled.append(win_sum / count - tok)
        pooled = pooled[0] if ns == 1 else jnp.concatenate(pooled, axis=0)
        mixed = _dot(pooled.astype(BF16), wmix_ref[g]) * scale_ref[:, ln]
        mixed_sc[:, ln] = (mixed * gate_sc[:, ln]).astype(BF16)

    assert p_ == a_ and POOL_GROUPS == 4
    project(0)
    project(4)
    for g, part in enumerate((1, 2, 3, 5)):
        project(part)
        pool_group(g)

    lane = lax.broadcasted_iota(jnp.int32, (rb, LANES), 1)
    even_head = lane < ATT_HEAD_DIM

    def attend(sequence_start):
        units = [(s, blk, p) for s in seqs for blk in range(tt // rb) for p in range(HEAD_PAIRS)]

        def band(blk):
            r0 = blk * rb
            k0 = BAND if sequence_start else r0
            c0 = BAND - r0 if sequence_start else 0
            return r0, k0, c0, kb - c0

        def stage_scores(i):
            s, blk, p = units[i]
            r0, k0, c0, kw = band(blk)
            ln = slice(p * LANES, (p + 1) * LANES)
            qp = q_sc[s * tt + r0:s * tt + r0 + rb, ln]
            zero = jnp.zeros_like(qp)
            q2 = jnp.concatenate([jnp.where(even_head, qp, zero), jnp.where(even_head, zero, qp)], axis=0)
            s_sc[i % SCORE_SLOTS, :, 0:kw] = _dot_nt(q2, kext[s, k0:k0 + kw, ln]) + bias_ref[p, :, c0:c0 + kw]

        def finish(i):
            s, blk, p = units[i]
            r0, k0, c0, kw = band(blk)
            ln = slice(p * LANES, (p + 1) * LANES)
            row_max = jnp.max(s_sc[i % SCORE_SLOTS, :, 0:kw], axis=-1, keepdims=True)
            e = jnp.exp2(s_sc[i % SCORE_SLOTS, :, 0:kw] - row_max)
            denom = jnp.sum(e, axis=-1, keepdims=True)
            o2 = _dot(e.astype(BF16), vext[s, k0:k0 + kw, ln]) / denom
            o = jnp.where(even_head, o2[0:rb], o2[rb:2 * rb])
            rows = slice(s * tt + r0, s * tt + r0 + rb)
            mo = slice(POOL_WIDTH + p * LANES, POOL_WIDTH + (p + 1) * LANES)
            mixed_sc[rows, mo] = (o * gate_sc[rows, mo]).astype(BF16)

        ahead = SCORE_SLOTS - 1
        for i in range(min(ahead, len(units))):
            stage_scores(i)
        for i in range(len(units)):
            if i + ahead < len(units):
                stage_scores(i + ahead)
            finish(i)

    if has_hist:
        attend(False)
    else:
        pl.when(t == 0)(functools.partial(attend, True))
        pl.when(t > 0)(functools.partial(attend, False))

    y = load_x() + _rmsnorm(_dot(mixed_sc[...], wout_ref[...]), gpost_ref[...])
    for s in seqs:
        y_ref[s] = y[seq_rows(s)]
        pool_ref[s] = uext[s, tt:tt + hr, :]


def _even_layer(x, hist, gpre, gpost, w_in, w_mix, scale, bias, w_out, *, tt, rb, ns, pos0):
    b, t_len, d = x.shape
    has_hist = hist is not None
    nt = t_len // tt
    assert t_len % tt == 0 and tt % rb == 0 and b % ns == 0
    assert has_hist or rb % LANES == 0, "band start at a sequence start must stay lane aligned in the bias"
    keep = min(BAND, t_len)
    assert keep == tt, "key/value cache rows must be exactly the last time tile"
    assert nt == 1 or tt >= BAND
    kb = BAND + rb

    def whole(shape):
        return pl.BlockSpec(shape, lambda i, j: (0,) * len(shape))

    def per_seq(shape):
        return pl.BlockSpec((ns,) + shape, lambda i, j: (i, 0, 0))

    in_specs = [pl.BlockSpec((ns, tt, d), lambda i, j: (i, j, 0))]
    args = [x]
    if has_hist:
        in_specs += [per_seq((POOL_HIST_ROWS, POOL_WIDTH)), per_seq((BAND, ATT_WIDTH)), per_seq((BAND, ATT_WIDTH))]
        args += list(hist)
    in_specs += [whole((1, d)), whole((1, d)), whole((d, EVEN_IN)),
                 whole((POOL_GROUPS, POOL_GROUP_DIM, POOL_GROUP_DIM)), whole((1, POOL_WIDTH)),
                 whole((HEAD_PAIRS, 2 * rb, kb)), whole((POOL_WIDTH + ATT_WIDTH, d))]
    args += [gpre, gpost, w_in, w_mix, scale, bias, w_out]
    out_shape = (jax.ShapeDtypeStruct((b, t_len, d), F32),
                 jax.ShapeDtypeStruct((b, POOL_HIST_ROWS, POOL_WIDTH), F32),
                 jax.ShapeDtypeStruct((b, keep, ATT_WIDTH), F32),
                 jax.ShapeDtypeStruct((b, keep, ATT_WIDTH), F32))
    out_specs = (pl.BlockSpec((ns, tt, d), lambda i, j: (i, j, 0)),
                 per_seq((POOL_HIST_ROWS, POOL_WIDTH)), per_seq((keep, ATT_WIDTH)), per_seq((keep, ATT_WIDTH)))
    scratch = [pltpu.VMEM((ns, POOL_HIST_ROWS + tt, POOL_WIDTH), F32),
               pltpu.VMEM((ns, BAND + tt, ATT_WIDTH), BF16),
               pltpu.VMEM((ns, BAND + tt, ATT_WIDTH), BF16),
               pltpu.VMEM((ns * tt, ATT_WIDTH), BF16),
               pltpu.VMEM((ns * tt, POOL_WIDTH + ATT_WIDTH), F32),
               pltpu.VMEM((ns * tt, POOL_WIDTH + ATT_WIDTH), BF16),
               pltpu.VMEM((SCORE_SLOTS, 2 * rb, kb), F32)]
    return pl.pallas_call(
        functools.partial(_even_kernel, tt=tt, rb=rb, ns=ns, has_hist=has_hist, pos0=pos0),
        grid=(b // ns, nt), in_specs=in_specs, out_specs=out_specs, out_shape=out_shape,
        scratch_shapes=scratch,
        compiler_params=pltpu.CompilerParams(dimension_semantics=("parallel", "arbitrary"),
                                             vmem_limit_bytes=V7X_VMEM_LIMIT_BYTES),
        name="even_layer_hist" if has_hist else "even_layer",
    )(*args)


def _odd_kernel(*refs, tt, lc, ns, has_state):
    if has_state:
        x_ref, c0_ref, n0_ref, m0_ref = refs[0:4]
        refs = refs[4:]
    else:
        x_ref = refs[0]
        refs = refs[1:]
    (gpre_ref, gpost_ref, w_ref, wg_ref, bg_ref, gain_ref, wout_ref,
     y_ref, cout_ref, nout_ref, mout_ref, q_sc, k_sc, v_sc, oz_sc, mixed_sc, c_sc, n_sc, m_sc) = refs
    t = pl.program_id(1)
    w_, dh, nh = ML_WIDTH, ML_HEAD_DIM, ML_HEADS
    if ns == 1:
        groups = [[(0, c * lc)] for c in range(tt // lc)]
    else:
        assert tt == lc
        groups = [[(s, s * tt) for s in range(ns)]]
    n_groups = len(groups)
    g_rows = lc * len(groups[0])

    def group_rows(g):
        return slice(g * g_rows, (g + 1) * g_rows)

    @pl.when(t == 0)
    def _():
        if has_state:
            c_sc[...] = c0_ref[...]
            n_sc[...] = n0_ref[...]
            m_sc[...] = m0_ref[...]
        else:
            c_sc[...] = jnp.zeros(c_sc.shape, F32)
            n_sc[...] = jnp.zeros(n_sc.shape, F32)
            m_sc[...] = jnp.zeros(m_sc.shape, F32)

    def x_rows(g):
        if ns == 1:
            return x_ref[0, group_rows(g), :]
        return jnp.concatenate([x_ref[s] for s in range(ns)], axis=0)

    h = [_rmsnorm(x_rows(g), gpre_ref[...]).astype(BF16) for g in range(n_groups)]

    def project(g, part):
        rows = group_rows(g)
        if part == 0:
            q_sc[rows, :] = _dot_nt(h[g], w_ref[0:w_, :]).astype(BF16)
        elif part == 1:
            k_sc[rows, :] = (_dot_nt(h[g], w_ref[w_:2 * w_, :]) * (dh ** -0.5)).astype(BF16)
        elif part == 2:
            v_sc[rows, :] = _dot_nt(h[g], w_ref[2 * w_:3 * w_, :]).astype(BF16)
        else:
            oz_sc[rows, :] = (jax.nn.sigmoid(_dot_nt(h[g], w_ref[3 * w_:4 * w_, :]))
                              * _silu(_dot_nt(h[g], w_ref[4 * w_:5 * w_, :])))

    tri = jnp.where(lax.broadcasted_iota(jnp.int32, (lc, lc), 1) <= lax.broadcasted_iota(jnp.int32, (lc, lc), 0),
                    1.0, 0.0).astype(BF16)
    sel = jnp.where(lax.broadcasted_iota(jnp.int32, (8, LANES), 0)
                    == lax.broadcasted_iota(jnp.int32, (8, LANES), 1), 1.0, 0.0).astype(BF16)
    causal = (lax.broadcasted_iota(jnp.int32, (lc, lc), 1) <= lax.broadcasted_iota(jnp.int32, (lc, lc), 0))

    slabs = [(0, part) for part in range(4)]
    m_prev = [m_sc[s] for s in range(ns)]
    seg_gates = {}
    for g, segments in enumerate(groups):
        gates = _dot_nt(h[g], wg_ref[...]) + bg_ref[...]
        if slabs:
            project(*slabs.pop(0))
        for s, r0 in segments:
            lo = r0 - g * g_rows
            ig = gates[lo:lo + lc, 0:LANES]
            lf3 = _split3(jax.nn.log_sigmoid(gates[lo:lo + lc, LANES:2 * LANES]))
            b_all = _dot(tri, lf3[0]) + _dot(tri, lf3[1]) + _dot(tri, lf3[2])
            if slabs and (s, r0) == segments[0]:
                project(*slabs.pop(0))
            b3 = _split3(b_all)
            g3 = _split3(ig)
            brow = _dot_nt(sel, b3[0]) + _dot_nt(sel, b3[1]) + _dot_nt(sel, b3[2])
            igrow = _dot_nt(sel, g3[0]) + _dot_nt(sel, g3[1]) + _dot_nt(sel, g3[2])
            b_last_all = b_all[lc - 1:lc]
            g_all = b_last_all - b_all + ig
            m_new = jnp.maximum(b_last_all + m_prev[s], jnp.max(g_all, axis=0, keepdims=True))
            seg_gates[(s, r0)] = dict(b=b_all, inter=b_all + m_prev[s], brow=brow, igrow=igrow,
                                      decay=jnp.exp(b_last_all + m_prev[s] - m_new),
                                      wgt=jnp.exp(g_all - m_new))
            m_prev[s] = m_new
    for s in range(ns):
        m_sc[s] = m_prev[s]
    while slabs:
        project(*slabs.pop(0))

    def unit(seg, hd):
        s, r0 = seg
        rows = slice(r0, r0 + lc)
        ln = slice(hd * dh, (hd + 1) * dh)
        gts = seg_gates[seg]
        b_c = gts["b"][:, hd:hd + 1]
        b_r = gts["brow"][hd:hd + 1, :]
        ig_r = gts["igrow"][hd:hd + 1, :]
        inter = gts["inter"][:, hd:hd + 1]
        dmat = jnp.where(causal, b_c - b_r + ig_r, -jnp.inf)
        m_t = jnp.maximum(inter, jnp.max(dmat, axis=-1, keepdims=True))
        a = jnp.exp(inter - m_t)
        q = q_sc[rows, ln]
        k = k_sc[rows, ln]
        v = v_sc[rows, ln]
        sc = _dot_nt(q, k) * jnp.exp(dmat - m_t)
        c_old = c_sc[s, hd]
        n_old = n_sc[s, hd]
        num = a * _dot_nt(q, c_old.astype(BF16)) + _dot(sc.astype(BF16), v)
        qn = jnp.sum(q.astype(F32) * n_old, axis=-1, keepdims=True)
        den = a * qn + jnp.sum(sc, axis=-1, keepdims=True)
        hv = num / jnp.maximum(jnp.abs(den), jnp.exp(-m_t))
        mu = jnp.mean(hv, axis=-1, keepdims=True)
        dlt = hv - mu
        var = jnp.mean(dlt * dlt, axis=-1, keepdims=True)
        hn = dlt * lax.rsqrt(var + LN_EPS) * gain_ref[:, ln]
        mixed_sc[rows, ln] = (hn * oz_sc[rows, ln]).astype(BF16)
        decay = gts["decay"][:, hd:hd + 1]
        wgt = gts["wgt"][:, hd:hd + 1]
        vw = (v.astype(F32) * wgt).astype(BF16)
        c_sc[s, hd] = decay * c_old + _dot_tn(vw, k)
        n_sc[s, hd] = decay * n_old + jnp.sum(wgt * k.astype(F32), axis=0, keepdims=True)

    def out_project(g):
        res = x_rows(g) + _rmsnorm(_dot(mixed_sc[group_rows(g), :], wout_ref[...]), gpost_ref[...])
        if ns == 1:
            y_ref[0, group_rows(g), :] = res
        else:
            for s in range(ns):
                y_ref[s] = res[s * tt:(s + 1) * tt]

    n_parts = 4
    for g, segments in enumerate(groups):
        if g > 0:
            out_project(g - 1)
        for i, (seg, hd) in enumerate((seg, hd) for seg in segments for hd in range(nh)):
            if g + 1 < n_groups and i < n_parts:
                project(g + 1, i)
            unit(seg, hd)
    out_project(n_groups - 1)
    cout_ref[...] = c_sc[...]
    nout_ref[...] = n_sc[...]
    mout_ref[...] = m_sc[...]


def _odd_layer(x, state, gpre, gpost, w_main, w_g, b_g, gain, w_out, *, tt, lc, ns):
    b, t_len, d = x.shape
    nt = t_len // tt
    assert t_len % tt == 0 and tt % lc == 0 and b % ns == 0
    nh, dh = ML_HEADS, ML_HEAD_DIM
    has_state = state is not None

    def whole(shape):
        return pl.BlockSpec(shape, lambda i, j: (0,) * len(shape))

    def per_seq(shape):
        return pl.BlockSpec((ns,) + shape, lambda i, j: (i,) + (0,) * len(shape))

    st_specs = [per_seq((nh, dh, dh)), per_seq((nh, 1, dh)), per_seq((1, LANES))]
    in_specs = [pl.BlockSpec((ns, tt, d), lambda i, j: (i, j, 0))]
    args = [x]
    if has_state:
        in_specs += st_specs
        args += list(state)
    in_specs += [whole((1, d)), whole((1, d)), whole((5 * ML_WIDTH, d)),
                 whole((2 * LANES, d)), whole((1, 2 * LANES)), whole((1, ML_WIDTH)), whole((ML_WIDTH, d))]
    args += [gpre, gpost, w_main, w_g, b_g, gain, w_out]
    out_shape = (jax.ShapeDtypeStruct((b, t_len, d), F32),
                 jax.ShapeDtypeStruct((b, nh, dh, dh), F32),
                 jax.ShapeDtypeStruct((b, nh, 1, dh), F32),
                 jax.ShapeDtypeStruct((b, 1, LANES), F32))
    out_specs = (pl.BlockSpec((ns, tt, d), lambda i, j: (i, j, 0)), *st_specs)
    rows = ns * tt
    scratch = [pltpu.VMEM((rows, ML_WIDTH), BF16), pltpu.VMEM((rows, ML_WIDTH), BF16),
               pltpu.VMEM((rows, ML_WIDTH), BF16), pltpu.VMEM((rows, ML_WIDTH), F32),
               pltpu.VMEM((rows, ML_WIDTH), BF16),
               pltpu.VMEM((ns, nh, dh, dh), F32), pltpu.VMEM((ns, nh, 1, dh), F32),
               pltpu.VMEM((ns, 1, LANES), F32)]
    return pl.pallas_call(
        functools.partial(_odd_kernel, tt=tt, lc=lc, ns=ns, has_state=has_state),
        grid=(b // ns, nt), in_specs=in_specs, out_specs=out_specs, out_shape=out_shape,
        scratch_shapes=scratch,
        compiler_params=pltpu.CompilerParams(dimension_semantics=("parallel", "arbitrary"),
                                             vmem_limit_bytes=V7X_VMEM_LIMIT_BYTES),
        name="odd_layer_state" if has_state else "odd_layer",
    )(*args)


PROMPT_TILE = 512
ATT_BLOCK_CHUNKS = 2
ML_PROMPT_CHUNK = 256
SAMPLE_SEQS_PER_STEP = 8


def kernel(x_prompt, x_sample, cache_pool, cache_k, cache_v, state_C, state_n, state_m, norm_pre, norm_post,
           w_in_even, w_pool_mix, pool_scale, rel_bias, w_out_even, w_in_odd, b_gate_odd, mlstm_norm, w_out_odd):
    depth = norm_pre.shape[0]
    bp, dec_seq = x_prompt.shape[0], x_sample.shape[1]
    bs = x_sample.shape[0]
    nh = ML_HEADS
    xp, xs = x_prompt, x_sample
    outs = {name: [] for name in ("pool_p", "k_p", "v_p", "C_p", "n_p", "m_p",
                                  "pool_s", "k_s", "v_s", "C_s", "n_s", "m_s")}
    for layer in range(depth):
        gpre = norm_pre[layer][None, :]
        gpost = norm_post[layer][None, :]
        if layer % 2 == 0:
            e = layer // 2
            w_in = _cast_range_bf16(w_in_even[e], 0, EVEN_IN, axis=1, block=CAST_BLOCK)
            w_mix = w_pool_mix[e].astype(BF16)
            w_out = w_out_even[e].astype(BF16)
            scale = pool_scale[e][None, :]
            bias_p = _expand_bias(rel_bias[e], CHUNK, ATT_BLOCK_CHUNKS)
            bias_s = _expand_bias(rel_bias[e], dec_seq, 1)
            xp, pp, kp, vp = _even_layer(xp, None, gpre, gpost, w_in, w_mix, scale, bias_p, w_out,
                                         tt=PROMPT_TILE, rb=ATT_BLOCK_CHUNKS * CHUNK, ns=1, pos0=0)
            hist = (jnp.pad(cache_pool[e], ((0, 0), (POOL_HIST_ROWS - POOL_HIST, 0), (0, 0))),
                    cache_k[e].reshape(bs, BAND, ATT_WIDTH).astype(BF16),
                    cache_v[e].reshape(bs, BAND, ATT_WIDTH).astype(BF16))
            xs, ps, ks, vs = _even_layer(xs, hist, gpre, gpost, w_in, w_mix, scale, bias_s, w_out,
                                         tt=dec_seq, rb=dec_seq, ns=SAMPLE_SEQS_PER_STEP, pos0=PAST_LEN)
            outs["pool_p"].append(pp[:, POOL_HIST_ROWS - POOL_HIST:])
            outs["k_p"].append(kp.reshape(bp, -1, ATT_HEADS, ATT_HEAD_DIM))
            outs["v_p"].append(vp.reshape(bp, -1, ATT_HEADS, ATT_HEAD_DIM))
            outs["pool_s"].append(ps[:, POOL_HIST_ROWS - POOL_HIST:])
            outs["k_s"].append(ks.reshape(bs, -1, ATT_HEADS, ATT_HEAD_DIM))
            outs["v_s"].append(vs.reshape(bs, -1, ATT_HEADS, ATT_HEAD_DIM))
        else:
            o = layer // 2
            wt = jnp.swapaxes(w_in_odd[o], 0, 1)
            w_main = _cast_range_bf16(wt, 0, 5 * ML_WIDTH, axis=0, block=CAST_BLOCK)
            w_gates = _cast_range_bf16(wt, 5 * ML_WIDTH, 2 * nh, axis=0, block=2 * nh)
            pad_h = ((0, LANES - nh), (0, 0))
            w_g = jnp.concatenate([jnp.pad(w_gates[:nh], pad_h), jnp.pad(w_gates[nh:], pad_h)], axis=0)
            b_g = jnp.concatenate([jnp.pad(b_gate_odd[o][:nh], (0, LANES - nh)),
                                   jnp.pad(b_gate_odd[o][nh:], (0, LANES - nh))])[None, :]
            gain = mlstm_norm[o][None, :]
            w_out = w_out_odd[o].astype(BF16)
            xp, cp, np_, mp = _odd_layer(xp, None, gpre, gpost, w_main, w_g, b_g, gain, w_out,
                                         tt=PROMPT_TILE, lc=ML_PROMPT_CHUNK, ns=1)
            state = (state_C[o], state_n[o][:, :, None, :],
                     jnp.pad(state_m[o], ((0, 0), (0, LANES - nh)))[:, None, :])
            xs, cs, ns, ms = _odd_layer(xs, state, gpre, gpost, w_main, w_g, b_g, gain, w_out,
                                        tt=dec_seq, lc=dec_seq, ns=SAMPLE_SEQS_PER_STEP)
            outs["C_p"].append(cp)
            outs["n_p"].append(np_[:, :, 0, :])
            outs["m_p"].append(mp[:, 0, :nh])
            outs["C_s"].append(cs)
            outs["n_s"].append(ns[:, :, 0, :])
            outs["m_s"].append(ms[:, 0, :nh])
    return (xp, xs,
            jnp.stack(outs["pool_p"]), jnp.stack(outs["k_p"]), jnp.stack(outs["v_p"]),
            jnp.stack(outs["C_p"]), jnp.stack(outs["n_p"]), jnp.stack(outs["m_p"]),
            jnp.stack(outs["pool_s"]), jnp.stack(outs["k_s"]), jnp.stack(outs["v_s"]),
            jnp.stack(outs["C_s"]), jnp.stack(outs["n_s"]), jnp.stack(outs["m_s"]))
```

```python
import functools

import jax
import jax.numpy as jnp
from jax import lax
from jax.experimental import pallas as pl
from jax.experimental.pallas import tpu as pltpu

D_MODEL = 1024
PAST_LEN = 4096
CHUNK = 64
POOL_WINDOWS = (2, 4, 8, 16)
POOL_GROUPS = len(POOL_WINDOWS)
POOL_WIDTH = D_MODEL // 2
POOL_GROUP_DIM = POOL_WIDTH // POOL_GROUPS
POOL_HIST = max(POOL_WINDOWS) - 1
POOL_HIST_ROWS = 16
ATT_WIDTH = D_MODEL // 2
ATT_HEADS = 8
ATT_HEAD_DIM = ATT_WIDTH // ATT_HEADS
HEAD_PAIRS = ATT_HEADS // 2
BAND = 8 * CHUNK
REL_CLIP = 128
REL_TABLE = 2 * REL_CLIP + 1
REL_TABLE_PAD = 384
ML_WIDTH = D_MODEL
ML_HEADS = 4
ML_HEAD_DIM = ML_WIDTH // ML_HEADS
EVEN_IN = 2 * POOL_WIDTH + 4 * ATT_WIDTH
RMS_EPS = 1e-6
LN_EPS = 1e-6
LOG2_E = 1.4426950408889634

LANES = 128
BIAS_ROW_UNROLL = 8
SCORE_SLOTS = 3
CAST_BLOCK = 512
V7X_VMEM_BYTES = 64 * 1024 * 1024
V7X_VMEM_LIMIT_BYTES = V7X_VMEM_BYTES * 7 // 8

BF16 = jnp.bfloat16
F32 = jnp.float32


def _dot(a, b):
    return jnp.dot(a, b, preferred_element_type=F32)


def _dot_nt(a, b):
    return lax.dot_general(a, b, (((1,), (1,)), ((), ())), preferred_element_type=F32)


def _dot_tn(a, b):
    return lax.dot_general(a, b, (((0,), (0,)), ((), ())), preferred_element_type=F32)


def _split3(x):
    hi = x.astype(BF16)
    r1 = x - hi.astype(F32)
    mid = r1.astype(BF16)
    lo = (r1 - mid.astype(F32)).astype(BF16)
    return hi, mid, lo


def _rmsnorm(x, g):
    return x * lax.rsqrt(jnp.mean(x * x, axis=-1, keepdims=True) + RMS_EPS) * g


def _silu(x):
    return x * jax.nn.sigmoid(x)


def _cast_kernel(src_ref, dst_ref):
    dst_ref[...] = src_ref[...].astype(dst_ref.dtype)


def _cast_range_bf16(w, start, n, axis, block):
    assert n % block == 0 and start % block == 0 and start + n <= w.shape[axis]
    first = start // block
    if axis == 0:
        in_spec = pl.BlockSpec((block, w.shape[1]), lambda j: (first + j, 0))
        out_spec = pl.BlockSpec((block, w.shape[1]), lambda j: (j, 0))
        shape = (n, w.shape[1])
    else:
        in_spec = pl.BlockSpec((w.shape[0], block), lambda j: (0, first + j))
        out_spec = pl.BlockSpec((w.shape[0], block), lambda j: (0, j))
        shape = (w.shape[0], n)
    return pl.pallas_call(
        _cast_kernel, grid=(n // block,), in_specs=[in_spec], out_specs=out_spec,
        out_shape=jax.ShapeDtypeStruct(shape, BF16), name="cast_bf16",
    )(w)


def _bias_kernel(tab_ref, out_ref, *, lc, nb):
    rb = nb * lc
    kb = BAND + rb
    n_dist = rb - 1 + kb
    gw = (n_dist + LANES - 1) // LANES * LANES
    parts = _split3(tab_ref[...])
    dist = lax.broadcasted_iota(jnp.int32, (REL_TABLE_PAD, gw), 1) - (rb - 1)
    ti = lax.broadcasted_iota(jnp.int32, (REL_TABLE_PAD, gw), 0)
    onehot = jnp.where(ti == jnp.clip(BAND - dist, -REL_CLIP, REL_CLIP) + REL_CLIP, 1.0, 0.0).astype(BF16)
    by_dist = _dot(parts[0], onehot) + _dot(parts[1], onehot) + _dot(parts[2], onehot)
    key_chunk = lax.broadcasted_iota(jnp.int32, (ATT_HEADS, kb), 1) // lc

    def row(qi, carry):
        shift = lax.rem(gw - (rb - 1 - qi), gw)
        acc = pltpu.roll(by_dist, shift, axis=1)[:, 0:kb]
        q_chunk = qi // lc
        in_band = (key_chunk >= q_chunk) & (key_chunk <= q_chunk + BAND // lc)
        out_ref[qi] = jnp.where(in_band, acc * LOG2_E, -jnp.inf)
        return carry

    lax.fori_loop(0, rb, row, 0, unroll=BIAS_ROW_UNROLL)


def _expand_bias(table, lc, nb):
    rb = nb * lc
    kb = BAND + rb
    tab = jnp.pad(table.astype(F32), ((0, 0), (0, REL_TABLE_PAD - REL_TABLE)))
    out = pl.pallas_call(
        functools.partial(_bias_kernel, lc=lc, nb=nb),
        out_shape=jax.ShapeDtypeStruct((rb, ATT_HEADS, kb), F32),
        name="rel_bias_expand",
    )(tab)
    return out.transpose(1, 0, 2).reshape(HEAD_PAIRS, 2 * rb, kb)


def _even_kernel(*refs, tt, rb, ns, has_hist, cast_along, pos0):
    refs = list(refs)
    x_ref = refs.pop(0)
    if has_hist:
        hu_ref, hk_ref, hv_ref = refs[0:3]
        refs = refs[3:]
    gpre_ref, gpost_ref, win_ref, wmix_ref, scale_ref, bias_ref, wout_ref = refs[0:7]
    refs = refs[7:]
    if cast_along:
        wsrc_ref = refs.pop(0)
        y_ref, pool_ref, kout_ref, vout_ref, wdst_ref = refs[0:5]
        refs = refs[5:]
        wdst_ref[...] = wsrc_ref[...].astype(BF16)
    else:
        y_ref, pool_ref, kout_ref, vout_ref = refs[0:4]
        refs = refs[4:]
    uext, kext, vext, q_sc, gate_sc, mixed_sc, s_sc = refs
    t = pl.program_id(1)
    hr = POOL_HIST_ROWS
    kb = BAND + rb
    seqs = range(ns)

    def seq_rows(s):
        return slice(s * tt, (s + 1) * tt)

    @pl.when(t == 0)
    def _():
        for s in seqs:
            if has_hist:
                uext[s, 0:hr, :] = hu_ref[s]
                kext[s, 0:BAND, :] = hk_ref[s]
                vext[s, 0:BAND, :] = hv_ref[s]
            else:
                uext[s, 0:hr, :] = jnp.zeros((hr, POOL_WIDTH), F32)
                kext[s, 0:BAND, :] = jnp.zeros((BAND, ATT_WIDTH), BF16)
                vext[s, 0:BAND, :] = jnp.zeros((BAND, ATT_WIDTH), BF16)

    @pl.when(t > 0)
    def _():
        for s in seqs:
            uext[s, 0:hr, :] = uext[s, tt:tt + hr, :]
            kext[s, 0:BAND, :] = kext[s, tt:tt + BAND, :]
            vext[s, 0:BAND, :] = vext[s, tt:tt + BAND, :]

    def load_x():
        return x_ref[0] if ns == 1 else jnp.concatenate([x_ref[s] for s in seqs], axis=0)

    h = _rmsnorm(load_x(), gpre_ref[...]).astype(BF16)
    p_, a_ = POOL_WIDTH, ATT_WIDTH

    def project(part):
        cols = slice(part * a_, (part + 1) * a_)
        res = _dot(h, win_ref[:, cols])
        if part == 0:
            for s in seqs:
                uext[s, hr:hr + tt, :] = res[seq_rows(s)]
        elif part == 1:
            q_sc[...] = (res * (ATT_HEAD_DIM ** -0.5 * LOG2_E)).astype(BF16)
        elif part == 2:
            for s in seqs:
                kext[s, BAND:BAND + tt, :] = res[seq_rows(s)].astype(BF16)
                kout_ref[s] = res[seq_rows(s)]
        elif part == 3:
            for s in seqs:
                vext[s, BAND:BAND + tt, :] = res[seq_rows(s)].astype(BF16)
                vout_ref[s] = res[seq_rows(s)]
        else:
            gate_sc[:, (part - 4) * a_:(part - 3) * a_] = _silu(res)

    pos = pos0 + t * tt + lax.broadcasted_iota(jnp.int32, (tt, 1), 0)

    def pool_group(g):
        w = POOL_WINDOWS[g]
        ln = slice(g * POOL_GROUP_DIM, (g + 1) * POOL_GROUP_DIM)
        count = jnp.minimum(pos + 1, w).astype(F32)
        pooled = []
        for s in seqs:
            tok = uext[s, hr:hr + tt, ln]
            win_sum = tok
            for back in range(1, w):
                win_sum = win_sum + uext[s, hr - back:hr - back + tt, ln]
            pooled.append(win_sum / count - tok)
        pooled = pooled[0] if ns == 1 else jnp.concatenate(pooled, axis=0)
        mixed = _dot(pooled.astype(BF16), wmix_ref[g]) * scale_ref[:, ln]
        mixed_sc[:, ln] = (mixed * gate_sc[:, ln]).astype(BF16)

    assert p_ == a_ and POOL_GROUPS == 4
    project(0)
    project(4)
    for g, part in enumerate((1, 2, 3, 5)):
        project(part)
        pool_group(g)

    lane = lax.broadcasted_iota(jnp.int32, (rb, LANES), 1)
    even_head = lane < ATT_HEAD_DIM

    def attend(sequence_start):
        units = [(s, blk, p) for s in seqs for blk in range(tt // rb) for p in range(HEAD_PAIRS)]

        def band(blk):
            r0 = blk * rb
            k0 = BAND if sequence_start else r0
            c0 = BAND - r0 if sequence_start else 0
            return r0, k0, c0, kb - c0

        def stage_scores(i):
            s, blk, p = units[i]
            r0, k0, c0, kw = band(blk)
            ln = slice(p * LANES, (p + 1) * LANES)
            qp = q_sc[s * tt + r0:s * tt + r0 + rb, ln]
            zero = jnp.zeros_like(qp)
            q2 = jnp.concatenate([jnp.where(even_head, qp, zero), jnp.where(even_head, zero, qp)], axis=0)
            s_sc[i % SCORE_SLOTS, :, 0:kw] = _dot_nt(q2, kext[s, k0:k0 + kw, ln]) + bias_ref[p, :, c0:c0 + kw]

        def finish(i):
            s, blk, p = units[i]
            r0, k0, c0, kw = band(blk)
            ln = slice(p * LANES, (p + 1) * LANES)
            row_max = jnp.max(s_sc[i % SCORE_SLOTS, :, 0:kw], axis=-1, keepdims=True)
            e = jnp.exp2(s_sc[i % SCORE_SLOTS, :, 0:kw] - row_max)
            denom = jnp.sum(e, axis=-1, keepdims=True)
            o2 = _dot(e.astype(BF16), vext[s, k0:k0 + kw, ln]) / denom
            o = jnp.where(even_head, o2[0:rb], o2[rb:2 * rb])
            rows = slice(s * tt + r0, s * tt + r0 + rb)
            mo = slice(POOL_WIDTH + p * LANES, POOL_WIDTH + (p + 1) * LANES)
            mixed_sc[rows, mo] = (o * gate_sc[rows, mo]).astype(BF16)

        ahead = SCORE_SLOTS - 1
        for i in range(min(ahead, len(units))):
            stage_scores(i)
        for i in range(len(units)):
            if i + ahead < len(units):
                stage_scores(i + ahead)
            finish(i)

    if has_hist:
        attend(False)
    else:
        pl.when(t == 0)(functools.partial(attend, True))
        pl.when(t > 0)(functools.partial(attend, False))

    y = load_x() + _rmsnorm(_dot(mixed_sc[...], wout_ref[...]), gpost_ref[...])
    for s in seqs:
        y_ref[s] = y[seq_rows(s)]
        pool_ref[s] = uext[s, tt:tt + hr, :]


def _even_layer(x, hist, gpre, gpost, w_in, w_mix, scale, bias, w_out, *, tt, rb, ns, pos0, cast_along=None):
    b, t_len, d = x.shape
    has_hist = hist is not None
    nt = t_len // tt
    assert t_len % tt == 0 and tt % rb == 0 and b % ns == 0
    assert has_hist or rb % LANES == 0, "band start at a sequence start must stay lane aligned in the bias"
    keep = min(BAND, t_len)
    assert keep == tt, "key/value cache rows must be exactly the last time tile"
    assert nt == 1 or tt >= BAND
    kb = BAND + rb

    def whole(shape):
        return pl.BlockSpec(shape, lambda i, j: (0,) * len(shape))

    def per_seq(shape):
        return pl.BlockSpec((ns,) + shape, lambda i, j: (i, 0, 0))

    in_specs = [pl.BlockSpec((ns, tt, d), lambda i, j: (i, j, 0))]
    args = [x]
    if has_hist:
        in_specs += [per_seq((POOL_HIST_ROWS, POOL_WIDTH)), per_seq((BAND, ATT_WIDTH)), per_seq((BAND, ATT_WIDTH))]
        args += list(hist)
    in_specs += [whole((1, d)), whole((1, d)), whole((d, EVEN_IN)),
                 whole((POOL_GROUPS, POOL_GROUP_DIM, POOL_GROUP_DIM)), whole((1, POOL_WIDTH)),
                 whole((HEAD_PAIRS, 2 * rb, kb)), whole((POOL_WIDTH + ATT_WIDTH, d))]
    args += [gpre, gpost, w_in, w_mix, scale, bias, w_out]
    out_shape = [jax.ShapeDtypeStruct((b, t_len, d), F32),
                 jax.ShapeDtypeStruct((b, POOL_HIST_ROWS, POOL_WIDTH), F32),
                 jax.ShapeDtypeStruct((b, keep, ATT_WIDTH), F32),
                 jax.ShapeDtypeStruct((b, keep, ATT_WIDTH), F32)]
    out_specs = [pl.BlockSpec((ns, tt, d), lambda i, j: (i, j, 0)),
                 per_seq((POOL_HIST_ROWS, POOL_WIDTH)), per_seq((keep, ATT_WIDTH)), per_seq((keep, ATT_WIDTH))]
    if cast_along is not None:
        src, n_rows = cast_along
        steps = (b // ns) * nt
        share = n_rows // steps
        assert n_rows % steps == 0 and share % 16 == 0 and n_rows <= src.shape[0]
        share_spec = pl.BlockSpec((share, src.shape[1]), lambda i, j: (i * nt + j, 0))
        in_specs.append(share_spec)
        args.append(src)
        out_specs.append(share_spec)
        out_shape.append(jax.ShapeDtypeStruct((n_rows, src.shape[1]), BF16))
    scratch = [pltpu.VMEM((ns, POOL_HIST_ROWS + tt, POOL_WIDTH), F32),
               pltpu.VMEM((ns, BAND + tt, ATT_WIDTH), BF16),
               pltpu.VMEM((ns, BAND + tt, ATT_WIDTH), BF16),
               pltpu.VMEM((ns * tt, ATT_WIDTH), BF16),
               pltpu.VMEM((ns * tt, POOL_WIDTH + ATT_WIDTH), F32),
               pltpu.VMEM((ns * tt, POOL_WIDTH + ATT_WIDTH), BF16),
               pltpu.VMEM((SCORE_SLOTS, 2 * rb, kb), F32)]
    return pl.pallas_call(
        functools.partial(_even_kernel, tt=tt, rb=rb, ns=ns, has_hist=has_hist,
                          cast_along=cast_along is not None, pos0=pos0),
        grid=(b // ns, nt), in_specs=in_specs, out_specs=tuple(out_specs), out_shape=tuple(out_shape),
        scratch_shapes=scratch,
        compiler_params=pltpu.CompilerParams(dimension_semantics=("parallel", "arbitrary"),
                                             vmem_limit_bytes=V7X_VMEM_LIMIT_BYTES),
        name="even_layer_hist" if has_hist else "even_layer",
    )(*args)


def _odd_kernel(*refs, tt, lc, ns, has_state):
    if has_state:
        x_ref, c0_ref, n0_ref, m0_ref = refs[0:4]
        refs = refs[4:]
    else:
        x_ref = refs[0]
        refs = refs[1:]
    (gpre_ref, gpost_ref, w_ref, wg_ref, bg_ref, gain_ref, wout_ref,
     y_ref, cout_ref, nout_ref, mout_ref, q_sc, k_sc, v_sc, oz_sc, mixed_sc, c_sc, n_sc, m_sc) = refs
    t = pl.program_id(1)
    w_, dh, nh = ML_WIDTH, ML_HEAD_DIM, ML_HEADS
    if ns == 1:
        groups = [[(0, c * lc)] for c in range(tt // lc)]
    else:
        assert tt == lc
        groups = [[(s, s * tt) for s in range(ns)]]
    n_groups = len(groups)
    g_rows = lc * len(groups[0])

    def group_rows(g):
        return slice(g * g_rows, (g + 1) * g_rows)

    @pl.when(t == 0)
    def _():
        if has_state:
            c_sc[...] = c0_ref[...]
            n_sc[...] = n0_ref[...]
            m_sc[...] = m0_ref[...]
        else:
            c_sc[...] = jnp.zeros(c_sc.shape, F32)
            n_sc[...] = jnp.zeros(n_sc.shape, F32)
            m_sc[...] = jnp.zeros(m_sc.shape, F32)

    def x_rows(g):
        if ns == 1:
            return x_ref[0, group_rows(g), :]
        return jnp.concatenate([x_ref[s] for s in range(ns)], axis=0)

    h = [_rmsnorm(x_rows(g), gpre_ref[...]).astype(BF16) for g in range(n_groups)]

    def project(g, part):
        rows = group_rows(g)
        if part == 0:
            q_sc[rows, :] = _dot_nt(h[g], w_ref[0:w_, :]).astype(BF16)
        elif part == 1:
            k_sc[rows, :] = (_dot_nt(h[g], w_ref[w_:2 * w_, :]) * (dh ** -0.5)).astype(BF16)
        elif part == 2:
            v_sc[rows, :] = _dot_nt(h[g], w_ref[2 * w_:3 * w_, :]).astype(BF16)
        else:
            oz_sc[rows, :] = (jax.nn.sigmoid(_dot_nt(h[g], w_ref[3 * w_:4 * w_, :]))
                              * _silu(_dot_nt(h[g], w_ref[4 * w_:5 * w_, :])))

    tri = jnp.where(lax.broadcasted_iota(jnp.int32, (lc, lc), 1) <= lax.broadcasted_iota(jnp.int32, (lc, lc), 0),
                    1.0, 0.0).astype(BF16)
    sel = jnp.where(lax.broadcasted_iota(jnp.int32, (8, LANES), 0)
                    == lax.broadcasted_iota(jnp.int32, (8, LANES), 1), 1.0, 0.0).astype(BF16)
    causal = (lax.broadcasted_iota(jnp.int32, (lc, lc), 1) <= lax.broadcasted_iota(jnp.int32, (lc, lc), 0))

    slabs = [(0, part) for part in range(4)]
    m_prev = [m_sc[s] for s in range(ns)]
    seg_gates = {}
    for g, segments in enumerate(groups):
        gates = _dot_nt(h[g], wg_ref[...]) + bg_ref[...]
        if slabs:
            project(*slabs.pop(0))
        for s, r0 in segments:
            lo = r0 - g * g_rows
            ig = gates[lo:lo + lc, 0:LANES]
            lf3 = _split3(jax.nn.log_sigmoid(gates[lo:lo + lc, LANES:2 * LANES]))
            b_all = _dot(tri, lf3[0]) + _dot(tri, lf3[1]) + _dot(tri, lf3[2])
            if slabs and (s, r0) == segments[0]:
                project(*slabs.pop(0))
            b3 = _split3(b_all)
            g3 = _split3(ig)
            brow = _dot_nt(sel, b3[0]) + _dot_nt(sel, b3[1]) + _dot_nt(sel, b3[2])
            igrow = _dot_nt(sel, g3[0]) + _dot_nt(sel, g3[1]) + _dot_nt(sel, g3[2])
            b_last_all = b_all[lc - 1:lc]
            g_all = b_last_all - b_all + ig
            m_new = jnp.maximum(b_last_all + m_prev[s], jnp.max(g_all, axis=0, keepdims=True))
            seg_gates[(s, r0)] = dict(b=b_all, inter=b_all + m_prev[s], brow=brow, igrow=igrow,
                                      decay=jnp.exp(b_last_all + m_prev[s] - m_new),
                                      wgt=jnp.exp(g_all - m_new))
            m_prev[s] = m_new
    for s in range(ns):
        m_sc[s] = m_prev[s]
    while slabs:
        project(*slabs.pop(0))

    def unit(seg, hd):
        s, r0 = seg
        rows = slice(r0, r0 + lc)
        ln = slice(hd * dh, (hd + 1) * dh)
        gts = seg_gates[seg]
        b_c = gts["b"][:, hd:hd + 1]
        b_r = gts["brow"][hd:hd + 1, :]
        ig_r = gts["igrow"][hd:hd + 1, :]
        inter = gts["inter"][:, hd:hd + 1]
        dmat = jnp.where(causal, b_c - b_r + ig_r, -jnp.inf)
        m_t = jnp.maximum(inter, jnp.max(dmat, axis=-1, keepdims=True))
        a = jnp.exp(inter - m_t)
        q = q_sc[rows, ln]
        k = k_sc[rows, ln]
        v = v_sc[rows, ln]
        sc = _dot_nt(q, k) * jnp.exp(dmat - m_t)
        c_old = c_sc[s, hd]
        n_old = n_sc[s, hd]
        num = a * _dot_nt(q, c_old.astype(BF16)) + _dot(sc.astype(BF16), v)
        qn = jnp.sum(q.astype(F32) * n_old, axis=-1, keepdims=True)
        den = a * qn + jnp.sum(sc, axis=-1, keepdims=True)
        hv = num / jnp.maximum(jnp.abs(den), jnp.exp(-m_t))
        mu = jnp.mean(hv, axis=-1, keepdims=True)
        dlt = hv - mu
        var = jnp.mean(dlt * dlt, axis=-1, keepdims=True)
        hn = dlt * lax.rsqrt(var + LN_EPS) * gain_ref[:, ln]
        mixed_sc[rows, ln] = (hn * oz_sc[rows, ln]).astype(BF16)
        decay = gts["decay"][:, hd:hd + 1]
        wgt = gts["wgt"][:, hd:hd + 1]
        vw = (v.astype(F32) * wgt).astype(BF16)
        c_sc[s, hd] = decay * c_old + _dot_tn(vw, k)
        n_sc[s, hd] = decay * n_old + jnp.sum(wgt * k.astype(F32), axis=0, keepdims=True)

    def out_project(g):
        res = x_rows(g) + _rmsnorm(_dot(mixed_sc[group_rows(g), :], wout_ref[...]), gpost_ref[...])
        if ns == 1:
            y_ref[0, group_rows(g), :] = res
        else:
            for s in range(ns):
                y_ref[s] = res[s * tt:(s + 1) * tt]

    n_parts = 4
    for g, segments in enumerate(groups):
        if g > 0:
            out_project(g - 1)
        for i, (seg, hd) in enumerate((seg, hd) for seg in segments for hd in range(nh)):
            if g + 1 < n_groups and i < n_parts:
                project(g + 1, i)
            unit(seg, hd)
    out_project(n_groups - 1)
    cout_ref[...] = c_sc[...]
    nout_ref[...] = n_sc[...]
    mout_ref[...] = m_sc[...]


def _odd_layer(x, state, gpre, gpost, w_main, w_g, b_g, gain, w_out, *, tt, lc, ns):
    b, t_len, d = x.shape
    nt = t_len // tt
    assert t_len % tt == 0 and tt % lc == 0 and b % ns == 0
    nh, dh = ML_HEADS, ML_HEAD_DIM
    has_state = state is not None

    def whole(shape):
        return pl.BlockSpec(shape, lambda i, j: (0,) * len(shape))

    def per_seq(shape):
        return pl.BlockSpec((ns,) + shape, lambda i, j: (i,) + (0,) * len(shape))

    st_specs = [per_seq((nh, dh, dh)), per_seq((nh, 1, dh)), per_seq((1, LANES))]
    in_specs = [pl.BlockSpec((ns, tt, d), lambda i, j: (i, j, 0))]
    args = [x]
    if has_state:
        in_specs += st_specs
        args += list(state)
    in_specs += [whole((1, d)), whole((1, d)), whole((5 * ML_WIDTH, d)),
                 whole((2 * LANES, d)), whole((1, 2 * LANES)), whole((1, ML_WIDTH)), whole((ML_WIDTH, d))]
    args += [gpre, gpost, w_main, w_g, b_g, gain, w_out]
    out_shape = (jax.ShapeDtypeStruct((b, t_len, d), F32),
                 jax.ShapeDtypeStruct((b, nh, dh, dh), F32),
                 jax.ShapeDtypeStruct((b, nh, 1, dh), F32),
                 jax.ShapeDtypeStruct((b, 1, LANES), F32))
    out_specs = (pl.BlockSpec((ns, tt, d), lambda i, j: (i, j, 0)), *st_specs)
    rows = ns * tt
    scratch = [pltpu.VMEM((rows, ML_WIDTH), BF16), pltpu.VMEM((rows, ML_WIDTH), BF16),
               pltpu.VMEM((rows, ML_WIDTH), BF16), pltpu.VMEM((rows, ML_WIDTH), F32),
               pltpu.VMEM((rows, ML_WIDTH), BF16),
               pltpu.VMEM((ns, nh, dh, dh), F32), pltpu.VMEM((ns, nh, 1, dh), F32),
               pltpu.VMEM((ns, 1, LANES), F32)]
    return pl.pallas_call(
        functools.partial(_odd_kernel, tt=tt, lc=lc, ns=ns, has_state=has_state),
        grid=(b // ns, nt), in_specs=in_specs, out_specs=out_specs, out_shape=out_shape,
        scratch_shapes=scratch,
        compiler_params=pltpu.CompilerParams(dimension_semantics=("parallel", "arbitrary"),
                                             vmem_limit_bytes=V7X_VMEM_LIMIT_BYTES),
        name="odd_layer_state" if has_state else "odd_layer",
    )(*args)


PROMPT_TILE = 512
ATT_BLOCK_CHUNKS = 2
ML_PROMPT_CHUNK = 256
SAMPLE_SEQS_PER_STEP = 8


def kernel(x_prompt, x_sample, cache_pool, cache_k, cache_v, state_C, state_n, state_m, norm_pre, norm_post,
           w_in_even, w_pool_mix, pool_scale, rel_bias, w_out_even, w_in_odd, b_gate_odd, mlstm_norm, w_out_odd):
    depth = norm_pre.shape[0]
    bp, dec_seq = x_prompt.shape[0], x_sample.shape[1]
    bs = x_sample.shape[0]
    nh = ML_HEADS
    xp, xs = x_prompt, x_sample
    outs = {name: [] for name in ("pool_p", "k_p", "v_p", "C_p", "n_p", "m_p",
                                  "pool_s", "k_s", "v_s", "C_s", "n_s", "m_s")}
    for layer in range(depth):
        gpre = norm_pre[layer][None, :]
        gpost = norm_post[layer][None, :]
        if layer % 2 == 0:
            e = layer // 2
            w_in = _cast_range_bf16(w_in_even[e], 0, EVEN_IN, axis=1, block=CAST_BLOCK)
            w_mix = w_pool_mix[e].astype(BF16)
            w_out = w_out_even[e].astype(BF16)
            scale = pool_scale[e][None, :]
            bias_p = _expand_bias(rel_bias[e], CHUNK, ATT_BLOCK_CHUNKS)
            bias_s = _expand_bias(rel_bias[e], dec_seq, 1)
            wt_next = jnp.swapaxes(w_in_odd[(layer + 1) // 2], 0, 1) if layer + 1 < depth else None
            res = _even_layer(xp, None, gpre, gpost, w_in, w_mix, scale, bias_p, w_out,
                              tt=PROMPT_TILE, rb=ATT_BLOCK_CHUNKS * CHUNK, ns=1, pos0=0,
                              cast_along=None if wt_next is None else (wt_next, 5 * ML_WIDTH))
            xp, pp, kp, vp = res[0:4]
            w_main_next = res[4] if wt_next is not None else None
            hist = (jnp.pad(cache_pool[e], ((0, 0), (POOL_HIST_ROWS - POOL_HIST, 0), (0, 0))),
                    cache_k[e].reshape(bs, BAND, ATT_WIDTH).astype(BF16),
                    cache_v[e].reshape(bs, BAND, ATT_WIDTH).astype(BF16))
            xs, ps, ks, vs = _even_layer(xs, hist, gpre, gpost, w_in, w_mix, scale, bias_s, w_out,
                                         tt=dec_seq, rb=dec_seq, ns=SAMPLE_SEQS_PER_STEP, pos0=PAST_LEN)
            outs["pool_p"].append(pp[:, POOL_HIST_ROWS - POOL_HIST:])
            outs["k_p"].append(kp.reshape(bp, -1, ATT_HEADS, ATT_HEAD_DIM))
            outs["v_p"].append(vp.reshape(bp, -1, ATT_HEADS, ATT_HEAD_DIM))
            outs["pool_s"].append(ps[:, POOL_HIST_ROWS - POOL_HIST:])
            outs["k_s"].append(ks.reshape(bs, -1, ATT_HEADS, ATT_HEAD_DIM))
            outs["v_s"].append(vs.reshape(bs, -1, ATT_HEADS, ATT_HEAD_DIM))
        else:
            o = layer // 2
            wt = jnp.swapaxes(w_in_odd[o], 0, 1)
            w_main = w_main_next
            w_gates = _cast_range_bf16(wt, 5 * ML_WIDTH, 2 * nh, axis=0, block=2 * nh)
            pad_h = ((0, LANES - nh), (0, 0))
            w_g = jnp.concatenate([jnp.pad(w_gates[:nh], pad_h), jnp.pad(w_gates[nh:], pad_h)], axis=0)
            b_g = jnp.concatenate([jnp.pad(b_gate_odd[o][:nh], (0, LANES - nh)),
                                   jnp.pad(b_gate_odd[o][nh:], (0, LANES - nh))])[None, :]
            gain = mlstm_norm[o][None, :]
            w_out = w_out_odd[o].astype(BF16)
            xp, cp, np_, mp = _odd_layer(xp, None, gpre, gpost, w_main, w_g, b_g, gain, w_out,
                                         tt=PROMPT_TILE, lc=ML_PROMPT_CHUNK, ns=1)
            state = (state_C[o], state_n[o][:, :, None, :],
                     jnp.pad(state_m[o], ((0, 0), (0, LANES - nh)))[:, None, :])
            xs, cs, ns, ms = _odd_layer(xs, state, gpre, gpost, w_main, w_g, b_g, gain, w_out,
                                        tt=dec_seq, lc=dec_seq, ns=SAMPLE_SEQS_PER_STEP)
            outs["C_p"].append(cp)
            outs["n_p"].append(np_[:, :, 0, :])
            outs["m_p"].append(mp[:, 0, :nh])
            outs["C_s"].append(cs)
            outs["n_s"].append(ns[:, :, 0, :])
            outs["m_s"].append(ms[:, 0, :nh])
    return (xp, xs,
            jnp.stack(outs["pool_p"]), jnp.stack(outs["k_p"]), jnp.stack(outs["v_p"]),
            jnp.stack(outs["C_p"]), jnp.stack(outs["n_p"]), jnp.stack(outs["m_p"]),
            jnp.stack(outs["pool_s"]), jnp.stack(outs["k_s"]), jnp.stack(outs["v_s"]),
            jnp.stack(outs["C_s"]), jnp.stack(outs["n_s"]), jnp.stack(outs["m_s"]))
```

```python
import functools

import jax
import jax.numpy as jnp
from jax import lax
from jax.experimental import pallas as pl
from jax.experimental.pallas import tpu as pltpu

D_MODEL = 1024
PAST_LEN = 4096
CHUNK = 64
POOL_WINDOWS = (2, 4, 8, 16)
POOL_GROUPS = len(POOL_WINDOWS)
POOL_WIDTH = D_MODEL // 2
POOL_GROUP_DIM = POOL_WIDTH // POOL_GROUPS
POOL_HIST = max(POOL_WINDOWS) - 1
POOL_HIST_ROWS = 16
ATT_WIDTH = D_MODEL // 2
ATT_HEADS = 8
ATT_HEAD_DIM = ATT_WIDTH // ATT_HEADS
HEAD_PAIRS = ATT_HEADS // 2
BAND = 8 * CHUNK
REL_CLIP = 128
REL_TABLE = 2 * REL_CLIP + 1
REL_TABLE_PAD = 384
ML_WIDTH = D_MODEL
ML_HEADS = 4
ML_HEAD_DIM = ML_WIDTH // ML_HEADS
EVEN_IN = 2 * POOL_WIDTH + 4 * ATT_WIDTH
RMS_EPS = 1e-6
LN_EPS = 1e-6
LOG2_E = 1.4426950408889634

LANES = 128
BIAS_ROW_UNROLL = 8
SCORE_SLOTS = 3
CAST_BLOCK = 512
V7X_VMEM_BYTES = 64 * 1024 * 1024
V7X_VMEM_LIMIT_BYTES = V7X_VMEM_BYTES * 7 // 8

BF16 = jnp.bfloat16
F32 = jnp.float32


def _dot(a, b):
    return jnp.dot(a, b, preferred_element_type=F32)


def _dot_nt(a, b):
    return lax.dot_general(a, b, (((1,), (1,)), ((), ())), preferred_element_type=F32)


def _dot_tn(a, b):
    return lax.dot_general(a, b, (((0,), (0,)), ((), ())), preferred_element_type=F32)


def _split3(x):
    hi = x.astype(BF16)
    r1 = x - hi.astype(F32)
    mid = r1.astype(BF16)
    lo = (r1 - mid.astype(F32)).astype(BF16)
    return hi, mid, lo


def _rmsnorm(x, g):
    return x * lax.rsqrt(jnp.mean(x * x, axis=-1, keepdims=True) + RMS_EPS) * g


def _silu(x):
    return x * jax.nn.sigmoid(x)


def _cast_kernel(src_ref, dst_ref):
    dst_ref[...] = src_ref[...].astype(dst_ref.dtype)


def _cast_range_bf16(w, start, n, axis, block):
    assert n % block == 0 and start % block == 0 and start + n <= w.shape[axis]
    first = start // block
    if axis == 0:
        in_spec = pl.BlockSpec((block, w.shape[1]), lambda j: (first + j, 0))
        out_spec = pl.BlockSpec((block, w.shape[1]), lambda j: (j, 0))
        shape = (n, w.shape[1])
    else:
        in_spec = pl.BlockSpec((w.shape[0], block), lambda j: (0, first + j))
        out_spec = pl.BlockSpec((w.shape[0], block), lambda j: (0, j))
        shape = (w.shape[0], n)
    return pl.pallas_call(
        _cast_kernel, grid=(n // block,), in_specs=[in_spec], out_specs=out_spec,
        out_shape=jax.ShapeDtypeStruct(shape, BF16), name="cast_bf16",
    )(w)


def _bias_kernel(tab_ref, out_ref, *, lc, nb):
    rb = nb * lc
    kb = BAND + rb
    n_dist = rb - 1 + kb
    gw = (n_dist + LANES - 1) // LANES * LANES
    parts = _split3(tab_ref[...])
    dist = lax.broadcasted_iota(jnp.int32, (REL_TABLE_PAD, gw), 1) - (rb - 1)
    ti = lax.broadcasted_iota(jnp.int32, (REL_TABLE_PAD, gw), 0)
    onehot = jnp.where(ti == jnp.clip(BAND - dist, -REL_CLIP, REL_CLIP) + REL_CLIP, 1.0, 0.0).astype(BF16)
    by_dist = _dot(parts[0], onehot) + _dot(parts[1], onehot) + _dot(parts[2], onehot)
    key_chunk = lax.broadcasted_iota(jnp.int32, (ATT_HEADS, kb), 1) // lc

    def row(qi, carry):
        shift = lax.rem(gw - (rb - 1 - qi), gw)
        acc = pltpu.roll(by_dist, shift, axis=1)[:, 0:kb]
        q_chunk = qi // lc
        in_band = (key_chunk >= q_chunk) & (key_chunk <= q_chunk + BAND // lc)
        out_ref[qi] = jnp.where(in_band, acc * LOG2_E, -jnp.inf)
        return carry

    lax.fori_loop(0, rb, row, 0, unroll=BIAS_ROW_UNROLL)


def _expand_bias(table, lc, nb):
    rb = nb * lc
    kb = BAND + rb
    tab = jnp.pad(table.astype(F32), ((0, 0), (0, REL_TABLE_PAD - REL_TABLE)))
    out = pl.pallas_call(
        functools.partial(_bias_kernel, lc=lc, nb=nb),
        out_shape=jax.ShapeDtypeStruct((rb, ATT_HEADS, kb), F32),
        name="rel_bias_expand",
    )(tab)
    return out.transpose(1, 0, 2).reshape(HEAD_PAIRS, 2 * rb, kb)


def _even_kernel(*refs, tt, rb, ns, has_hist, cast_along, pos0):
    refs = list(refs)
    x_ref = refs.pop(0)
    if has_hist:
        hu_ref, hk_ref, hv_ref = refs[0:3]
        refs = refs[3:]
    gpre_ref, gpost_ref, win_ref, wmix_ref, scale_ref, bias_ref, wout_ref = refs[0:7]
    refs = refs[7:]
    if cast_along:
        wsrc_ref = refs.pop(0)
        y_ref, pool_ref, kout_ref, vout_ref, wdst_ref = refs[0:5]
        refs = refs[5:]
        wdst_ref[...] = wsrc_ref[...].astype(BF16)
    else:
        y_ref, pool_ref, kout_ref, vout_ref = refs[0:4]
        refs = refs[4:]
    uext, kext, vext, q_sc, gate_sc, mixed_sc, s_sc = refs
    t = pl.program_id(1)
    hr = POOL_HIST_ROWS
    kb = BAND + rb
    seqs = range(ns)

    def seq_rows(s):
        return slice(s * tt, (s + 1) * tt)

    @pl.when(t == 0)
    def _():
        for s in seqs:
            if has_hist:
                uext[s, 0:hr, :] = hu_ref[s]
                kext[s, 0:BAND, :] = hk_ref[s]
                vext[s, 0:BAND, :] = hv_ref[s]
            else:
                uext[s, 0:hr, :] = jnp.zeros((hr, POOL_WIDTH), F32)
                kext[s, 0:BAND, :] = jnp.zeros((BAND, ATT_WIDTH), BF16)
                vext[s, 0:BAND, :] = jnp.zeros((BAND, ATT_WIDTH), BF16)

    @pl.when(t > 0)
    def _():
        for s in seqs:
            uext[s, 0:hr, :] = uext[s, tt:tt + hr, :]
            kext[s, 0:BAND, :] = kext[s, tt:tt + BAND, :]
            vext[s, 0:BAND, :] = vext[s, tt:tt + BAND, :]

    def load_x():
        return x_ref[0] if ns == 1 else jnp.concatenate([x_ref[s] for s in seqs], axis=0)

    h = _rmsnorm(load_x(), gpre_ref[...]).astype(BF16)
    p_, a_ = POOL_WIDTH, ATT_WIDTH

    def project(part):
        cols = slice(part * a_, (part + 1) * a_)
        res = _dot(h, win_ref[:, cols])
        if part == 0:
            for s in seqs:
                uext[s, hr:hr + tt, :] = res[seq_rows(s)]
        elif part == 1:
            q_sc[...] = (res * (ATT_HEAD_DIM ** -0.5 * LOG2_E)).astype(BF16)
        elif part == 2:
            for s in seqs:
                kext[s, BAND:BAND + tt, :] = res[seq_rows(s)].astype(BF16)
                kout_ref[s] = res[seq_rows(s)]
        elif part == 3:
            for s in seqs:
                vext[s, BAND:BAND + tt, :] = res[seq_rows(s)].astype(BF16)
                vout_ref[s] = res[seq_rows(s)]
        else:
            gate_sc[:, (part - 4) * a_:(part - 3) * a_] = _silu(res)

    pos = pos0 + t * tt + lax.broadcasted_iota(jnp.int32, (tt, 1), 0)

    def pool_group(g):
        w = POOL_WINDOWS[g]
        ln = slice(g * POOL_GROUP_DIM, (g + 1) * POOL_GROUP_DIM)
        count = jnp.minimum(pos + 1, w).astype(F32)
        pooled = []
        for s in seqs:
            tok = uext[s, hr:hr + tt, ln]
            win_sum = tok
            for back in range(1, w):
                win_sum = win_sum + uext[s, hr - back:hr - back + tt, ln]
            pooled.append(win_sum / count - tok)
        pooled = pooled[0] if ns == 1 else jnp.concatenate(pooled, axis=0)
        mixed = _dot(pooled.astype(BF16), wmix_ref[g]) * scale_ref[:, ln]
        mixed_sc[:, ln] = (mixed * gate_sc[:, ln]).astype(BF16)

    assert p_ == a_ and POOL_GROUPS == 4
    project(0)
    project(4)
    for g, part in enumerate((1, 2, 3, 5)):
        project(part)
        pool_group(g)

    lane = lax.broadcasted_iota(jnp.int32, (rb, LANES), 1)
    even_head = lane < ATT_HEAD_DIM

    def attend(sequence_start):
        units = [(s, blk, p) for s in seqs for blk in range(tt // rb) for p in range(HEAD_PAIRS)]

        def band(blk):
            r0 = blk * rb
            k0 = BAND if sequence_start else r0
            c0 = BAND - r0 if sequence_start else 0
            return r0, k0, c0, kb - c0

        def stage_scores(i):
            s, blk, p = units[i]
            r0, k0, c0, kw = band(blk)
            ln = slice(p * LANES, (p + 1) * LANES)
            qp = q_sc[s * tt + r0:s * tt + r0 + rb, ln]
            zero = jnp.zeros_like(qp)
            q2 = jnp.concatenate([jnp.where(even_head, qp, zero), jnp.where(even_head, zero, qp)], axis=0)
            s_sc[i % SCORE_SLOTS, :, 0:kw] = _dot_nt(q2, kext[s, k0:k0 + kw, ln]) + bias_ref[p, :, c0:c0 + kw]

        def finish(i):
            s, blk, p = units[i]
            r0, k0, c0, kw = band(blk)
            ln = slice(p * LANES, (p + 1) * LANES)
            row_max = jnp.max(s_sc[i % SCORE_SLOTS, :, 0:kw], axis=-1, keepdims=True)
            e = jnp.exp2(s_sc[i % SCORE_SLOTS, :, 0:kw] - row_max)
            denom = jnp.sum(e, axis=-1, keepdims=True)
            o2 = _dot(e.astype(BF16), vext[s, k0:k0 + kw, ln]) / denom
            o = jnp.where(even_head, o2[0:rb], o2[rb:2 * rb])
            rows = slice(s * tt + r0, s * tt + r0 + rb)
            mo = slice(POOL_WIDTH + p * LANES, POOL_WIDTH + (p + 1) * LANES)
            mixed_sc[rows, mo] = (o * gate_sc[rows, mo]).astype(BF16)

        ahead = SCORE_SLOTS - 1
        for i in range(min(ahead, len(units))):
            stage_scores(i)
        for i in range(len(units)):
            if i + ahead < len(units):
                stage_scores(i + ahead)
            finish(i)

    if has_hist:
        attend(False)
    else:
        pl.when(t == 0)(functools.partial(attend, True))
        pl.when(t > 0)(functools.partial(attend, False))

    y = load_x() + _rmsnorm(_dot(mixed_sc[...], wout_ref[...]), gpost_ref[...])
    for s in seqs:
        y_ref[s] = y[seq_rows(s)]
        pool_ref[s] = uext[s, tt:tt + hr, :]


def _even_layer(x, hist, gpre, gpost, w_in, w_mix, scale, bias, w_out, *, tt, rb, ns, pos0, cast_along=None):
    b, t_len, d = x.shape
    has_hist = hist is not None
    nt = t_len // tt
    assert t_len % tt == 0 and tt % rb == 0 and b % ns == 0
    assert has_hist or rb % LANES == 0, "band start at a sequence start must stay lane aligned in the bias"
    keep = min(BAND, t_len)
    assert keep == tt, "key/value cache rows must be exactly the last time tile"
    assert nt == 1 or tt >= BAND
    kb = BAND + rb

    def whole(shape):
        return pl.BlockSpec(shape, lambda i, j: (0,) * len(shape))

    def per_seq(shape):
        return pl.BlockSpec((ns,) + shape, lambda i, j: (i, 0, 0))

    in_specs = [pl.BlockSpec((ns, tt, d), lambda i, j: (i, j, 0))]
    args = [x]
    if has_hist:
        in_specs += [per_seq((POOL_HIST_ROWS, POOL_WIDTH)), per_seq((BAND, ATT_WIDTH)), per_seq((BAND, ATT_WIDTH))]
        args += list(hist)
    in_specs += [whole((1, d)), whole((1, d)), whole((d, EVEN_IN)),
                 whole((POOL_GROUPS, POOL_GROUP_DIM, POOL_GROUP_DIM)), whole((1, POOL_WIDTH)),
                 whole((HEAD_PAIRS, 2 * rb, kb)), whole((POOL_WIDTH + ATT_WIDTH, d))]
    args += [gpre, gpost, w_in, w_mix, scale, bias, w_out]
    out_shape = [jax.ShapeDtypeStruct((b, t_len, d), F32),
                 jax.ShapeDtypeStruct((b, POOL_HIST_ROWS, POOL_WIDTH), F32),
                 jax.ShapeDtypeStruct((b, keep, ATT_WIDTH), F32),
                 jax.ShapeDtypeStruct((b, keep, ATT_WIDTH), F32)]
    out_specs = [pl.BlockSpec((ns, tt, d), lambda i, j: (i, j, 0)),
                 per_seq((POOL_HIST_ROWS, POOL_WIDTH)), per_seq((keep, ATT_WIDTH)), per_seq((keep, ATT_WIDTH))]
    if cast_along is not None:
        src, n_rows = cast_along
        steps = (b // ns) * nt
        share = n_rows // steps
        assert n_rows % steps == 0 and share % 16 == 0 and n_rows <= src.shape[0]
        share_spec = pl.BlockSpec((share, src.shape[1]), lambda i, j: (i * nt + j, 0))
        in_specs.append(share_spec)
        args.append(src)
        out_specs.append(share_spec)
        out_shape.append(jax.ShapeDtypeStruct((n_rows, src.shape[1]), BF16))
    scratch = [pltpu.VMEM((ns, POOL_HIST_ROWS + tt, POOL_WIDTH), F32),
               pltpu.VMEM((ns, BAND + tt, ATT_WIDTH), BF16),
               pltpu.VMEM((ns, BAND + tt, ATT_WIDTH), BF16),
               pltpu.VMEM((ns * tt, ATT_WIDTH), BF16),
               pltpu.VMEM((ns * tt, POOL_WIDTH + ATT_WIDTH), F32),
               pltpu.VMEM((ns * tt, POOL_WIDTH + ATT_WIDTH), BF16),
               pltpu.VMEM((SCORE_SLOTS, 2 * rb, kb), F32)]
    return pl.pallas_call(
        functools.partial(_even_kernel, tt=tt, rb=rb, ns=ns, has_hist=has_hist,
                          cast_along=cast_along is not None, pos0=pos0),
        grid=(b // ns, nt), in_specs=in_specs, out_specs=tuple(out_specs), out_shape=tuple(out_shape),
        scratch_shapes=scratch,
        compiler_params=pltpu.CompilerParams(dimension_semantics=("parallel", "arbitrary"),
                                             vmem_limit_bytes=V7X_VMEM_LIMIT_BYTES),
        name="even_layer_hist" if has_hist else "even_layer",
    )(*args)


def _odd_kernel(*refs, tt, lc, ns, has_state):
    if has_state:
        x_ref, c0_ref, n0_ref, m0_ref = refs[0:4]
        refs = refs[4:]
    else:
        x_ref = refs[0]
        refs = refs[1:]
    (gpre_ref, gpost_ref, w_ref, wg_ref, bg_ref, gain_ref, wout_ref,
     y_ref, cout_ref, nout_ref, mout_ref, q_sc, k_sc, v_sc, oz_sc, mixed_sc, c_sc, n_sc, m_sc) = refs
    t = pl.program_id(1)
    w_, dh, nh = ML_WIDTH, ML_HEAD_DIM, ML_HEADS
    if ns == 1:
        groups = [[(0, c * lc)] for c in range(tt // lc)]
    else:
        assert tt == lc
        groups = [[(s, s * tt) for s in range(ns)]]
    n_groups = len(groups)
    g_rows = lc * len(groups[0])

    def group_rows(g):
        return slice(g * g_rows, (g + 1) * g_rows)

    @pl.when(t == 0)
    def _():
        if has_state:
            c_sc[...] = c0_ref[...]
            n_sc[...] = n0_ref[...]
            m_sc[...] = m0_ref[...]
        else:
            c_sc[...] = jnp.zeros(c_sc.shape, F32)
            n_sc[...] = jnp.zeros(n_sc.shape, F32)
            m_sc[...] = jnp.zeros(m_sc.shape, F32)

    def x_rows(g):
        if ns == 1:
            return x_ref[0, group_rows(g), :]
        return jnp.concatenate([x_ref[s] for s in range(ns)], axis=0)

    h = [_rmsnorm(x_rows(g), gpre_ref[...]).astype(BF16) for g in range(n_groups)]

    def project(g, part):
        rows = group_rows(g)
        if part == 0:
            q_sc[rows, :] = _dot_nt(h[g], w_ref[0:w_, :]).astype(BF16)
        elif part == 1:
            k_sc[rows, :] = (_dot_nt(h[g], w_ref[w_:2 * w_, :]) * (dh ** -0.5)).astype(BF16)
        elif part == 2:
            v_sc[rows, :] = _dot_nt(h[g], w_ref[2 * w_:3 * w_, :]).astype(BF16)
        else:
            oz_sc[rows, :] = (jax.nn.sigmoid(_dot_nt(h[g], w_ref[3 * w_:4 * w_, :]))
                              * _silu(_dot_nt(h[g], w_ref[4 * w_:5 * w_, :])))

    tri = jnp.where(lax.broadcasted_iota(jnp.int32, (lc, lc), 1) <= lax.broadcasted_iota(jnp.int32, (lc, lc), 0),
                    1.0, 0.0).astype(BF16)
    sel = jnp.where(lax.broadcasted_iota(jnp.int32, (8, LANES), 0)
                    == lax.broadcasted_iota(jnp.int32, (8, LANES), 1), 1.0, 0.0).astype(BF16)
    causal = (lax.broadcasted_iota(jnp.int32, (lc, lc), 1) <= lax.broadcasted_iota(jnp.int32, (lc, lc), 0))

    slabs = [(0, part) for part in range(4)]
    m_prev = [m_sc[s] for s in range(ns)]
    seg_gates = {}
    for g, segments in enumerate(groups):
        gates = _dot_nt(h[g], wg_ref[...]) + bg_ref[...]
        if slabs:
            project(*slabs.pop(0))
        for s, r0 in segments:
            lo = r0 - g * g_rows
            ig = gates[lo:lo + lc, 0:LANES]
            lf3 = _split3(jax.nn.log_sigmoid(gates[lo:lo + lc, LANES:2 * LANES]))
            b_all = _dot(tri, lf3[0]) + _dot(tri, lf3[1]) + _dot(tri, lf3[2])
            if slabs and (s, r0) == segments[0]:
                project(*slabs.pop(0))
            b3 = _split3(b_all)
            g3 = _split3(ig)
            brow = _dot_nt(sel, b3[0]) + _dot_nt(sel, b3[1]) + _dot_nt(sel, b3[2])
            igrow = _dot_nt(sel, g3[0]) + _dot_nt(sel, g3[1]) + _dot_nt(sel, g3[2])
            b_last_all = b_all[lc - 1:lc]
            g_all = b_last_all - b_all + ig
            m_new = jnp.maximum(b_last_all + m_prev[s], jnp.max(g_all, axis=0, keepdims=True))
            seg_gates[(s, r0)] = dict(b=b_all, inter=b_all + m_prev[s], brow=brow, igrow=igrow,
                                      decay=jnp.exp(b_last_all + m_prev[s] - m_new),
                                      wgt=jnp.exp(g_all - m_new))
            m_prev[s] = m_new
    for s in range(ns):
        m_sc[s] = m_prev[s]
    while slabs:
        project(*slabs.pop(0))

    def unit(seg, hd):
        s, r0 = seg
        rows = slice(r0, r0 + lc)
        ln = slice(hd * dh, (hd + 1) * dh)
        gts = seg_gates[seg]
        b_c = gts["b"][:, hd:hd + 1]
        b_r = gts["brow"][hd:hd + 1, :]
        ig_r = gts["igrow"][hd:hd + 1, :]
        inter = gts["inter"][:, hd:hd + 1]
        dmat = jnp.where(causal, b_c - b_r + ig_r, -jnp.inf)
        m_t = jnp.maximum(inter, jnp.max(dmat, axis=-1, keepdims=True))
        a = jnp.exp(inter - m_t)
        q = q_sc[rows, ln]
        k = k_sc[rows, ln]
        v = v_sc[rows, ln]
        sc = _dot_nt(q, k) * jnp.exp(dmat - m_t)
        c_old = c_sc[s, hd]
        n_old = n_sc[s, hd]
        num = a * _dot_nt(q, c_old.astype(BF16)) + _dot(sc.astype(BF16), v)
        qn = jnp.sum(q.astype(F32) * n_old, axis=-1, keepdims=True)
        den = a * qn + jnp.sum(sc, axis=-1, keepdims=True)
        hv = num / jnp.maximum(jnp.abs(den), jnp.exp(-m_t))
        mu = jnp.mean(hv, axis=-1, keepdims=True)
        dlt = hv - mu
        var = jnp.mean(dlt * dlt, axis=-1, keepdims=True)
        hn = dlt * lax.rsqrt(var + LN_EPS) * gain_ref[:, ln]
        mixed_sc[rows, ln] = (hn * oz_sc[rows, ln]).astype(BF16)
        decay = gts["decay"][:, hd:hd + 1]
        wgt = gts["wgt"][:, hd:hd + 1]
        vw = (v.astype(F32) * wgt).astype(BF16)
        c_sc[s, hd] = decay * c_old + _dot_tn(vw, k)
        n_sc[s, hd] = decay * n_old + jnp.sum(wgt * k.astype(F32), axis=0, keepdims=True)

    def out_project(g):
        res = x_rows(g) + _rmsnorm(_dot(mixed_sc[group_rows(g), :], wout_ref[...]), gpost_ref[...])
        if ns == 1:
            y_ref[0, group_rows(g), :] = res
        else:
            for s in range(ns):
                y_ref[s] = res[s * tt:(s + 1) * tt]

    n_parts = 4
    for g, segments in enumerate(groups):
        if g > 0:
            out_project(g - 1)
        for i, (seg, hd) in enumerate((seg, hd) for seg in segments for hd in range(nh)):
            if g + 1 < n_groups and i < n_parts:
                project(g + 1, i)
            unit(seg, hd)
    out_project(n_groups - 1)
    cout_ref[...] = c_sc[...]
    nout_ref[...] = n_sc[...]
    mout_ref[...] = m_sc[...]


def _odd_layer(x, state, gpre, gpost, w_main, w_g, b_g, gain, w_out, *, tt, lc, ns):
    b, t_len, d = x.shape
    nt = t_len // tt
    assert t_len % tt == 0 and tt % lc == 0 and b % ns == 0
    nh, dh = ML_HEADS, ML_HEAD_DIM
    has_state = state is not None

    def whole(shape):
        return pl.BlockSpec(shape, lambda i, j: (0,) * len(shape))

    def per_seq(shape):
        return pl.BlockSpec((ns,) + shape, lambda i, j: (i,) + (0,) * len(shape))

    st_specs = [per_seq((nh, dh, dh)), per_seq((nh, 1, dh)), per_seq((1, LANES))]
    in_specs = [pl.BlockSpec((ns, tt, d), lambda i, j: (i, j, 0))]
    args = [x]
    if has_state:
        in_specs += st_specs
        args += list(state)
    in_specs += [whole((1, d)), whole((1, d)), whole((5 * ML_WIDTH, d)),
                 whole((2 * LANES, d)), whole((1, 2 * LANES)), whole((1, ML_WIDTH)), whole((ML_WIDTH, d))]
    args += [gpre, gpost, w_main, w_g, b_g, gain, w_out]
    out_shape = (jax.ShapeDtypeStruct((b, t_len, d), F32),
                 jax.ShapeDtypeStruct((b, nh, dh, dh), F32),
                 jax.ShapeDtypeStruct((b, nh, 1, dh), F32),
                 jax.ShapeDtypeStruct((b, 1, LANES), F32))
    out_specs = (pl.BlockSpec((ns, tt, d), lambda i, j: (i, j, 0)), *st_specs)
    rows = ns * tt
    scratch = [pltpu.VMEM((rows, ML_WIDTH), BF16), pltpu.VMEM((rows, ML_WIDTH), BF16),
               pltpu.VMEM((rows, ML_WIDTH), BF16), pltpu.VMEM((rows, ML_WIDTH), F32),
               pltpu.VMEM((rows, ML_WIDTH), BF16),
               pltpu.VMEM((ns, nh, dh, dh), F32), pltpu.VMEM((ns, nh, 1, dh), F32),
               pltpu.VMEM((ns, 1, LANES), F32)]
    return pl.pallas_call(
        functools.partial(_odd_kernel, tt=tt, lc=lc, ns=ns, has_state=has_state),
        grid=(b // ns, nt), in_specs=in_specs, out_specs=out_specs, out_shape=out_shape,
        scratch_shapes=scratch,
        compiler_params=pltpu.CompilerParams(dimension_semantics=("parallel", "arbitrary"),
                                             vmem_limit_bytes=V7X_VMEM_LIMIT_BYTES),
        name="odd_layer_state" if has_state else "odd_layer",
    )(*args)


PROMPT_TILE = 512
ATT_BLOCK_CHUNKS = 2
ML_PROMPT_CHUNK = 256
SAMPLE_SEQS_PER_STEP = 8


def kernel(x_prompt, x_sample, cache_pool, cache_k, cache_v, state_C, state_n, state_m, norm_pre, norm_post,
           w_in_even, w_pool_mix, pool_scale, rel_bias, w_out_even, w_in_odd, b_gate_odd, mlstm_norm, w_out_odd):
    depth = norm_pre.shape[0]
    bp, dec_seq = x_prompt.shape[0], x_sample.shape[1]
    bs = x_sample.shape[0]
    nh = ML_HEADS
    xp, xs = x_prompt, x_sample
    outs = {name: [] for name in ("pool_p", "k_p", "v_p", "C_p", "n_p", "m_p",
                                  "pool_s", "k_s", "v_s", "C_s", "n_s", "m_s")}
    for layer in range(depth):
        gpre = norm_pre[layer][None, :]
        gpost = norm_post[layer][None, :]
        if layer % 2 == 0:
            e = layer // 2
            w_in = _cast_range_bf16(w_in_even[e], 0, EVEN_IN, axis=1, block=CAST_BLOCK)
            w_mix = w_pool_mix[e].astype(BF16)
            w_out = w_out_even[e].astype(BF16)
            scale = pool_scale[e][None, :]
            bias_p = _expand_bias(rel_bias[e], CHUNK, ATT_BLOCK_CHUNKS)
            assert dec_seq <= CHUNK
            rb_p = ATT_BLOCK_CHUNKS * CHUNK
            bias_s = jnp.concatenate([bias_p[:, 0:dec_seq, 0:BAND + dec_seq],
                                      bias_p[:, rb_p:rb_p + dec_seq, 0:BAND + dec_seq]], axis=1)
            wt_next = jnp.swapaxes(w_in_odd[(layer + 1) // 2], 0, 1) if layer + 1 < depth else None
            res = _even_layer(xp, None, gpre, gpost, w_in, w_mix, scale, bias_p, w_out,
                              tt=PROMPT_TILE, rb=ATT_BLOCK_CHUNKS * CHUNK, ns=1, pos0=0,
                              cast_along=None if wt_next is None else (wt_next, 5 * ML_WIDTH))
            xp, pp, kp, vp = res[0:4]
            w_main_next = res[4] if wt_next is not None else None
            hist = (jnp.pad(cache_pool[e], ((0, 0), (POOL_HIST_ROWS - POOL_HIST, 0), (0, 0))),
                    cache_k[e].reshape(bs, BAND, ATT_WIDTH).astype(BF16),
                    cache_v[e].reshape(bs, BAND, ATT_WIDTH).astype(BF16))
            xs, ps, ks, vs = _even_layer(xs, hist, gpre, gpost, w_in, w_mix, scale, bias_s, w_out,
                                         tt=dec_seq, rb=dec_seq, ns=SAMPLE_SEQS_PER_STEP, pos0=PAST_LEN)
            outs["pool_p"].append(pp[:, POOL_HIST_ROWS - POOL_HIST:])
            outs["k_p"].append(kp.reshape(bp, -1, ATT_HEADS, ATT_HEAD_DIM))
            outs["v_p"].append(vp.reshape(bp, -1, ATT_HEADS, ATT_HEAD_DIM))
            outs["pool_s"].append(ps[:, POOL_HIST_ROWS - POOL_HIST:])
            outs["k_s"].append(ks.reshape(bs, -1, ATT_HEADS, ATT_HEAD_DIM))
            outs["v_s"].append(vs.reshape(bs, -1, ATT_HEADS, ATT_HEAD_DIM))
        else:
            o = layer // 2
            wt = jnp.swapaxes(w_in_odd[o], 0, 1)
            w_main = w_main_next
            w_gates = _cast_range_bf16(wt, 5 * ML_WIDTH, 2 * nh, axis=0, block=2 * nh)
            pad_h = ((0, LANES - nh), (0, 0))
            w_g = jnp.concatenate([jnp.pad(w_gates[:nh], pad_h), jnp.pad(w_gates[nh:], pad_h)], axis=0)
            b_g = jnp.concatenate([jnp.pad(b_gate_odd[o][:nh], (0, LANES - nh)),
                                   jnp.pad(b_gate_odd[o][nh:], (0, LANES - nh))])[None, :]
            gain = mlstm_norm[o][None, :]
            w_out = w_out_odd[o].astype(BF16)
            xp, cp, np_, mp = _odd_layer(xp, None, gpre, gpost, w_main, w_g, b_g, gain, w_out,
                                         tt=PROMPT_TILE, lc=ML_PROMPT_CHUNK, ns=1)
            state = (state_C[o], state_n[o][:, :, None, :],
                     jnp.pad(state_m[o], ((0, 0), (0, LANES - nh)))[:, None, :])
            xs, cs, ns, ms = _odd_layer(xs, state, gpre, gpost, w_main, w_g, b_g, gain, w_out,
                                        tt=dec_seq, lc=dec_seq, ns=SAMPLE_SEQS_PER_STEP)
            outs["C_p"].append(cp)
            outs["n_p"].append(np_[:, :, 0, :])
            outs["m_p"].append(mp[:, 0, :nh])
            outs["C_s"].append(cs)
            outs["n_s"].append(ns[:, :, 0, :])
            outs["m_s"].append(ms[:, 0, :nh])
    return (xp, xs,
            jnp.stack(outs["pool_p"]), jnp.stack(outs["k_p"]), jnp.stack(outs["v_p"]),
            jnp.stack(outs["C_p"]), jnp.stack(outs["n_p"]), jnp.stack(outs["m_p"]),
            jnp.stack(outs["pool_s"]), jnp.stack(outs["k_s"]), jnp.stack(outs["v_s"]),
            jnp.stack(outs["C_s"]), jnp.stack(outs["n_s"]), jnp.stack(outs["m_s"]))
```

```python
import functools

import jax
import jax.numpy as jnp
from jax import lax
from jax.experimental import pallas as pl
from jax.experimental.pallas import tpu as pltpu

D_MODEL = 1024
PAST_LEN = 4096
CHUNK = 64
POOL_WINDOWS = (2, 4, 8, 16)
POOL_GROUPS = len(POOL_WINDOWS)
POOL_WIDTH = D_MODEL // 2
POOL_GROUP_DIM = POOL_WIDTH // POOL_GROUPS
POOL_HIST = max(POOL_WINDOWS) - 1
POOL_HIST_ROWS = 16
ATT_WIDTH = D_MODEL // 2
ATT_HEADS = 8
ATT_HEAD_DIM = ATT_WIDTH // ATT_HEADS
HEAD_PAIRS = ATT_HEADS // 2
BAND = 8 * CHUNK
REL_CLIP = 128
REL_TABLE = 2 * REL_CLIP + 1
REL_TABLE_PAD = 384
ML_WIDTH = D_MODEL
ML_HEADS = 4
ML_HEAD_DIM = ML_WIDTH // ML_HEADS
EVEN_IN = 2 * POOL_WIDTH + 4 * ATT_WIDTH
RMS_EPS = 1e-6
LN_EPS = 1e-6
LOG2_E = 1.4426950408889634

LANES = 128
SUBLANES = 8
BF16_ROWS_PER_VREG = 2 * SUBLANES
BIAS_ROW_UNROLL = 8
SCORE_SLOTS = 3
CAST_BLOCK = 1024
V7X_VMEM_BYTES = 64 * 1024 * 1024
V7X_VMEM_LIMIT_BYTES = V7X_VMEM_BYTES * 7 // 8

BF16 = jnp.bfloat16
F32 = jnp.float32


def _dot(a, b):
    return jnp.dot(a, b, preferred_element_type=F32)


def _dot_nt(a, b):
    return lax.dot_general(a, b, (((1,), (1,)), ((), ())), preferred_element_type=F32)


def _dot_tn(a, b):
    return lax.dot_general(a, b, (((0,), (0,)), ((), ())), preferred_element_type=F32)


def _split3(x):
    hi = x.astype(BF16)
    r1 = x - hi.astype(F32)
    mid = r1.astype(BF16)
    lo = (r1 - mid.astype(F32)).astype(BF16)
    return hi, mid, lo


def _rmsnorm(x, g):
    return x * lax.rsqrt(jnp.mean(x * x, axis=-1, keepdims=True) + RMS_EPS) * g


def _silu(x):
    return x * jax.nn.sigmoid(x)


def _cast_kernel(src_ref, dst_ref):
    dst_ref[...] = src_ref[...].astype(dst_ref.dtype)


def _cast_range_bf16(w, start, n, axis, block):
    assert n % block == 0 and start % block == 0 and start + n <= w.shape[axis]
    first = start // block
    if axis == 0:
        in_spec = pl.BlockSpec((block, w.shape[1]), lambda j: (first + j, 0))
        out_spec = pl.BlockSpec((block, w.shape[1]), lambda j: (j, 0))
        shape = (n, w.shape[1])
    else:
        in_spec = pl.BlockSpec((w.shape[0], block), lambda j: (0, first + j))
        out_spec = pl.BlockSpec((w.shape[0], block), lambda j: (0, j))
        shape = (w.shape[0], n)
    return pl.pallas_call(
        _cast_kernel, grid=(n // block,), in_specs=[in_spec], out_specs=out_spec,
        out_shape=jax.ShapeDtypeStruct(shape, BF16), name="cast_bf16",
    )(w)


def _bias_kernel(tab_ref, out_ref, *, lc, nb):
    rb = nb * lc
    kb = BAND + rb
    n_dist = rb - 1 + kb
    gw = (n_dist + LANES - 1) // LANES * LANES
    parts = _split3(tab_ref[...])
    dist = lax.broadcasted_iota(jnp.int32, (REL_TABLE_PAD, gw), 1) - (rb - 1)
    ti = lax.broadcasted_iota(jnp.int32, (REL_TABLE_PAD, gw), 0)
    onehot = jnp.where(ti == jnp.clip(BAND - dist, -REL_CLIP, REL_CLIP) + REL_CLIP, 1.0, 0.0).astype(BF16)
    by_dist = _dot(parts[0], onehot) + _dot(parts[1], onehot) + _dot(parts[2], onehot)
    key_chunk = lax.broadcasted_iota(jnp.int32, (ATT_HEADS, kb), 1) // lc

    def row(qi, carry):
        shift = lax.rem(gw - (rb - 1 - qi), gw)
        acc = pltpu.roll(by_dist, shift, axis=1)[:, 0:kb]
        q_chunk = qi // lc
        in_band = (key_chunk >= q_chunk) & (key_chunk <= q_chunk + BAND // lc)
        out_ref[qi] = jnp.where(in_band, acc * LOG2_E, -jnp.inf)
        return carry

    lax.fori_loop(0, rb, row, 0, unroll=BIAS_ROW_UNROLL)


def _expand_bias(table, lc, nb):
    rb = nb * lc
    kb = BAND + rb
    tab = jnp.pad(table.astype(F32), ((0, 0), (0, REL_TABLE_PAD - REL_TABLE)))
    out = pl.pallas_call(
        functools.partial(_bias_kernel, lc=lc, nb=nb),
        out_shape=jax.ShapeDtypeStruct((rb, ATT_HEADS, kb), F32),
        name="rel_bias_expand",
    )(tab)
    return out.transpose(1, 0, 2).reshape(HEAD_PAIRS, 2 * rb, kb)


def _even_kernel(*refs, tt, rb, ns, has_hist, cast_along, pos0):
    refs = list(refs)
    x_ref = refs.pop(0)
    if has_hist:
        hu_ref, hk_ref, hv_ref = refs[0:3]
        refs = refs[3:]
    gpre_ref, gpost_ref, win_ref, wmix_ref, scale_ref, bias_ref, wout_ref = refs[0:7]
    refs = refs[7:]
    if cast_along:
        wsrc_ref = refs.pop(0)
        y_ref, pool_ref, kout_ref, vout_ref, wdst_ref = refs[0:5]
        refs = refs[5:]
        wdst_ref[...] = wsrc_ref[...].astype(BF16)
    else:
        y_ref, pool_ref, kout_ref, vout_ref = refs[0:4]
        refs = refs[4:]
    uext, kext, vext, q_sc, gate_sc, mixed_sc, s_sc = refs
    t = pl.program_id(1)
    hr = POOL_HIST_ROWS
    kb = BAND + rb
    seqs = range(ns)

    def seq_rows(s):
        return slice(s * tt, (s + 1) * tt)

    @pl.when(t == 0)
    def _():
        for s in seqs:
            if has_hist:
                uext[s, 0:hr, :] = hu_ref[s]
                kext[s, 0:BAND, :] = hk_ref[s]
                vext[s, 0:BAND, :] = hv_ref[s]
            else:
                uext[s, 0:hr, :] = jnp.zeros((hr, POOL_WIDTH), F32)
                kext[s, 0:BAND, :] = jnp.zeros((BAND, ATT_WIDTH), BF16)
                vext[s, 0:BAND, :] = jnp.zeros((BAND, ATT_WIDTH), BF16)

    @pl.when(t > 0)
    def _():
        for s in seqs:
            uext[s, 0:hr, :] = uext[s, tt:tt + hr, :]
            kext[s, 0:BAND, :] = kext[s, tt:tt + BAND, :]
            vext[s, 0:BAND, :] = vext[s, tt:tt + BAND, :]

    def load_x():
        return x_ref[0] if ns == 1 else jnp.concatenate([x_ref[s] for s in seqs], axis=0)

    h = _rmsnorm(load_x(), gpre_ref[...]).astype(BF16)
    p_, a_ = POOL_WIDTH, ATT_WIDTH

    def project(part):
        cols = slice(part * a_, (part + 1) * a_)
        res = _dot(h, win_ref[:, cols])
        if part == 0:
            for s in seqs:
                uext[s, hr:hr + tt, :] = res[seq_rows(s)]
        elif part == 1:
            q_sc[...] = (res * (ATT_HEAD_DIM ** -0.5 * LOG2_E)).astype(BF16)
        elif part == 2:
            for s in seqs:
                kext[s, BAND:BAND + tt, :] = res[seq_rows(s)].astype(BF16)
                kout_ref[s] = res[seq_rows(s)]
        elif part == 3:
            for s in seqs:
                vext[s, BAND:BAND + tt, :] = res[seq_rows(s)].astype(BF16)
                vout_ref[s] = res[seq_rows(s)]
        else:
            gate_sc[:, (part - 4) * a_:(part - 3) * a_] = _silu(res)

    pos = pos0 + t * tt + lax.broadcasted_iota(jnp.int32, (tt, 1), 0)

    def pool_group(g):
        w = POOL_WINDOWS[g]
        ln = slice(g * POOL_GROUP_DIM, (g + 1) * POOL_GROUP_DIM)
        count = jnp.minimum(pos + 1, w).astype(F32)
        pooled = []
        for s in seqs:
            tok = uext[s, hr:hr + tt, ln]
            win_sum = tok
            for back in range(1, w):
                win_sum = win_sum + uext[s, hr - back:hr - back + tt, ln]
            pooled.append(win_sum / count - tok)
        pooled = pooled[0] if ns == 1 else jnp.concatenate(pooled, axis=0)
        mixed = _dot(pooled.astype(BF16), wmix_ref[g]) * scale_ref[:, ln]
        mixed_sc[:, ln] = (mixed * gate_sc[:, ln]).astype(BF16)

    assert p_ == a_ and POOL_GROUPS == 4
    project(0)
    project(4)
    for g, part in enumerate((1, 2, 3, 5)):
        project(part)
        pool_group(g)

    lane = lax.broadcasted_iota(jnp.int32, (rb, LANES), 1)
    even_head = lane < ATT_HEAD_DIM

    def attend(sequence_start):
        units = [(s, blk, p) for s in seqs for blk in range(tt // rb) for p in range(HEAD_PAIRS)]

        def band(blk):
            r0 = blk * rb
            k0 = BAND if sequence_start else r0
            c0 = BAND - r0 if sequence_start else 0
            return r0, k0, c0, kb - c0

        def stage_scores(i):
            s, blk, p = units[i]
            r0, k0, c0, kw = band(blk)
            ln = slice(p * LANES, (p + 1) * LANES)
            qp = q_sc[s * tt + r0:s * tt + r0 + rb, ln]
            zero = jnp.zeros_like(qp)
            q2 = jnp.concatenate([jnp.where(even_head, qp, zero), jnp.where(even_head, zero, qp)], axis=0)
            s_sc[i % SCORE_SLOTS, :, 0:kw] = _dot_nt(q2, kext[s, k0:k0 + kw, ln]) + bias_ref[p, :, c0:c0 + kw]

        def finish(i):
            s, blk, p = units[i]
            r0, k0, c0, kw = band(blk)
            ln = slice(p * LANES, (p + 1) * LANES)
            row_max = jnp.max(s_sc[i % SCORE_SLOTS, :, 0:kw], axis=-1, keepdims=True)
            e = jnp.exp2(s_sc[i % SCORE_SLOTS, :, 0:kw] - row_max)
            denom = jnp.sum(e, axis=-1, keepdims=True)
            o2 = _dot(e.astype(BF16), vext[s, k0:k0 + kw, ln]) / denom
            o = jnp.where(even_head, o2[0:rb], o2[rb:2 * rb])
            rows = slice(s * tt + r0, s * tt + r0 + rb)
            mo = slice(POOL_WIDTH + p * LANES, POOL_WIDTH + (p + 1) * LANES)
            mixed_sc[rows, mo] = (o * gate_sc[rows, mo]).astype(BF16)

        ahead = SCORE_SLOTS - 1
        for i in range(min(ahead, len(units))):
            stage_scores(i)
        for i in range(len(units)):
            if i + ahead < len(units):
                stage_scores(i + ahead)
            finish(i)

    if has_hist:
        attend(False)
    else:
        pl.when(t == 0)(functools.partial(attend, True))
        pl.when(t > 0)(functools.partial(attend, False))

    y = load_x() + _rmsnorm(_dot(mixed_sc[...], wout_ref[...]), gpost_ref[...])
    for s in seqs:
        y_ref[s] = y[seq_rows(s)]
        pool_ref[s] = uext[s, tt:tt + hr, :]


def _even_layer(x, hist, gpre, gpost, w_in, w_mix, scale, bias, w_out, *, tt, rb, ns, pos0, cast_along=None):
    b, t_len, d = x.shape
    has_hist = hist is not None
    nt = t_len // tt
    assert t_len % tt == 0 and tt % rb == 0 and b % ns == 0
    assert has_hist or rb % LANES == 0, "band start at a sequence start must stay lane aligned in the bias"
    keep = min(BAND, t_len)
    assert keep == tt, "key/value cache rows must be exactly the last time tile"
    assert nt == 1 or tt >= BAND
    kb = BAND + rb

    def whole(shape):
        return pl.BlockSpec(shape, lambda i, j: (0,) * len(shape))

    def per_seq(shape):
        return pl.BlockSpec((ns,) + shape, lambda i, j: (i, 0, 0))

    in_specs = [pl.BlockSpec((ns, tt, d), lambda i, j: (i, j, 0))]
    args = [x]
    if has_hist:
        in_specs += [per_seq((POOL_HIST_ROWS, POOL_WIDTH)), per_seq((BAND, ATT_WIDTH)), per_seq((BAND, ATT_WIDTH))]
        args += list(hist)
    in_specs += [whole((1, d)), whole((1, d)), whole((d, EVEN_IN)),
                 whole((POOL_GROUPS, POOL_GROUP_DIM, POOL_GROUP_DIM)), whole((1, POOL_WIDTH)),
                 whole((HEAD_PAIRS, 2 * rb, kb)), whole((POOL_WIDTH + ATT_WIDTH, d))]
    args += [gpre, gpost, w_in, w_mix, scale, bias, w_out]
    out_shape = [jax.ShapeDtypeStruct((b, t_len, d), F32),
                 jax.ShapeDtypeStruct((b, POOL_HIST_ROWS, POOL_WIDTH), F32),
                 jax.ShapeDtypeStruct((b, keep, ATT_WIDTH), F32),
                 jax.ShapeDtypeStruct((b, keep, ATT_WIDTH), F32)]
    out_specs = [pl.BlockSpec((ns, tt, d), lambda i, j: (i, j, 0)),
                 per_seq((POOL_HIST_ROWS, POOL_WIDTH)), per_seq((keep, ATT_WIDTH)), per_seq((keep, ATT_WIDTH))]
    if cast_along is not None:
        src, n_rows = cast_along
        steps = (b // ns) * nt
        share = n_rows // steps
        assert n_rows % steps == 0 and share % BF16_ROWS_PER_VREG == 0 and n_rows <= src.shape[0]
        share_spec = pl.BlockSpec((share, src.shape[1]), lambda i, j: (i * nt + j, 0))
        in_specs.append(share_spec)
        args.append(src)
        out_specs.append(share_spec)
        out_shape.append(jax.ShapeDtypeStruct((n_rows, src.shape[1]), BF16))
    scratch = [pltpu.VMEM((ns, POOL_HIST_ROWS + tt, POOL_WIDTH), F32),
               pltpu.VMEM((ns, BAND + tt, ATT_WIDTH), BF16),
               pltpu.VMEM((ns, BAND + tt, ATT_WIDTH), BF16),
               pltpu.VMEM((ns * tt, ATT_WIDTH), BF16),
               pltpu.VMEM((ns * tt, POOL_WIDTH + ATT_WIDTH), F32),
               pltpu.VMEM((ns * tt, POOL_WIDTH + ATT_WIDTH), BF16),
               pltpu.VMEM((SCORE_SLOTS, 2 * rb, kb), F32)]
    return pl.pallas_call(
        functools.partial(_even_kernel, tt=tt, rb=rb, ns=ns, has_hist=has_hist,
                          cast_along=cast_along is not None, pos0=pos0),
        grid=(b // ns, nt), in_specs=in_specs, out_specs=tuple(out_specs), out_shape=tuple(out_shape),
        scratch_shapes=scratch,
        compiler_params=pltpu.CompilerParams(dimension_semantics=("parallel", "arbitrary"),
                                             vmem_limit_bytes=V7X_VMEM_LIMIT_BYTES),
        name="even_layer_hist" if has_hist else "even_layer",
    )(*args)


def _odd_kernel(*refs, tt, lc, ns, has_state):
    if has_state:
        x_ref, c0_ref, n0_ref, m0_ref = refs[0:4]
        refs = refs[4:]
    else:
        x_ref = refs[0]
        refs = refs[1:]
    (gpre_ref, gpost_ref, w_ref, wg_ref, bg_ref, gain_ref, wout_ref,
     y_ref, cout_ref, nout_ref, mout_ref, q_sc, k_sc, v_sc, oz_sc, mixed_sc, c_sc, n_sc, m_sc) = refs
    t = pl.program_id(1)
    w_, dh, nh = ML_WIDTH, ML_HEAD_DIM, ML_HEADS
    assert nh <= SUBLANES, "the per-head gate lanes are moved to sublanes eight at a time"
    if ns == 1:
        groups = [[(0, c * lc)] for c in range(tt // lc)]
    else:
        assert tt == lc
        groups = [[(s, s * tt) for s in range(ns)]]
    n_groups = len(groups)
    g_rows = lc * len(groups[0])

    def group_rows(g):
        return slice(g * g_rows, (g + 1) * g_rows)

    @pl.when(t == 0)
    def _():
        if has_state:
            c_sc[...] = c0_ref[...]
            n_sc[...] = n0_ref[...]
            m_sc[...] = m0_ref[...]
        else:
            c_sc[...] = jnp.zeros(c_sc.shape, F32)
            n_sc[...] = jnp.zeros(n_sc.shape, F32)
            m_sc[...] = jnp.zeros(m_sc.shape, F32)

    def x_rows(g):
        if ns == 1:
            return x_ref[0, group_rows(g), :]
        return jnp.concatenate([x_ref[s] for s in range(ns)], axis=0)

    h = [_rmsnorm(x_rows(g), gpre_ref[...]).astype(BF16) for g in range(n_groups)]

    def project(g, part):
        rows = group_rows(g)
        if part == 0:
            q_sc[rows, :] = _dot_nt(h[g], w_ref[0:w_, :]).astype(BF16)
        elif part == 1:
            k_sc[rows, :] = (_dot_nt(h[g], w_ref[w_:2 * w_, :]) * (dh ** -0.5)).astype(BF16)
        elif part == 2:
            v_sc[rows, :] = _dot_nt(h[g], w_ref[2 * w_:3 * w_, :]).astype(BF16)
        else:
            oz_sc[rows, :] = (jax.nn.sigmoid(_dot_nt(h[g], w_ref[3 * w_:4 * w_, :]))
                              * _silu(_dot_nt(h[g], w_ref[4 * w_:5 * w_, :])))

    tri = jnp.where(lax.broadcasted_iota(jnp.int32, (lc, lc), 1) <= lax.broadcasted_iota(jnp.int32, (lc, lc), 0),
                    1.0, 0.0).astype(BF16)
    sel = jnp.where(lax.broadcasted_iota(jnp.int32, (SUBLANES, LANES), 0)
                    == lax.broadcasted_iota(jnp.int32, (SUBLANES, LANES), 1), 1.0, 0.0).astype(BF16)
    causal = (lax.broadcasted_iota(jnp.int32, (lc, lc), 1) <= lax.broadcasted_iota(jnp.int32, (lc, lc), 0))

    slabs = [(0, part) for part in range(4)]
    m_prev = [m_sc[s] for s in range(ns)]
    seg_gates = {}
    for g, segments in enumerate(groups):
        gates = _dot_nt(h[g], wg_ref[...]) + bg_ref[...]
        if slabs:
            project(*slabs.pop(0))
        for s, r0 in segments:
            lo = r0 - g * g_rows
            ig = gates[lo:lo + lc, 0:LANES]
            lf3 = _split3(jax.nn.log_sigmoid(gates[lo:lo + lc, LANES:2 * LANES]))
            b_all = _dot(tri, lf3[0]) + _dot(tri, lf3[1]) + _dot(tri, lf3[2])
            if slabs and (s, r0) == segments[0]:
                project(*slabs.pop(0))
            b3 = _split3(b_all)
            g3 = _split3(ig)
            brow = _dot_nt(sel, b3[0]) + _dot_nt(sel, b3[1]) + _dot_nt(sel, b3[2])
            igrow = _dot_nt(sel, g3[0]) + _dot_nt(sel, g3[1]) + _dot_nt(sel, g3[2])
            b_last_all = b_all[lc - 1:lc]
            g_all = b_last_all - b_all + ig
            m_new = jnp.maximum(b_last_all + m_prev[s], jnp.max(g_all, axis=0, keepdims=True))
            seg_gates[(s, r0)] = dict(b=b_all, inter=b_all + m_prev[s], brow=brow, igrow=igrow,
                                      decay=jnp.exp(b_last_all + m_prev[s] - m_new),
                                      wgt=jnp.exp(g_all - m_new))
            m_prev[s] = m_new
    for s in range(ns):
        m_sc[s] = m_prev[s]
    while slabs:
        project(*slabs.pop(0))

    def unit(seg, hd):
        s, r0 = seg
        rows = slice(r0, r0 + lc)
        ln = slice(hd * dh, (hd + 1) * dh)
        gts = seg_gates[seg]
        b_c = gts["b"][:, hd:hd + 1]
        b_r = gts["brow"][hd:hd + 1, :]
        ig_r = gts["igrow"][hd:hd + 1, :]
        inter = gts["inter"][:, hd:hd + 1]
        dmat = jnp.where(causal, b_c - b_r + ig_r, -jnp.inf)
        m_t = jnp.maximum(inter, jnp.max(dmat, axis=-1, keepdims=True))
        a = jnp.exp(inter - m_t)
        q = q_sc[rows, ln]
        k = k_sc[rows, ln]
        v = v_sc[rows, ln]
        sc = _dot_nt(q, k) * jnp.exp(dmat - m_t)
        c_old = c_sc[s, hd]
        n_old = n_sc[s, hd]
        num = a * _dot_nt(q, c_old.astype(BF16)) + _dot(sc.astype(BF16), v)
        qn = jnp.sum(q.astype(F32) * n_old, axis=-1, keepdims=True)
        den = a * qn + jnp.sum(sc, axis=-1, keepdims=True)
        hv = num / jnp.maximum(jnp.abs(den), jnp.exp(-m_t))
        mu = jnp.mean(hv, axis=-1, keepdims=True)
        dlt = hv - mu
        var = jnp.mean(dlt * dlt, axis=-1, keepdims=True)
        hn = dlt * lax.rsqrt(var + LN_EPS) * gain_ref[:, ln]
        mixed_sc[rows, ln] = (hn * oz_sc[rows, ln]).astype(BF16)
        decay = gts["decay"][:, hd:hd + 1]
        wgt = gts["wgt"][:, hd:hd + 1]
        vw = (v.astype(F32) * wgt).astype(BF16)
        c_sc[s, hd] = decay * c_old + _dot_tn(vw, k)
        n_sc[s, hd] = decay * n_old + jnp.sum(wgt * k.astype(F32), axis=0, keepdims=True)

    def out_project(g):
        res = x_rows(g) + _rmsnorm(_dot(mixed_sc[group_rows(g), :], wout_ref[...]), gpost_ref[...])
        if ns == 1:
            y_ref[0, group_rows(g), :] = res
        else:
            for s in range(ns):
                y_ref[s] = res[s * tt:(s + 1) * tt]

    n_parts = 4
    for g, segments in enumerate(groups):
        if g > 0:
            out_project(g - 1)
        for i, (seg, hd) in enumerate((seg, hd) for seg in segments for hd in range(nh)):
            if g + 1 < n_groups and i < n_parts:
                project(g + 1, i)
            unit(seg, hd)
    out_project(n_groups - 1)
    cout_ref[...] = c_sc[...]
    nout_ref[...] = n_sc[...]
    mout_ref[...] = m_sc[...]


def _odd_layer(x, state, gpre, gpost, w_main, w_g, b_g, gain, w_out, *, tt, lc, ns):
    b, t_len, d = x.shape
    nt = t_len // tt
    assert t_len % tt == 0 and tt % lc == 0 and b % ns == 0
    nh, dh = ML_HEADS, ML_HEAD_DIM
    has_state = state is not None

    def whole(shape):
        return pl.BlockSpec(shape, lambda i, j: (0,) * len(shape))

    def per_seq(shape):
        return pl.BlockSpec((ns,) + shape, lambda i, j: (i,) + (0,) * len(shape))

    st_specs = [per_seq((nh, dh, dh)), per_seq((nh, 1, dh)), per_seq((1, LANES))]
    in_specs = [pl.BlockSpec((ns, tt, d), lambda i, j: (i, j, 0))]
    args = [x]
    if has_state:
        in_specs += st_specs
        args += list(state)
    in_specs += [whole((1, d)), whole((1, d)), whole((5 * ML_WIDTH, d)),
                 whole((2 * LANES, d)), whole((1, 2 * LANES)), whole((1, ML_WIDTH)), whole((ML_WIDTH, d))]
    args += [gpre, gpost, w_main, w_g, b_g, gain, w_out]
    out_shape = (jax.ShapeDtypeStruct((b, t_len, d), F32),
                 jax.ShapeDtypeStruct((b, nh, dh, dh), F32),
                 jax.ShapeDtypeStruct((b, nh, 1, dh), F32),
                 jax.ShapeDtypeStruct((b, 1, LANES), F32))
    out_specs = (pl.BlockSpec((ns, tt, d), lambda i, j: (i, j, 0)), *st_specs)
    rows = ns * tt
    scratch = [pltpu.VMEM((rows, ML_WIDTH), BF16), pltpu.VMEM((rows, ML_WIDTH), BF16),
               pltpu.VMEM((rows, ML_WIDTH), BF16), pltpu.VMEM((rows, ML_WIDTH), F32),
               pltpu.VMEM((rows, ML_WIDTH), BF16),
               pltpu.VMEM((ns, nh, dh, dh), F32), pltpu.VMEM((ns, nh, 1, dh), F32),
               pltpu.VMEM((ns, 1, LANES), F32)]
    return pl.pallas_call(
        functools.partial(_odd_kernel, tt=tt, lc=lc, ns=ns, has_state=has_state),
        grid=(b // ns, nt), in_specs=in_specs, out_specs=out_specs, out_shape=out_shape,
        scratch_shapes=scratch,
        compiler_params=pltpu.CompilerParams(dimension_semantics=("parallel", "arbitrary"),
                                             vmem_limit_bytes=V7X_VMEM_LIMIT_BYTES),
        name="odd_layer_state" if has_state else "odd_layer",
    )(*args)


PROMPT_TILE = 512
ATT_BLOCK_CHUNKS = 2
ML_PROMPT_CHUNK = 256
SAMPLE_SEQS_PER_STEP = 8


def kernel(x_prompt, x_sample, cache_pool, cache_k, cache_v, state_C, state_n, state_m, norm_pre, norm_post,
           w_in_even, w_pool_mix, pool_scale, rel_bias, w_out_even, w_in_odd, b_gate_odd, mlstm_norm, w_out_odd):
    depth = norm_pre.shape[0]
    bp, dec_seq = x_prompt.shape[0], x_sample.shape[1]
    bs = x_sample.shape[0]
    nh = ML_HEADS
    xp, xs = x_prompt, x_sample
    outs = {name: [] for name in ("pool_p", "k_p", "v_p", "C_p", "n_p", "m_p",
                                  "pool_s", "k_s", "v_s", "C_s", "n_s", "m_s")}
    for layer in range(depth):
        gpre = norm_pre[layer][None, :]
        gpost = norm_post[layer][None, :]
        if layer % 2 == 0:
            e = layer // 2
            w_in = _cast_range_bf16(w_in_even[e], 0, EVEN_IN, axis=1, block=CAST_BLOCK)
            w_mix = w_pool_mix[e].astype(BF16)
            w_out = w_out_even[e].astype(BF16)
            scale = pool_scale[e][None, :]
            bias_p = _expand_bias(rel_bias[e], CHUNK, ATT_BLOCK_CHUNKS)
            assert dec_seq <= CHUNK
            rb_p = ATT_BLOCK_CHUNKS * CHUNK
            bias_s = jnp.concatenate([bias_p[:, 0:dec_seq, 0:BAND + dec_seq],
                                      bias_p[:, rb_p:rb_p + dec_seq, 0:BAND + dec_seq]], axis=1)
            wt_next = jnp.swapaxes(w_in_odd[(layer + 1) // 2], 0, 1) if layer + 1 < depth else None
            res = _even_layer(xp, None, gpre, gpost, w_in, w_mix, scale, bias_p, w_out,
                              tt=PROMPT_TILE, rb=ATT_BLOCK_CHUNKS * CHUNK, ns=1, pos0=0,
                              cast_along=None if wt_next is None else (wt_next, 5 * ML_WIDTH))
            xp, pp, kp, vp = res[0:4]
            w_main_next = res[4] if wt_next is not None else None
            hist = (jnp.pad(cache_pool[e], ((0, 0), (POOL_HIST_ROWS - POOL_HIST, 0), (0, 0))),
                    cache_k[e].reshape(bs, BAND, ATT_WIDTH).astype(BF16),
                    cache_v[e].reshape(bs, BAND, ATT_WIDTH).astype(BF16))
            xs, ps, ks, vs = _even_layer(xs, hist, gpre, gpost, w_in, w_mix, scale, bias_s, w_out,
                                         tt=dec_seq, rb=dec_seq, ns=SAMPLE_SEQS_PER_STEP, pos0=PAST_LEN)
            outs["pool_p"].append(pp[:, POOL_HIST_ROWS - POOL_HIST:])
            outs["k_p"].append(kp.reshape(bp, -1, ATT_HEADS, ATT_HEAD_DIM))
            outs["v_p"].append(vp.reshape(bp, -1, ATT_HEADS, ATT_HEAD_DIM))
            outs["pool_s"].append(ps[:, POOL_HIST_ROWS - POOL_HIST:])
            outs["k_s"].append(ks.reshape(bs, -1, ATT_HEADS, ATT_HEAD_DIM))
            outs["v_s"].append(vs.reshape(bs, -1, ATT_HEADS, ATT_HEAD_DIM))
        else:
            o = layer // 2
            wt = jnp.swapaxes(w_in_odd[o], 0, 1)
            w_main = w_main_next
            w_gates = _cast_range_bf16(wt, 5 * ML_WIDTH, 2 * nh, axis=0, block=2 * nh)
            pad_h = ((0, LANES - nh), (0, 0))
            w_g = jnp.concatenate([jnp.pad(w_gates[:nh], pad_h), jnp.pad(w_gates[nh:], pad_h)], axis=0)
            b_g = jnp.concatenate([jnp.pad(b_gate_odd[o][:nh], (0, LANES - nh)),
                                   jnp.pad(b_gate_odd[o][nh:], (0, LANES - nh))])[None, :]
            gain = mlstm_norm[o][None, :]
            w_out = w_out_odd[o].astype(BF16)
            xp, cp, np_, mp = _odd_layer(xp, None, gpre, gpost, w_main, w_g, b_g, gain, w_out,
                                         tt=PROMPT_TILE, lc=ML_PROMPT_CHUNK, ns=1)
            state = (state_C[o], state_n[o][:, :, None, :],
                     jnp.pad(state_m[o], ((0, 0), (0, LANES - nh)))[:, None, :])
            xs, cs, ns, ms = _odd_layer(xs, state, gpre, gpost, w_main, w_g, b_g, gain, w_out,
                                        tt=dec_seq, lc=dec_seq, ns=SAMPLE_SEQS_PER_STEP)
            outs["C_p"].append(cp)
            outs["n_p"].append(np_[:, :, 0, :])
            outs["m_p"].append(mp[:, 0, :nh])
            outs["C_s"].append(cs)
            outs["n_s"].append(ns[:, :, 0, :])
            outs["m_s"].append(ms[:, 0, :nh])
    return (xp, xs,
            jnp.stack(outs["pool_p"]), jnp.stack(outs["k_p"]), jnp.stack(outs["v_p"]),
            jnp.stack(outs["C_p"]), jnp.stack(outs["n_p"]), jnp.stack(outs["m_p"]),
            jnp.stack(outs["pool_s"]), jnp.stack(outs["k_s"]), jnp.stack(outs["v_s"]),
            jnp.stack(outs["C_s"]), jnp.stack(outs["n_s"]), jnp.stack(outs["m_s"]))
```

```python
import functools

import jax
import jax.numpy as jnp
from jax import lax
from jax.experimental import pallas as pl
from jax.experimental.pallas import tpu as pltpu

D_MODEL = 1024
PAST_LEN = 4096
CHUNK = 64
POOL_WINDOWS = (2, 4, 8, 16)
POOL_GROUPS = len(POOL_WINDOWS)
POOL_WIDTH = D_MODEL // 2
POOL_GROUP_DIM = POOL_WIDTH // POOL_GROUPS
POOL_HIST = max(POOL_WINDOWS) - 1
POOL_HIST_ROWS = 16
ATT_WIDTH = D_MODEL // 2
ATT_HEADS = 8
ATT_HEAD_DIM = ATT_WIDTH // ATT_HEADS
HEAD_PAIRS = ATT_HEADS // 2
BAND = 8 * CHUNK
REL_CLIP = 128
REL_TABLE = 2 * REL_CLIP + 1
REL_TABLE_PAD = 384
ML_WIDTH = D_MODEL
ML_HEADS = 4
ML_HEAD_DIM = ML_WIDTH // ML_HEADS
EVEN_IN = 2 * POOL_WIDTH + 4 * ATT_WIDTH
RMS_EPS = 1e-6
LN_EPS = 1e-6
LOG2_E = 1.4426950408889634

LANES = 128
SUBLANES = 8
BF16_ROWS_PER_VREG = 2 * SUBLANES
BIAS_ROW_UNROLL = 8
SCORE_SLOTS = 3
CAST_BLOCK = 1024
V7X_VMEM_BYTES = 64 * 1024 * 1024
V7X_VMEM_LIMIT_BYTES = V7X_VMEM_BYTES * 7 // 8

BF16 = jnp.bfloat16
F32 = jnp.float32


def _dot(a, b):
    return jnp.dot(a, b, preferred_element_type=F32)


def _dot_nt(a, b):
    return lax.dot_general(a, b, (((1,), (1,)), ((), ())), preferred_element_type=F32)


def _dot_tn(a, b):
    return lax.dot_general(a, b, (((0,), (0,)), ((), ())), preferred_element_type=F32)


def _split3(x):
    hi = x.astype(BF16)
    r1 = x - hi.astype(F32)
    mid = r1.astype(BF16)
    lo = (r1 - mid.astype(F32)).astype(BF16)
    return hi, mid, lo


def _rmsnorm(x, g):
    return x * lax.rsqrt(jnp.mean(x * x, axis=-1, keepdims=True) + RMS_EPS) * g


def _silu(x):
    return x * jax.nn.sigmoid(x)


def _cast_kernel(src_ref, dst_ref):
    dst_ref[...] = src_ref[...].astype(dst_ref.dtype)


def _cast_range_bf16(w, start, n, axis, block):
    assert n % block == 0 and start % block == 0 and start + n <= w.shape[axis]
    first = start // block
    if axis == 0:
        in_spec = pl.BlockSpec((block, w.shape[1]), lambda j: (first + j, 0))
        out_spec = pl.BlockSpec((block, w.shape[1]), lambda j: (j, 0))
        shape = (n, w.shape[1])
    else:
        in_spec = pl.BlockSpec((w.shape[0], block), lambda j: (0, first + j))
        out_spec = pl.BlockSpec((w.shape[0], block), lambda j: (0, j))
        shape = (w.shape[0], n)
    return pl.pallas_call(
        _cast_kernel, grid=(n // block,), in_specs=[in_spec], out_specs=out_spec,
        out_shape=jax.ShapeDtypeStruct(shape, BF16), name="cast_bf16",
    )(w)


def _bias_kernel(tab_ref, out_ref, *, lc, nb):
    rb = nb * lc
    kb = BAND + rb
    n_dist = rb - 1 + kb
    gw = (n_dist + LANES - 1) // LANES * LANES
    parts = _split3(tab_ref[...])
    dist = lax.broadcasted_iota(jnp.int32, (REL_TABLE_PAD, gw), 1) - (rb - 1)
    ti = lax.broadcasted_iota(jnp.int32, (REL_TABLE_PAD, gw), 0)
    onehot = jnp.where(ti == jnp.clip(BAND - dist, -REL_CLIP, REL_CLIP) + REL_CLIP, 1.0, 0.0).astype(BF16)
    by_dist = _dot(parts[0], onehot) + _dot(parts[1], onehot) + _dot(parts[2], onehot)
    key_chunk = lax.broadcasted_iota(jnp.int32, (ATT_HEADS, kb), 1) // lc

    def row(qi, carry):
        shift = lax.rem(gw - (rb - 1 - qi), gw)
        acc = pltpu.roll(by_dist, shift, axis=1)[:, 0:kb]
        q_chunk = qi // lc
        in_band = (key_chunk >= q_chunk) & (key_chunk <= q_chunk + BAND // lc)
        out_ref[qi] = jnp.where(in_band, acc * LOG2_E, -jnp.inf)
        return carry

    lax.fori_loop(0, rb, row, 0, unroll=BIAS_ROW_UNROLL)


def _expand_bias(table, lc, nb):
    rb = nb * lc
    kb = BAND + rb
    tab = jnp.pad(table.astype(F32), ((0, 0), (0, REL_TABLE_PAD - REL_TABLE)))
    out = pl.pallas_call(
        functools.partial(_bias_kernel, lc=lc, nb=nb),
        out_shape=jax.ShapeDtypeStruct((rb, ATT_HEADS, kb), F32),
        name="rel_bias_expand",
    )(tab)
    return out.transpose(1, 0, 2).reshape(HEAD_PAIRS, 2 * rb, kb)


def _even_kernel(*refs, tt, rb, ns, has_hist, cast_along, pos0):
    refs = list(refs)
    x_ref = refs.pop(0)
    if has_hist:
        hu_ref, hk_ref, hv_ref = refs[0:3]
        refs = refs[3:]
    gpre_ref, gpost_ref, win_ref, wmix_ref, scale_ref, bias_ref, wout_ref = refs[0:7]
    refs = refs[7:]
    if cast_along:
        wsrc_ref = refs.pop(0)
        y_ref, pool_ref, kout_ref, vout_ref, wdst_ref = refs[0:5]
        refs = refs[5:]
        wdst_ref[...] = wsrc_ref[...].astype(BF16)
    else:
        y_ref, pool_ref, kout_ref, vout_ref = refs[0:4]
        refs = refs[4:]
    uext, kext, vext, q_sc, gate_sc, mixed_sc, s_sc = refs
    t = pl.program_id(1)
    hr = POOL_HIST_ROWS
    kb = BAND + rb
    seqs = range(ns)

    def seq_rows(s):
        return slice(s * tt, (s + 1) * tt)

    @pl.when(t == 0)
    def _():
        for s in seqs:
            if has_hist:
                uext[s, 0:hr, :] = hu_ref[s]
                kext[s, 0:BAND, :] = hk_ref[s]
                vext[s, 0:BAND, :] = hv_ref[s]
            else:
                uext[s, 0:hr, :] = jnp.zeros((hr, POOL_WIDTH), F32)
                kext[s, 0:BAND, :] = jnp.zeros((BAND, ATT_WIDTH), BF16)
                vext[s, 0:BAND, :] = jnp.zeros((BAND, ATT_WIDTH), BF16)

    @pl.when(t > 0)
    def _():
        for s in seqs:
            uext[s, 0:hr, :] = uext[s, tt:tt + hr, :]
            kext[s, 0:BAND, :] = kext[s, tt:tt + BAND, :]
            vext[s, 0:BAND, :] = vext[s, tt:tt + BAND, :]

    def load_x():
        return x_ref[0] if ns == 1 else jnp.concatenate([x_ref[s] for s in seqs], axis=0)

    h = _rmsnorm(load_x(), gpre_ref[...]).astype(BF16)
    p_, a_ = POOL_WIDTH, ATT_WIDTH

    def project(part):
        cols = slice(part * a_, (part + 1) * a_)
        res = _dot(h, win_ref[:, cols])
        if part == 0:
            for s in seqs:
                uext[s, hr:hr + tt, :] = res[seq_rows(s)]
        elif part == 1:
            q_sc[...] = (res * (ATT_HEAD_DIM ** -0.5 * LOG2_E)).astype(BF16)
        elif part == 2:
            for s in seqs:
                kext[s, BAND:BAND + tt, :] = res[seq_rows(s)].astype(BF16)
                kout_ref[s] = res[seq_rows(s)]
        elif part == 3:
            for s in seqs:
                vext[s, BAND:BAND + tt, :] = res[seq_rows(s)].astype(BF16)
                vout_ref[s] = res[seq_rows(s)]
        else:
            gate_sc[:, (part - 4) * a_:(part - 3) * a_] = _silu(res)

    pos = pos0 + t * tt + lax.broadcasted_iota(jnp.int32, (tt, 1), 0)

    def pool_group(g):
        w = POOL_WINDOWS[g]
        ln = slice(g * POOL_GROUP_DIM, (g + 1) * POOL_GROUP_DIM)
        count = jnp.minimum(pos + 1, w).astype(F32)
        pooled = []
        for s in seqs:
            tok = uext[s, hr:hr + tt, ln]
            win_sum = tok
            for back in range(1, w):
                win_sum = win_sum + uext[s, hr - back:hr - back + tt, ln]
            pooled.append(win_sum / count - tok)
        pooled = pooled[0] if ns == 1 else jnp.concatenate(pooled, axis=0)
        mixed = _dot(pooled.astype(BF16), wmix_ref[g]) * scale_ref[:, ln]
        mixed_sc[:, ln] = (mixed * gate_sc[:, ln]).astype(BF16)

    assert p_ == a_ and POOL_GROUPS == 4
    project(0)
    project(4)
    for g, part in enumerate((1, 2, 3, 5)):
        project(part)
        pool_group(g)

    lane = lax.broadcasted_iota(jnp.int32, (rb, LANES), 1)
    even_head = lane < ATT_HEAD_DIM

    def attend(sequence_start):
        units = [(s, blk, p) for s in seqs for blk in range(tt // rb) for p in range(HEAD_PAIRS)]

        def band(blk):
            r0 = blk * rb
            k0 = BAND if sequence_start else r0
            c0 = BAND - r0 if sequence_start else 0
            return r0, k0, c0, kb - c0

        def stage_scores(i):
            s, blk, p = units[i]
            r0, k0, c0, kw = band(blk)
            ln = slice(p * LANES, (p + 1) * LANES)
            qp = q_sc[s * tt + r0:s * tt + r0 + rb, ln]
            zero = jnp.zeros_like(qp)
            q2 = jnp.concatenate([jnp.where(even_head, qp, zero), jnp.where(even_head, zero, qp)], axis=0)
            s_sc[i % SCORE_SLOTS, :, 0:kw] = _dot_nt(q2, kext[s, k0:k0 + kw, ln]) + bias_ref[p, :, c0:c0 + kw]

        def finish(i):
            s, blk, p = units[i]
            r0, k0, c0, kw = band(blk)
            ln = slice(p * LANES, (p + 1) * LANES)
            row_max = jnp.max(s_sc[i % SCORE_SLOTS, :, 0:kw], axis=-1, keepdims=True)
            e = jnp.exp2(s_sc[i % SCORE_SLOTS, :, 0:kw] - row_max)
            denom = jnp.sum(e, axis=-1, keepdims=True)
            o2 = _dot(e.astype(BF16), vext[s, k0:k0 + kw, ln]) / denom
            o = jnp.where(even_head, o2[0:rb], o2[rb:2 * rb])
            rows = slice(s * tt + r0, s * tt + r0 + rb)
            mo = slice(POOL_WIDTH + p * LANES, POOL_WIDTH + (p + 1) * LANES)
            mixed_sc[rows, mo] = (o * gate_sc[rows, mo]).astype(BF16)

        ahead = SCORE_SLOTS - 1
        for i in range(min(ahead, len(units))):
            stage_scores(i)
        for i in range(len(units)):
            if i + ahead < len(units):
                stage_scores(i + ahead)
            finish(i)

    if has_hist:
        attend(False)
    else:
        pl.when(t == 0)(functools.partial(attend, True))
        pl.when(t > 0)(functools.partial(attend, False))

    y = load_x() + _rmsnorm(_dot(mixed_sc[...], wout_ref[...]), gpost_ref[...])
    for s in seqs:
        y_ref[s] = y[seq_rows(s)]
        pool_ref[s] = uext[s, tt:tt + hr, :]


def _even_layer(x, hist, gpre, gpost, w_in, w_mix, scale, bias, w_out, *, tt, rb, ns, pos0, cast_along=None):
    b, t_len, d = x.shape
    has_hist = hist is not None
    nt = t_len // tt
    assert t_len % tt == 0 and tt % rb == 0 and b % ns == 0
    assert has_hist or rb % LANES == 0, "band start at a sequence start must stay lane aligned in the bias"
    keep = min(BAND, t_len)
    assert keep == tt, "key/value cache rows must be exactly the last time tile"
    assert nt == 1 or tt >= BAND
    kb = BAND + rb

    def whole(shape):
        return pl.BlockSpec(shape, lambda i, j: (0,) * len(shape))

    def per_seq(shape):
        return pl.BlockSpec((ns,) + shape, lambda i, j: (i, 0, 0))

    in_specs = [pl.BlockSpec((ns, tt, d), lambda i, j: (i, j, 0))]
    args = [x]
    if has_hist:
        in_specs += [per_seq((POOL_HIST_ROWS, POOL_WIDTH)), per_seq((BAND, ATT_WIDTH)), per_seq((BAND, ATT_WIDTH))]
        args += list(hist)
    in_specs += [whole((1, d)), whole((1, d)), whole((d, EVEN_IN)),
                 whole((POOL_GROUPS, POOL_GROUP_DIM, POOL_GROUP_DIM)), whole((1, POOL_WIDTH)),
                 whole((HEAD_PAIRS, 2 * rb, kb)), whole((POOL_WIDTH + ATT_WIDTH, d))]
    args += [gpre, gpost, w_in, w_mix, scale, bias, w_out]
    out_shape = [jax.ShapeDtypeStruct((b, t_len, d), F32),
                 jax.ShapeDtypeStruct((b, POOL_HIST_ROWS, POOL_WIDTH), F32),
                 jax.ShapeDtypeStruct((b, keep, ATT_WIDTH), F32),
                 jax.ShapeDtypeStruct((b, keep, ATT_WIDTH), F32)]
    out_specs = [pl.BlockSpec((ns, tt, d), lambda i, j: (i, j, 0)),
                 per_seq((POOL_HIST_ROWS, POOL_WIDTH)), per_seq((keep, ATT_WIDTH)), per_seq((keep, ATT_WIDTH))]
    if cast_along is not None:
        src, n_rows = cast_along
        steps = (b // ns) * nt
        share = n_rows // steps
        assert n_rows % steps == 0 and share % BF16_ROWS_PER_VREG == 0 and n_rows <= src.shape[0]
        share_spec = pl.BlockSpec((share, src.shape[1]), lambda i, j: (i * nt + j, 0))
        in_specs.append(share_spec)
        args.append(src)
        out_specs.append(share_spec)
        out_shape.append(jax.ShapeDtypeStruct((n_rows, src.shape[1]), BF16))
    scratch = [pltpu.VMEM((ns, POOL_HIST_ROWS + tt, POOL_WIDTH), F32),
               pltpu.VMEM((ns, BAND + tt, ATT_WIDTH), BF16),
               pltpu.VMEM((ns, BAND + tt, ATT_WIDTH), BF16),
               pltpu.VMEM((ns * tt, ATT_WIDTH), BF16),
               pltpu.VMEM((ns * tt, POOL_WIDTH + ATT_WIDTH), F32),
               pltpu.VMEM((ns * tt, POOL_WIDTH + ATT_WIDTH), BF16),
               pltpu.VMEM((SCORE_SLOTS, 2 * rb, kb), F32)]
    return pl.pallas_call(
        functools.partial(_even_kernel, tt=tt, rb=rb, ns=ns, has_hist=has_hist,
                          cast_along=cast_along is not None, pos0=pos0),
        grid=(b // ns, nt), in_specs=in_specs, out_specs=tuple(out_specs), out_shape=tuple(out_shape),
        scratch_shapes=scratch,
        compiler_params=pltpu.CompilerParams(dimension_semantics=("parallel", "arbitrary"),
                                             vmem_limit_bytes=V7X_VMEM_LIMIT_BYTES),
        name="even_layer_hist" if has_hist else "even_layer",
    )(*args)


def _odd_kernel(*refs, tt, lc, ns, has_state):
    if has_state:
        x_ref, c0_ref, n0_ref, m0_ref = refs[0:4]
        refs = refs[4:]
    else:
        x_ref = refs[0]
        refs = refs[1:]
    (gpre_ref, gpost_ref, w_ref, wg_ref, bg_ref, gain_ref, wout_ref,
     y_ref, cout_ref, nout_ref, mout_ref, q_sc, k_sc, v_sc, oz_sc, mixed_sc, c_sc, n_sc, m_sc) = refs
    t = pl.program_id(1)
    w_, dh, nh = ML_WIDTH, ML_HEAD_DIM, ML_HEADS
    assert nh <= SUBLANES, "the per-head gate lanes are moved to sublanes eight at a time"
    if ns == 1:
        groups = [[(0, c * lc)] for c in range(tt // lc)]
    else:
        assert tt == lc
        groups = [[(s, s * tt) for s in range(ns)]]
    n_groups = len(groups)
    g_rows = lc * len(groups[0])

    def group_rows(g):
        return slice(g * g_rows, (g + 1) * g_rows)

    @pl.when(t == 0)
    def _():
        if has_state:
            c_sc[...] = c0_ref[...]
            n_sc[...] = n0_ref[...]
            m_sc[...] = m0_ref[...]
        else:
            c_sc[...] = jnp.zeros(c_sc.shape, F32)
            n_sc[...] = jnp.zeros(n_sc.shape, F32)
            m_sc[...] = jnp.zeros(m_sc.shape, F32)

    def x_rows(g):
        if ns == 1:
            return x_ref[0, group_rows(g), :]
        return jnp.concatenate([x_ref[s] for s in range(ns)], axis=0)

    h = [_rmsnorm(x_rows(g), gpre_ref[...]).astype(BF16) for g in range(n_groups)]

    def project(g, part):
        rows = group_rows(g)
        if part == 0:
            q_sc[rows, :] = _dot_nt(h[g], w_ref[0:w_, :]).astype(BF16)
        elif part == 1:
            k_sc[rows, :] = (_dot_nt(h[g], w_ref[w_:2 * w_, :]) * (dh ** -0.5)).astype(BF16)
        elif part == 2:
            v_sc[rows, :] = _dot_nt(h[g], w_ref[2 * w_:3 * w_, :]).astype(BF16)
        else:
            oz_sc[rows, :] = (jax.nn.sigmoid(_dot_nt(h[g], w_ref[3 * w_:4 * w_, :]))
                              * _silu(_dot_nt(h[g], w_ref[4 * w_:5 * w_, :])))

    tri = jnp.where(lax.broadcasted_iota(jnp.int32, (lc, lc), 1) <= lax.broadcasted_iota(jnp.int32, (lc, lc), 0),
                    1.0, 0.0).astype(BF16)
    sel = jnp.where(lax.broadcasted_iota(jnp.int32, (SUBLANES, LANES), 0)
                    == lax.broadcasted_iota(jnp.int32, (SUBLANES, LANES), 1), 1.0, 0.0).astype(BF16)
    causal = (lax.broadcasted_iota(jnp.int32, (lc, lc), 1) <= lax.broadcasted_iota(jnp.int32, (lc, lc), 0))

    slabs = [(0, part) for part in range(4)]
    m_prev = [m_sc[s] for s in range(ns)]
    seg_gates = {}
    for g, segments in enumerate(groups):
        gates = _dot_nt(h[g], wg_ref[...]) + bg_ref[...]
        if slabs:
            project(*slabs.pop(0))
        for s, r0 in segments:
            lo = r0 - g * g_rows
            ig = gates[lo:lo + lc, 0:LANES]
            lf3 = _split3(jax.nn.log_sigmoid(gates[lo:lo + lc, LANES:2 * LANES]))
            b_all = _dot(tri, lf3[0]) + _dot(tri, lf3[1]) + _dot(tri, lf3[2])
            if slabs and (s, r0) == segments[0]:
                project(*slabs.pop(0))
            b3 = _split3(b_all)
            g3 = _split3(ig)
            brow = _dot_nt(sel, b3[0]) + _dot_nt(sel, b3[1]) + _dot_nt(sel, b3[2])
            igrow = _dot_nt(sel, g3[0]) + _dot_nt(sel, g3[1]) + _dot_nt(sel, g3[2])
            b_last_all = b_all[lc - 1:lc]
            g_all = b_last_all - b_all + ig
            m_new = jnp.maximum(b_last_all + m_prev[s], jnp.max(g_all, axis=0, keepdims=True))
            seg_gates[(s, r0)] = dict(b=b_all, inter=b_all + m_prev[s], brow=brow, igrow=igrow,
                                      decay=jnp.exp(b_last_all + m_prev[s] - m_new),
                                      wgt=jnp.exp(g_all - m_new))
            m_prev[s] = m_new
    for s in range(ns):
        m_sc[s] = m_prev[s]
    while slabs:
        project(*slabs.pop(0))

    def unit(seg, hd):
        s, r0 = seg
        rows = slice(r0, r0 + lc)
        ln = slice(hd * dh, (hd + 1) * dh)
        gts = seg_gates[seg]
        b_c = gts["b"][:, hd:hd + 1]
        b_r = gts["brow"][hd:hd + 1, :]
        ig_r = gts["igrow"][hd:hd + 1, :]
        inter = gts["inter"][:, hd:hd + 1]
        dmat = jnp.where(causal, b_c - b_r + ig_r, -jnp.inf)
        m_t = jnp.maximum(inter, jnp.max(dmat, axis=-1, keepdims=True))
        a = jnp.exp(inter - m_t)
        q = q_sc[rows, ln]
        k = k_sc[rows, ln]
        v = v_sc[rows, ln]
        sc = _dot_nt(q, k) * jnp.exp(dmat - m_t)
        c_old = c_sc[s, hd]
        n_old = n_sc[s, hd]
        num = a * _dot_nt(q, c_old.astype(BF16)) + _dot(sc.astype(BF16), v)
        qn = jnp.sum(q.astype(F32) * n_old, axis=-1, keepdims=True)
        den = a * qn + jnp.sum(sc, axis=-1, keepdims=True)
        hv = num / jnp.maximum(jnp.abs(den), jnp.exp(-m_t))
        mu = jnp.mean(hv, axis=-1, keepdims=True)
        dlt = hv - mu
        var = jnp.mean(dlt * dlt, axis=-1, keepdims=True)
        hn = dlt * lax.rsqrt(var + LN_EPS) * gain_ref[:, ln]
        mixed_sc[rows, ln] = (hn * oz_sc[rows, ln]).astype(BF16)
        decay = gts["decay"][:, hd:hd + 1]
        wgt = gts["wgt"][:, hd:hd + 1]
        vw = (v.astype(F32) * wgt).astype(BF16)
        c_sc[s, hd] = decay * c_old + _dot_tn(vw, k)
        n_sc[s, hd] = decay * n_old + jnp.sum(wgt * k.astype(F32), axis=0, keepdims=True)

    def out_project(g):
        res = x_rows(g) + _rmsnorm(_dot(mixed_sc[group_rows(g), :], wout_ref[...]), gpost_ref[...])
        if ns == 1:
            y_ref[0, group_rows(g), :] = res
        else:
            for s in range(ns):
                y_ref[s] = res[s * tt:(s + 1) * tt]

    n_parts = 4
    for g, segments in enumerate(groups):
        for i, (seg, hd) in enumerate((seg, hd) for seg in segments for hd in range(nh)):
            unit(seg, hd)
            if g + 1 < n_groups and i < n_parts:
                project(g + 1, i)
    for g in range(n_groups):
        out_project(g)
    cout_ref[...] = c_sc[...]
    nout_ref[...] = n_sc[...]
    mout_ref[...] = m_sc[...]


def _odd_layer(x, state, gpre, gpost, w_main, w_g, b_g, gain, w_out, *, tt, lc, ns):
    b, t_len, d = x.shape
    nt = t_len // tt
    assert t_len % tt == 0 and tt % lc == 0 and b % ns == 0
    nh, dh = ML_HEADS, ML_HEAD_DIM
    has_state = state is not None

    def whole(shape):
        return pl.BlockSpec(shape, lambda i, j: (0,) * len(shape))

    def per_seq(shape):
        return pl.BlockSpec((ns,) + shape, lambda i, j: (i,) + (0,) * len(shape))

    st_specs = [per_seq((nh, dh, dh)), per_seq((nh, 1, dh)), per_seq((1, LANES))]
    in_specs = [pl.BlockSpec((ns, tt, d), lambda i, j: (i, j, 0))]
    args = [x]
    if has_state:
        in_specs += st_specs
        args += list(state)
    in_specs += [whole((1, d)), whole((1, d)), whole((5 * ML_WIDTH, d)),
                 whole((2 * LANES, d)), whole((1, 2 * LANES)), whole((1, ML_WIDTH)), whole((ML_WIDTH, d))]
    args += [gpre, gpost, w_main, w_g, b_g, gain, w_out]
    out_shape = (jax.ShapeDtypeStruct((b, t_len, d), F32),
                 jax.ShapeDtypeStruct((b, nh, dh, dh), F32),
                 jax.ShapeDtypeStruct((b, nh, 1, dh), F32),
                 jax.ShapeDtypeStruct((b, 1, LANES), F32))
    out_specs = (pl.BlockSpec((ns, tt, d), lambda i, j: (i, j, 0)), *st_specs)
    rows = ns * tt
    scratch = [pltpu.VMEM((rows, ML_WIDTH), BF16), pltpu.VMEM((rows, ML_WIDTH), BF16),
               pltpu.VMEM((rows, ML_WIDTH), BF16), pltpu.VMEM((rows, ML_WIDTH), F32),
               pltpu.VMEM((rows, ML_WIDTH), BF16),
               pltpu.VMEM((ns, nh, dh, dh), F32), pltpu.VMEM((ns, nh, 1, dh), F32),
               pltpu.VMEM((ns, 1, LANES), F32)]
    return pl.pallas_call(
        functools.partial(_odd_kernel, tt=tt, lc=lc, ns=ns, has_state=has_state),
        grid=(b // ns, nt), in_specs=in_specs, out_specs=out_specs, out_shape=out_shape,
        scratch_shapes=scratch,
        compiler_params=pltpu.CompilerParams(dimension_semantics=("parallel", "arbitrary"),
                                             vmem_limit_bytes=V7X_VMEM_LIMIT_BYTES),
        name="odd_layer_state" if has_state else "odd_layer",
    )(*args)


PROMPT_TILE = 512
ATT_BLOCK_CHUNKS = 2
ML_PROMPT_CHUNK = 256
SAMPLE_SEQS_PER_STEP = 8


def kernel(x_prompt, x_sample, cache_pool, cache_k, cache_v, state_C, state_n, state_m, norm_pre, norm_post,
           w_in_even, w_pool_mix, pool_scale, rel_bias, w_out_even, w_in_odd, b_gate_odd, mlstm_norm, w_out_odd):
    depth = norm_pre.shape[0]
    bp, dec_seq = x_prompt.shape[0], x_sample.shape[1]
    bs = x_sample.shape[0]
    nh = ML_HEADS
    xp, xs = x_prompt, x_sample
    outs = {name: [] for name in ("pool_p", "k_p", "v_p", "C_p", "n_p", "m_p",
                                  "pool_s", "k_s", "v_s", "C_s", "n_s", "m_s")}
    for layer in range(depth):
        gpre = norm_pre[layer][None, :]
        gpost = norm_post[layer][None, :]
        if layer % 2 == 0:
            e = layer // 2
            w_in = _cast_range_bf16(w_in_even[e], 0, EVEN_IN, axis=1, block=CAST_BLOCK)
            w_mix = w_pool_mix[e].astype(BF16)
            w_out = w_out_even[e].astype(BF16)
            scale = pool_scale[e][None, :]
            bias_p = _expand_bias(rel_bias[e], CHUNK, ATT_BLOCK_CHUNKS)
            assert dec_seq <= CHUNK
            rb_p = ATT_BLOCK_CHUNKS * CHUNK
            bias_s = jnp.concatenate([bias_p[:, 0:dec_seq, 0:BAND + dec_seq],
                                      bias_p[:, rb_p:rb_p + dec_seq, 0:BAND + dec_seq]], axis=1)
            wt_next = jnp.swapaxes(w_in_odd[(layer + 1) // 2], 0, 1) if layer + 1 < depth else None
            res = _even_layer(xp, None, gpre, gpost, w_in, w_mix, scale, bias_p, w_out,
                              tt=PROMPT_TILE, rb=ATT_BLOCK_CHUNKS * CHUNK, ns=1, pos0=0,
                              cast_along=None if wt_next is None else (wt_next, 5 * ML_WIDTH))
            xp, pp, kp, vp = res[0:4]
            w_main_next = res[4] if wt_next is not None else None
            hist = (jnp.pad(cache_pool[e], ((0, 0), (POOL_HIST_ROWS - POOL_HIST, 0), (0, 0))),
                    cache_k[e].reshape(bs, BAND, ATT_WIDTH).astype(BF16),
                    cache_v[e].reshape(bs, BAND, ATT_WIDTH).astype(BF16))
            xs, ps, ks, vs = _even_layer(xs, hist, gpre, gpost, w_in, w_mix, scale, bias_s, w_out,
                                         tt=dec_seq, rb=dec_seq, ns=SAMPLE_SEQS_PER_STEP, pos0=PAST_LEN)
            outs["pool_p"].append(pp[:, POOL_HIST_ROWS - POOL_HIST:])
            outs["k_p"].append(kp.reshape(bp, -1, ATT_HEADS, ATT_HEAD_DIM))
            outs["v_p"].append(vp.reshape(bp, -1, ATT_HEADS, ATT_HEAD_DIM))
            outs["pool_s"].append(ps[:, POOL_HIST_ROWS - POOL_HIST:])
            outs["k_s"].append(ks.reshape(bs, -1, ATT_HEADS, ATT_HEAD_DIM))
            outs["v_s"].append(vs.reshape(bs, -1, ATT_HEADS, ATT_HEAD_DIM))
        else:
            o = layer // 2
            wt = jnp.swapaxes(w_in_odd[o], 0, 1)
            w_main = w_main_next
            w_gates = _cast_range_bf16(wt, 5 * ML_WIDTH, 2 * nh, axis=0, block=2 * nh)
            pad_h = ((0, LANES - nh), (0, 0))
            w_g = jnp.concatenate([jnp.pad(w_gates[:nh], pad_h), jnp.pad(w_gates[nh:], pad_h)], axis=0)
            b_g = jnp.concatenate([jnp.pad(b_gate_odd[o][:nh], (0, LANES - nh)),
                                   jnp.pad(b_gate_odd[o][nh:], (0, LANES - nh))])[None, :]
            gain = mlstm_norm[o][None, :]
            w_out = w_out_odd[o].astype(BF16)
            xp, cp, np_, mp = _odd_layer(xp, None, gpre, gpost, w_main, w_g, b_g, gain, w_out,
                                         tt=PROMPT_TILE, lc=ML_PROMPT_CHUNK, ns=1)
            state = (state_C[o], state_n[o][:, :, None, :],
                     jnp.pad(state_m[o], ((0, 0), (0, LANES - nh)))[:, None, :])
            xs, cs, ns, ms = _odd_layer(xs, state, gpre, gpost, w_main, w_g, b_g, gain, w_out,
                                        tt=dec_seq, lc=dec_seq, ns=SAMPLE_SEQS_PER_STEP)
            outs["C_p"].append(cp)
            outs["n_p"].append(np_[:, :, 0, :])
            outs["m_p"].append(mp[:, 0, :nh])
            outs["C_s"].append(cs)
            outs["n_s"].append(ns[:, :, 0, :])
            outs["m_s"].append(ms[:, 0, :nh])
    return (xp, xs,
            jnp.stack(outs["pool_p"]), jnp.stack(outs["k_p"]), jnp.stack(outs["v_p"]),
            jnp.stack(outs["C_p"]), jnp.stack(outs["n_p"]), jnp.stack(outs["m_p"]),
            jnp.stack(outs["pool_s"]), jnp.stack(outs["k_s"]), jnp.stack(outs["v_s"]),
            jnp.stack(outs["C_s"]), jnp.stack(outs["n_s"]), jnp.stack(outs["m_s"]))
```

```python
import functools

import jax
import jax.numpy as jnp
from jax import lax
from jax.experimental import pallas as pl
from jax.experimental.pallas import tpu as pltpu

D_MODEL = 1024
PAST_LEN = 4096
CHUNK = 64
POOL_WINDOWS = (2, 4, 8, 16)
POOL_GROUPS = len(POOL_WINDOWS)
POOL_WIDTH = D_MODEL // 2
POOL_GROUP_DIM = POOL_WIDTH // POOL_GROUPS
POOL_HIST = max(POOL_WINDOWS) - 1
POOL_HIST_ROWS = 16
ATT_WIDTH = D_MODEL // 2
ATT_HEADS = 8
ATT_HEAD_DIM = ATT_WIDTH // ATT_HEADS
HEAD_PAIRS = ATT_HEADS // 2
BAND = 8 * CHUNK
REL_CLIP = 128
REL_TABLE = 2 * REL_CLIP + 1
REL_TABLE_PAD = 384
ML_WIDTH = D_MODEL
ML_HEADS = 4
ML_HEAD_DIM = ML_WIDTH // ML_HEADS
EVEN_IN = 2 * POOL_WIDTH + 4 * ATT_WIDTH
RMS_EPS = 1e-6
LN_EPS = 1e-6
LOG2_E = 1.4426950408889634

LANES = 128
SUBLANES = 8
BF16_ROWS_PER_VREG = 2 * SUBLANES
BIAS_ROW_UNROLL = 8
SCORE_SLOTS = 3
CAST_BLOCK = 1024
V7X_VMEM_BYTES = 64 * 1024 * 1024
V7X_VMEM_LIMIT_BYTES = V7X_VMEM_BYTES * 7 // 8

BF16 = jnp.bfloat16
F32 = jnp.float32


def _dot(a, b):
    return jnp.dot(a, b, preferred_element_type=F32)


def _dot_nt(a, b):
    return lax.dot_general(a, b, (((1,), (1,)), ((), ())), preferred_element_type=F32)


def _dot_tn(a, b):
    return lax.dot_general(a, b, (((0,), (0,)), ((), ())), preferred_element_type=F32)


def _split3(x):
    hi = x.astype(BF16)
    r1 = x - hi.astype(F32)
    mid = r1.astype(BF16)
    lo = (r1 - mid.astype(F32)).astype(BF16)
    return hi, mid, lo


def _rmsnorm(x, g):
    return x * lax.rsqrt(jnp.mean(x * x, axis=-1, keepdims=True) + RMS_EPS) * g


def _silu(x):
    return x * jax.nn.sigmoid(x)


def _cast_kernel(src_ref, dst_ref):
    dst_ref[...] = src_ref[...].astype(dst_ref.dtype)


def _cast_range_bf16(w, start, n, axis, block):
    assert n % block == 0 and start % block == 0 and start + n <= w.shape[axis]
    first = start // block
    if axis == 0:
        in_spec = pl.BlockSpec((block, w.shape[1]), lambda j: (first + j, 0))
        out_spec = pl.BlockSpec((block, w.shape[1]), lambda j: (j, 0))
        shape = (n, w.shape[1])
    else:
        in_spec = pl.BlockSpec((w.shape[0], block), lambda j: (0, first + j))
        out_spec = pl.BlockSpec((w.shape[0], block), lambda j: (0, j))
        shape = (w.shape[0], n)
    return pl.pallas_call(
        _cast_kernel, grid=(n // block,), in_specs=[in_spec], out_specs=out_spec,
        out_shape=jax.ShapeDtypeStruct(shape, BF16), name="cast_bf16",
    )(w)


def _bias_kernel(tab_ref, out_ref, *, lc, nb):
    rb = nb * lc
    kb = BAND + rb
    n_dist = rb - 1 + kb
    gw = (n_dist + LANES - 1) // LANES * LANES
    parts = _split3(tab_ref[...])
    dist = lax.broadcasted_iota(jnp.int32, (REL_TABLE_PAD, gw), 1) - (rb - 1)
    ti = lax.broadcasted_iota(jnp.int32, (REL_TABLE_PAD, gw), 0)
    onehot = jnp.where(ti == jnp.clip(BAND - dist, -REL_CLIP, REL_CLIP) + REL_CLIP, 1.0, 0.0).astype(BF16)
    by_dist = _dot(parts[0], onehot) + _dot(parts[1], onehot) + _dot(parts[2], onehot)
    key_chunk = lax.broadcasted_iota(jnp.int32, (ATT_HEADS, kb), 1) // lc

    def row(qi, carry):
        shift = lax.rem(gw - (rb - 1 - qi), gw)
        acc = pltpu.roll(by_dist, shift, axis=1)[:, 0:kb]
        q_chunk = qi // lc
        in_band = (key_chunk >= q_chunk) & (key_chunk <= q_chunk + BAND // lc)
        out_ref[qi] = jnp.where(in_band, acc * LOG2_E, -jnp.inf)
        return carry

    lax.fori_loop(0, rb, row, 0, unroll=BIAS_ROW_UNROLL)


def _expand_bias(table, lc, nb):
    rb = nb * lc
    kb = BAND + rb
    tab = jnp.pad(table.astype(F32), ((0, 0), (0, REL_TABLE_PAD - REL_TABLE)))
    out = pl.pallas_call(
        functools.partial(_bias_kernel, lc=lc, nb=nb),
        out_shape=jax.ShapeDtypeStruct((rb, ATT_HEADS, kb), F32),
        name="rel_bias_expand",
    )(tab)
    return out.transpose(1, 0, 2).reshape(HEAD_PAIRS, 2 * rb, kb)


def _even_kernel(*refs, tt, rb, ns, has_hist, cast_along, pos0):
    refs = list(refs)
    x_ref = refs.pop(0)
    if has_hist:
        hu_ref, hk_ref, hv_ref = refs[0:3]
        refs = refs[3:]
    gpre_ref, gpost_ref, win_ref, wmix_ref, scale_ref, bias_ref, wout_ref = refs[0:7]
    refs = refs[7:]
    if cast_along:
        wsrc_ref = refs.pop(0)
        y_ref, pool_ref, kout_ref, vout_ref, wdst_ref = refs[0:5]
        refs = refs[5:]
        wdst_ref[...] = wsrc_ref[...].astype(BF16)
    else:
        y_ref, pool_ref, kout_ref, vout_ref = refs[0:4]
        refs = refs[4:]
    uext, kext, vext, q_sc, gate_sc, mixed_sc, s_sc = refs
    t = pl.program_id(1)
    hr = POOL_HIST_ROWS
    kb = BAND + rb
    seqs = range(ns)

    def seq_rows(s):
        return slice(s * tt, (s + 1) * tt)

    @pl.when(t == 0)
    def _():
        for s in seqs:
            if has_hist:
                uext[s, 0:hr, :] = hu_ref[s]
                kext[s, 0:BAND, :] = hk_ref[s]
                vext[s, 0:BAND, :] = hv_ref[s]
            else:
                uext[s, 0:hr, :] = jnp.zeros((hr, POOL_WIDTH), F32)
                kext[s, 0:BAND, :] = jnp.zeros((BAND, ATT_WIDTH), BF16)
                vext[s, 0:BAND, :] = jnp.zeros((BAND, ATT_WIDTH), BF16)

    @pl.when(t > 0)
    def _():
        for s in seqs:
            uext[s, 0:hr, :] = uext[s, tt:tt + hr, :]
            kext[s, 0:BAND, :] = kext[s, tt:tt + BAND, :]
            vext[s, 0:BAND, :] = vext[s, tt:tt + BAND, :]

    def load_x():
        return x_ref[0] if ns == 1 else jnp.concatenate([x_ref[s] for s in seqs], axis=0)

    h = _rmsnorm(load_x(), gpre_ref[...]).astype(BF16)
    p_, a_ = POOL_WIDTH, ATT_WIDTH

    def project(part):
        cols = slice(part * a_, (part + 1) * a_)
        res = _dot(h, win_ref[:, cols])
        if part == 0:
            for s in seqs:
                uext[s, hr:hr + tt, :] = res[seq_rows(s)]
        elif part == 1:
            q_sc[...] = (res * (ATT_HEAD_DIM ** -0.5 * LOG2_E)).astype(BF16)
        elif part == 2:
            for s in seqs:
                kext[s, BAND:BAND + tt, :] = res[seq_rows(s)].astype(BF16)
                kout_ref[s] = res[seq_rows(s)]
        elif part == 3:
            for s in seqs:
                vext[s, BAND:BAND + tt, :] = res[seq_rows(s)].astype(BF16)
                vout_ref[s] = res[seq_rows(s)]
        else:
            gate_sc[:, (part - 4) * a_:(part - 3) * a_] = _silu(res)

    pos = pos0 + t * tt + lax.broadcasted_iota(jnp.int32, (tt, 1), 0)

    def pool_group(g):
        w = POOL_WINDOWS[g]
        ln = slice(g * POOL_GROUP_DIM, (g + 1) * POOL_GROUP_DIM)
        count = jnp.minimum(pos + 1, w).astype(F32)
        pooled = []
        for s in seqs:
            tok = uext[s, hr:hr + tt, ln]
            win_sum = tok
            for back in range(1, w):
                win_sum = win_sum + uext[s, hr - back:hr - back + tt, ln]
            pooled.append(win_sum / count - tok)
        pooled = pooled[0] if ns == 1 else jnp.concatenate(pooled, axis=0)
        mixed = _dot(pooled.astype(BF16), wmix_ref[g]) * scale_ref[:, ln]
        mixed_sc[:, ln] = (mixed * gate_sc[:, ln]).astype(BF16)

    assert p_ == a_ and POOL_GROUPS == 4
    project(0)
    project(4)
    for g, part in enumerate((1, 2, 3, 5)):
        project(part)
        pool_group(g)

    lane = lax.broadcasted_iota(jnp.int32, (rb, LANES), 1)
    even_head = lane < ATT_HEAD_DIM

    def attend(sequence_start):
        units = [(s, blk, p) for s in seqs for blk in range(tt // rb) for p in range(HEAD_PAIRS)]

        def band(blk):
            r0 = blk * rb
            k0 = BAND if sequence_start else r0
            c0 = BAND - r0 if sequence_start else 0
            return r0, k0, c0, kb - c0

        def stage_scores(i):
            s, blk, p = units[i]
            r0, k0, c0, kw = band(blk)
            ln = slice(p * LANES, (p + 1) * LANES)
            qp = q_sc[s * tt + r0:s * tt + r0 + rb, ln]
            zero = jnp.zeros_like(qp)
            q2 = jnp.concatenate([jnp.where(even_head, qp, zero), jnp.where(even_head, zero, qp)], axis=0)
            s_sc[i % SCORE_SLOTS, :, 0:kw] = _dot_nt(q2, kext[s, k0:k0 + kw, ln]) + bias_ref[p, :, c0:c0 + kw]

        def finish(i):
            s, blk, p = units[i]
            r0, k0, c0, kw = band(blk)
            ln = slice(p * LANES, (p + 1) * LANES)
            row_max = jnp.max(s_sc[i % SCORE_SLOTS, :, 0:kw], axis=-1, keepdims=True)
            e = jnp.exp2(s_sc[i % SCORE_SLOTS, :, 0:kw] - row_max)
            denom = jnp.sum(e, axis=-1, keepdims=True)
            o2 = _dot(e.astype(BF16), vext[s, k0:k0 + kw, ln]) / denom
            o = jnp.where(even_head, o2[0:rb], o2[rb:2 * rb])
            rows = slice(s * tt + r0, s * tt + r0 + rb)
            mo = slice(POOL_WIDTH + p * LANES, POOL_WIDTH + (p + 1) * LANES)
            mixed_sc[rows, mo] = (o * gate_sc[rows, mo]).astype(BF16)

        ahead = SCORE_SLOTS - 1
        for i in range(min(ahead, len(units))):
            stage_scores(i)
        for i in range(len(units)):
            if i + ahead < len(units):
                stage_scores(i + ahead)
            finish(i)

    if has_hist:
        attend(False)
    else:
        pl.when(t == 0)(functools.partial(attend, True))
        pl.when(t > 0)(functools.partial(attend, False))

    y = load_x() + _rmsnorm(_dot(mixed_sc[...], wout_ref[...]), gpost_ref[...])
    for s in seqs:
        y_ref[s] = y[seq_rows(s)]
        pool_ref[s] = uext[s, tt:tt + hr, :]


def _even_layer(x, hist, gpre, gpost, w_in, w_mix, scale, bias, w_out, *, tt, rb, ns, pos0, cast_along=None):
    b, t_len, d = x.shape
    has_hist = hist is not None
    nt = t_len // tt
    assert t_len % tt == 0 and tt % rb == 0 and b % ns == 0
    assert has_hist or rb % LANES == 0, "band start at a sequence start must stay lane aligned in the bias"
    keep = min(BAND, t_len)
    assert keep == tt, "key/value cache rows must be exactly the last time tile"
    assert nt == 1 or tt >= BAND
    kb = BAND + rb

    def whole(shape):
        return pl.BlockSpec(shape, lambda i, j: (0,) * len(shape))

    def per_seq(shape):
        return pl.BlockSpec((ns,) + shape, lambda i, j: (i, 0, 0))

    in_specs = [pl.BlockSpec((ns, tt, d), lambda i, j: (i, j, 0))]
    args = [x]
    if has_hist:
        in_specs += [per_seq((POOL_HIST_ROWS, POOL_WIDTH)), per_seq((BAND, ATT_WIDTH)), per_seq((BAND, ATT_WIDTH))]
        args += list(hist)
    in_specs += [whole((1, d)), whole((1, d)), whole((d, EVEN_IN)),
                 whole((POOL_GROUPS, POOL_GROUP_DIM, POOL_GROUP_DIM)), whole((1, POOL_WIDTH)),
                 whole((HEAD_PAIRS, 2 * rb, kb)), whole((POOL_WIDTH + ATT_WIDTH, d))]
    args += [gpre, gpost, w_in, w_mix, scale, bias, w_out]
    out_shape = [jax.ShapeDtypeStruct((b, t_len, d), F32),
                 jax.ShapeDtypeStruct((b, POOL_HIST_ROWS, POOL_WIDTH), F32),
                 jax.ShapeDtypeStruct((b, keep, ATT_WIDTH), F32),
                 jax.ShapeDtypeStruct((b, keep, ATT_WIDTH), F32)]
    out_specs = [pl.BlockSpec((ns, tt, d), lambda i, j: (i, j, 0)),
                 per_seq((POOL_HIST_ROWS, POOL_WIDTH)), per_seq((keep, ATT_WIDTH)), per_seq((keep, ATT_WIDTH))]
    if cast_along is not None:
        src, n_rows = cast_along
        steps = (b // ns) * nt
        share = n_rows // steps
        assert n_rows % steps == 0 and share % BF16_ROWS_PER_VREG == 0 and n_rows <= src.shape[0]
        share_spec = pl.BlockSpec((share, src.shape[1]), lambda i, j: (i * nt + j, 0))
        in_specs.append(share_spec)
        args.append(src)
        out_specs.append(share_spec)
        out_shape.append(jax.ShapeDtypeStruct((n_rows, src.shape[1]), BF16))
    scratch = [pltpu.VMEM((ns, POOL_HIST_ROWS + tt, POOL_WIDTH), F32),
               pltpu.VMEM((ns, BAND + tt, ATT_WIDTH), BF16),
               pltpu.VMEM((ns, BAND + tt, ATT_WIDTH), BF16),
               pltpu.VMEM((ns * tt, ATT_WIDTH), BF16),
               pltpu.VMEM((ns * tt, POOL_WIDTH + ATT_WIDTH), F32),
               pltpu.VMEM((ns * tt, POOL_WIDTH + ATT_WIDTH), BF16),
               pltpu.VMEM((SCORE_SLOTS, 2 * rb, kb), F32)]
    return pl.pallas_call(
        functools.partial(_even_kernel, tt=tt, rb=rb, ns=ns, has_hist=has_hist,
                          cast_along=cast_along is not None, pos0=pos0),
        grid=(b // ns, nt), in_specs=in_specs, out_specs=tuple(out_specs), out_shape=tuple(out_shape),
        scratch_shapes=scratch,
        compiler_params=pltpu.CompilerParams(dimension_semantics=("parallel", "arbitrary"),
                                             vmem_limit_bytes=V7X_VMEM_LIMIT_BYTES),
        name="even_layer_hist" if has_hist else "even_layer",
    )(*args)


def _odd_kernel(*refs, tt, lc, ns, has_state):
    if has_state:
        x_ref, c0_ref, n0_ref, m0_ref = refs[0:4]
        refs = refs[4:]
    else:
        x_ref = refs[0]
        refs = refs[1:]
    (gpre_ref, gpost_ref, w_ref, wg_ref, bg_ref, gain_ref, wout_ref,
     y_ref, cout_ref, nout_ref, mout_ref, q_sc, k_sc, v_sc, oz_sc, mixed_sc, c_sc, n_sc, m_sc) = refs
    t = pl.program_id(1)
    w_, dh, nh = ML_WIDTH, ML_HEAD_DIM, ML_HEADS
    assert nh <= SUBLANES, "the per-head gate lanes are moved to sublanes eight at a time"
    if ns == 1:
        groups = [[(0, c * lc)] for c in range(tt // lc)]
    else:
        assert tt == lc
        groups = [[(s, s * tt) for s in range(ns)]]
    n_groups = len(groups)
    g_rows = lc * len(groups[0])

    def group_rows(g):
        return slice(g * g_rows, (g + 1) * g_rows)

    @pl.when(t == 0)
    def _():
        if has_state:
            c_sc[...] = c0_ref[...]
            n_sc[...] = n0_ref[...]
            m_sc[...] = m0_ref[...]
        else:
            c_sc[...] = jnp.zeros(c_sc.shape, F32)
            n_sc[...] = jnp.zeros(n_sc.shape, F32)
            m_sc[...] = jnp.zeros(m_sc.shape, F32)

    def x_rows(g):
        if ns == 1:
            return x_ref[0, group_rows(g), :]
        return jnp.concatenate([x_ref[s] for s in range(ns)], axis=0)

    h = [_rmsnorm(x_rows(g), gpre_ref[...]).astype(BF16) for g in range(n_groups)]

    def project(g, part):
        rows = group_rows(g)
        if part == 0:
            q_sc[rows, :] = _dot_nt(h[g], w_ref[0:w_, :]).astype(BF16)
        elif part == 1:
            k_sc[rows, :] = (_dot_nt(h[g], w_ref[w_:2 * w_, :]) * (dh ** -0.5)).astype(BF16)
        elif part == 2:
            v_sc[rows, :] = _dot_nt(h[g], w_ref[2 * w_:3 * w_, :]).astype(BF16)
        else:
            oz_sc[rows, :] = (jax.nn.sigmoid(_dot_nt(h[g], w_ref[3 * w_:4 * w_, :]))
                              * _silu(_dot_nt(h[g], w_ref[4 * w_:5 * w_, :])))

    tri = jnp.where(lax.broadcasted_iota(jnp.int32, (lc, lc), 1) <= lax.broadcasted_iota(jnp.int32, (lc, lc), 0),
                    1.0, 0.0).astype(BF16)
    sel = jnp.where(lax.broadcasted_iota(jnp.int32, (SUBLANES, LANES), 0)
                    == lax.broadcasted_iota(jnp.int32, (SUBLANES, LANES), 1), 1.0, 0.0).astype(BF16)
    causal = (lax.broadcasted_iota(jnp.int32, (lc, lc), 1) <= lax.broadcasted_iota(jnp.int32, (lc, lc), 0))

    slabs = [(0, part) for part in range(4)]
    m_prev = [m_sc[s] for s in range(ns)]
    seg_gates = {}
    for g, segments in enumerate(groups):
        gates = _dot_nt(h[g], wg_ref[...]) + bg_ref[...]
        if slabs:
            project(*slabs.pop(0))
        for s, r0 in segments:
            lo = r0 - g * g_rows
            ig = gates[lo:lo + lc, 0:LANES]
            lf3 = _split3(jax.nn.log_sigmoid(gates[lo:lo + lc, LANES:2 * LANES]))
            b_all = _dot(tri, lf3[0]) + _dot(tri, lf3[1]) + _dot(tri, lf3[2])
            if slabs and (s, r0) == segments[0]:
                project(*slabs.pop(0))
            b3 = _split3(b_all)
            g3 = _split3(ig)
            brow = _dot_nt(sel, b3[0]) + _dot_nt(sel, b3[1]) + _dot_nt(sel, b3[2])
            igrow = _dot_nt(sel, g3[0]) + _dot_nt(sel, g3[1]) + _dot_nt(sel, g3[2])
            b_last_all = b_all[lc - 1:lc]
            g_all = b_last_all - b_all + ig
            m_new = jnp.maximum(b_last_all + m_prev[s], jnp.max(g_all, axis=0, keepdims=True))
            seg_gates[(s, r0)] = dict(b=b_all, inter=b_all + m_prev[s], brow=brow, igrow=igrow,
                                      decay=jnp.exp(b_last_all + m_prev[s] - m_new),
                                      wgt=jnp.exp(g_all - m_new))
            m_prev[s] = m_new
    for s in range(ns):
        m_sc[s] = m_prev[s]
    while slabs:
        project(*slabs.pop(0))

    def unit(seg, hd):
        s, r0 = seg
        rows = slice(r0, r0 + lc)
        ln = slice(hd * dh, (hd + 1) * dh)
        gts = seg_gates[seg]
        qc = _dot_nt(q_sc[rows, ln], c_sc[s, hd].astype(BF16))
        b_c = gts["b"][:, hd:hd + 1]
        b_r = gts["brow"][hd:hd + 1, :]
        ig_r = gts["igrow"][hd:hd + 1, :]
        inter = gts["inter"][:, hd:hd + 1]
        dmat = jnp.where(causal, b_c - b_r + ig_r, -jnp.inf)
        m_t = jnp.maximum(inter, jnp.max(dmat, axis=-1, keepdims=True))
        a = jnp.exp(inter - m_t)
        q = q_sc[rows, ln]
        k = k_sc[rows, ln]
        v = v_sc[rows, ln]
        sc = _dot_nt(q, k) * jnp.exp(dmat - m_t)
        c_old = c_sc[s, hd]
        n_old = n_sc[s, hd]
        num = a * qc + _dot(sc.astype(BF16), v)
        qn = jnp.sum(q.astype(F32) * n_old, axis=-1, keepdims=True)
        den = a * qn + jnp.sum(sc, axis=-1, keepdims=True)
        hv = num / jnp.maximum(jnp.abs(den), jnp.exp(-m_t))
        mu = jnp.mean(hv, axis=-1, keepdims=True)
        dlt = hv - mu
        var = jnp.mean(dlt * dlt, axis=-1, keepdims=True)
        hn = dlt * lax.rsqrt(var + LN_EPS) * gain_ref[:, ln]
        mixed_sc[rows, ln] = (hn * oz_sc[rows, ln]).astype(BF16)
        decay = gts["decay"][:, hd:hd + 1]
        wgt = gts["wgt"][:, hd:hd + 1]
        vw = (v.astype(F32) * wgt).astype(BF16)
        c_sc[s, hd] = decay * c_old + _dot_tn(vw, k)
        n_sc[s, hd] = decay * n_old + jnp.sum(wgt * k.astype(F32), axis=0, keepdims=True)

    def out_project(g):
        res = x_rows(g) + _rmsnorm(_dot(mixed_sc[group_rows(g), :], wout_ref[...]), gpost_ref[...])
        if ns == 1:
            y_ref[0, group_rows(g), :] = res
        else:
            for s in range(ns):
                y_ref[s] = res[s * tt:(s + 1) * tt]

    n_parts = 4
    for g, segments in enumerate(groups):
        for i, (seg, hd) in enumerate((seg, hd) for seg in segments for hd in range(nh)):
            unit(seg, hd)
            if g + 1 < n_groups and i < n_parts:
                project(g + 1, i)
    for g in range(n_groups):
        out_project(g)
    cout_ref[...] = c_sc[...]
    nout_ref[...] = n_sc[...]
    mout_ref[...] = m_sc[...]


def _odd_layer(x, state, gpre, gpost, w_main, w_g, b_g, gain, w_out, *, tt, lc, ns):
    b, t_len, d = x.shape
    nt = t_len // tt
    assert t_len % tt == 0 and tt % lc == 0 and b % ns == 0
    nh, dh = ML_HEADS, ML_HEAD_DIM
    has_state = state is not None

    def whole(shape):
        return pl.BlockSpec(shape, lambda i, j: (0,) * len(shape))

    def per_seq(shape):
        return pl.BlockSpec((ns,) + shape, lambda i, j: (i,) + (0,) * len(shape))

    st_specs = [per_seq((nh, dh, dh)), per_seq((nh, 1, dh)), per_seq((1, LANES))]
    in_specs = [pl.BlockSpec((ns, tt, d), lambda i, j: (i, j, 0))]
    args = [x]
    if has_state:
        in_specs += st_specs
        args += list(state)
    in_specs += [whole((1, d)), whole((1, d)), whole((5 * ML_WIDTH, d)),
                 whole((2 * LANES, d)), whole((1, 2 * LANES)), whole((1, ML_WIDTH)), whole((ML_WIDTH, d))]
    args += [gpre, gpost, w_main, w_g, b_g, gain, w_out]
    out_shape = (jax.ShapeDtypeStruct((b, t_len, d), F32),
                 jax.ShapeDtypeStruct((b, nh, dh, dh), F32),
                 jax.ShapeDtypeStruct((b, nh, 1, dh), F32),
                 jax.ShapeDtypeStruct((b, 1, LANES), F32))
    out_specs = (pl.BlockSpec((ns, tt, d), lambda i, j: (i, j, 0)), *st_specs)
    rows = ns * tt
    scratch = [pltpu.VMEM((rows, ML_WIDTH), BF16), pltpu.VMEM((rows, ML_WIDTH), BF16),
               pltpu.VMEM((rows, ML_WIDTH), BF16), pltpu.VMEM((rows, ML_WIDTH), F32),
               pltpu.VMEM((rows, ML_WIDTH), BF16),
               pltpu.VMEM((ns, nh, dh, dh), F32), pltpu.VMEM((ns, nh, 1, dh), F32),
               pltpu.VMEM((ns, 1, LANES), F32)]
    return pl.pallas_call(
        functools.partial(_odd_kernel, tt=tt, lc=lc, ns=ns, has_state=has_state),
        grid=(b // ns, nt), in_specs=in_specs, out_specs=out_specs, out_shape=out_shape,
        scratch_shapes=scratch,
        compiler_params=pltpu.CompilerParams(dimension_semantics=("parallel", "arbitrary"),
                                             vmem_limit_bytes=V7X_VMEM_LIMIT_BYTES),
        name="odd_layer_state" if has_state else "odd_layer",
    )(*args)


PROMPT_TILE = 512
ATT_BLOCK_CHUNKS = 2
ML_PROMPT_CHUNK = 256
SAMPLE_SEQS_PER_STEP = 8


def kernel(x_prompt, x_sample, cache_pool, cache_k, cache_v, state_C, state_n, state_m, norm_pre, norm_post,
           w_in_even, w_pool_mix, pool_scale, rel_bias, w_out_even, w_in_odd, b_gate_odd, mlstm_norm, w_out_odd):
    depth = norm_pre.shape[0]
    bp, dec_seq = x_prompt.shape[0], x_sample.shape[1]
    bs = x_sample.shape[0]
    nh = ML_HEADS
    xp, xs = x_prompt, x_sample
    outs = {name: [] for name in ("pool_p", "k_p", "v_p", "C_p", "n_p", "m_p",
                                  "pool_s", "k_s", "v_s", "C_s", "n_s", "m_s")}
    for layer in range(depth):
        gpre = norm_pre[layer][None, :]
        gpost = norm_post[layer][None, :]
        if layer % 2 == 0:
            e = layer // 2
            w_in = _cast_range_bf16(w_in_even[e], 0, EVEN_IN, axis=1, block=CAST_BLOCK)
            w_mix = w_pool_mix[e].astype(BF16)
            w_out = w_out_even[e].astype(BF16)
            scale = pool_scale[e][None, :]
            bias_p = _expand_bias(rel_bias[e], CHUNK, ATT_BLOCK_CHUNKS)
            assert dec_seq <= CHUNK
            rb_p = ATT_BLOCK_CHUNKS * CHUNK
            bias_s = jnp.concatenate([bias_p[:, 0:dec_seq, 0:BAND + dec_seq],
                                      bias_p[:, rb_p:rb_p + dec_seq, 0:BAND + dec_seq]], axis=1)
            wt_next = jnp.swapaxes(w_in_odd[(layer + 1) // 2], 0, 1) if layer + 1 < depth else None
            res = _even_layer(xp, None, gpre, gpost, w_in, w_mix, scale, bias_p, w_out,
                              tt=PROMPT_TILE, rb=ATT_BLOCK_CHUNKS * CHUNK, ns=1, pos0=0,
                              cast_along=None if wt_next is None else (wt_next, 5 * ML_WIDTH))
            xp, pp, kp, vp = res[0:4]
            w_main_next = res[4] if wt_next is not None else None
            hist = (jnp.pad(cache_pool[e], ((0, 0), (POOL_HIST_ROWS - POOL_HIST, 0), (0, 0))),
                    cache_k[e].reshape(bs, BAND, ATT_WIDTH).astype(BF16),
                    cache_v[e].reshape(bs, BAND, ATT_WIDTH).astype(BF16))
            xs, ps, ks, vs = _even_layer(xs, hist, gpre, gpost, w_in, w_mix, scale, bias_s, w_out,
                                         tt=dec_seq, rb=dec_seq, ns=SAMPLE_SEQS_PER_STEP, pos0=PAST_LEN)
            outs["pool_p"].append(pp[:, POOL_HIST_ROWS - POOL_HIST:])
            outs["k_p"].append(kp.reshape(bp, -1, ATT_HEADS, ATT_HEAD_DIM))
            outs["v_p"].append(vp.reshape(bp, -1, ATT_HEADS, ATT_HEAD_DIM))
            outs["pool_s"].append(ps[:, POOL_HIST_ROWS - POOL_HIST:])
            outs["k_s"].append(ks.reshape(bs, -1, ATT_HEADS, ATT_HEAD_DIM))
            outs["v_s"].append(vs.reshape(bs, -1, ATT_HEADS, ATT_HEAD_DIM))
        else:
            o = layer // 2
            wt = jnp.swapaxes(w_in_odd[o], 0, 1)
            w_main = w_main_next
            w_gates = _cast_range_bf16(wt, 5 * ML_WIDTH, 2 * nh, axis=0, block=2 * nh)
            pad_h = ((0, LANES - nh), (0, 0))
            w_g = jnp.concatenate([jnp.pad(w_gates[:nh], pad_h), jnp.pad(w_gates[nh:], pad_h)], axis=0)
            b_g = jnp.concatenate([jnp.pad(b_gate_odd[o][:nh], (0, LANES - nh)),
                                   jnp.pad(b_gate_odd[o][nh:], (0, LANES - nh))])[None, :]
            gain = mlstm_norm[o][None, :]
            w_out = w_out_odd[o].astype(BF16)
            xp, cp, np_, mp = _odd_layer(xp, None, gpre, gpost, w_main, w_g, b_g, gain, w_out,
                                         tt=PROMPT_TILE, lc=ML_PROMPT_CHUNK, ns=1)
            state = (state_C[o], state_n[o][:, :, None, :],
                     jnp.pad(state_m[o], ((0, 0), (0, LANES - nh)))[:, None, :])
            xs, cs, ns, ms = _odd_layer(xs, state, gpre, gpost, w_main, w_g, b_g, gain, w_out,
                                        tt=dec_seq, lc=dec_seq, ns=SAMPLE_SEQS_PER_STEP)
            outs["C_p"].append(cp)
            outs["n_p"].append(np_[:, :, 0, :])
            outs["m_p"].append(mp[:, 0, :nh])
            outs["C_s"].append(cs)
            outs["n_s"].append(ns[:, :, 0, :])
            outs["m_s"].append(ms[:, 0, :nh])
    return (xp, xs,
            jnp.stack(outs["pool_p"]), jnp.stack(outs["k_p"]), jnp.stack(outs["v_p"]),
            jnp.stack(outs["C_p"]), jnp.stack(outs["n_p"]), jnp.stack(outs["m_p"]),
            jnp.stack(outs["pool_s"]), jnp.stack(outs["k_s"]), jnp.stack(outs["v_s"]),
            jnp.stack(outs["C_s"]), jnp.stack(outs["n_s"]), jnp.stack(outs["m_s"]))
```

```python
import functools

import jax
import jax.numpy as jnp
from jax import lax
from jax.experimental import pallas as pl
from jax.experimental.pallas import tpu as pltpu

D_MODEL = 1024
PAST_LEN = 4096
CHUNK = 64
POOL_WINDOWS = (2, 4, 8, 16)
POOL_GROUPS = len(POOL_WINDOWS)
POOL_WIDTH = D_MODEL // 2
POOL_GROUP_DIM = POOL_WIDTH // POOL_GROUPS
POOL_HIST = max(POOL_WINDOWS) - 1
POOL_HIST_ROWS = 16
ATT_WIDTH = D_MODEL // 2
ATT_HEADS = 8
ATT_HEAD_DIM = ATT_WIDTH // ATT_HEADS
HEAD_PAIRS = ATT_HEADS // 2
BAND = 8 * CHUNK
REL_CLIP = 128
REL_TABLE = 2 * REL_CLIP + 1
REL_TABLE_PAD = 384
ML_WIDTH = D_MODEL
ML_HEADS = 4
ML_HEAD_DIM = ML_WIDTH // ML_HEADS
EVEN_IN = 2 * POOL_WIDTH + 4 * ATT_WIDTH
RMS_EPS = 1e-6
LN_EPS = 1e-6
LOG2_E = 1.4426950408889634

LANES = 128
SUBLANES = 8
BF16_ROWS_PER_VREG = 2 * SUBLANES
BIAS_ROW_UNROLL = 8
SCORE_SLOTS = 3
CAST_BLOCK = 1024
V7X_VMEM_BYTES = 64 * 1024 * 1024
V7X_VMEM_LIMIT_BYTES = V7X_VMEM_BYTES * 7 // 8

BF16 = jnp.bfloat16
F32 = jnp.float32


def _dot(a, b):
    return jnp.dot(a, b, preferred_element_type=F32)


def _dot_nt(a, b):
    return lax.dot_general(a, b, (((1,), (1,)), ((), ())), preferred_element_type=F32)


def _dot_tn(a, b):
    return lax.dot_general(a, b, (((0,), (0,)), ((), ())), preferred_element_type=F32)


def _split3(x):
    hi = x.astype(BF16)
    r1 = x - hi.astype(F32)
    mid = r1.astype(BF16)
    lo = (r1 - mid.astype(F32)).astype(BF16)
    return hi, mid, lo


def _rmsnorm(x, g):
    return x * lax.rsqrt(jnp.mean(x * x, axis=-1, keepdims=True) + RMS_EPS) * g


def _silu(x):
    return x * jax.nn.sigmoid(x)


def _cast_kernel(src_ref, dst_ref):
    dst_ref[...] = src_ref[...].astype(dst_ref.dtype)


def _cast_range_bf16(w, start, n, axis, block):
    assert n % block == 0 and start % block == 0 and start + n <= w.shape[axis]
    first = start // block
    if axis == 0:
        in_spec = pl.BlockSpec((block, w.shape[1]), lambda j: (first + j, 0))
        out_spec = pl.BlockSpec((block, w.shape[1]), lambda j: (j, 0))
        shape = (n, w.shape[1])
    else:
        in_spec = pl.BlockSpec((w.shape[0], block), lambda j: (0, first + j))
        out_spec = pl.BlockSpec((w.shape[0], block), lambda j: (0, j))
        shape = (w.shape[0], n)
    return pl.pallas_call(
        _cast_kernel, grid=(n // block,), in_specs=[in_spec], out_specs=out_spec,
        out_shape=jax.ShapeDtypeStruct(shape, BF16), name="cast_bf16",
    )(w)


def _bias_kernel(tab_ref, out_ref, *, lc, nb):
    rb = nb * lc
    kb = BAND + rb
    n_dist = rb - 1 + kb
    gw = (n_dist + LANES - 1) // LANES * LANES
    parts = _split3(tab_ref[...])
    dist = lax.broadcasted_iota(jnp.int32, (REL_TABLE_PAD, gw), 1) - (rb - 1)
    ti = lax.broadcasted_iota(jnp.int32, (REL_TABLE_PAD, gw), 0)
    onehot = jnp.where(ti == jnp.clip(BAND - dist, -REL_CLIP, REL_CLIP) + REL_CLIP, 1.0, 0.0).astype(BF16)
    by_dist = _dot(parts[0], onehot) + _dot(parts[1], onehot) + _dot(parts[2], onehot)
    key_chunk = lax.broadcasted_iota(jnp.int32, (ATT_HEADS, kb), 1) // lc

    def row(qi, carry):
        shift = lax.rem(gw - (rb - 1 - qi), gw)
        acc = pltpu.roll(by_dist, shift, axis=1)[:, 0:kb]
        q_chunk = qi // lc
        in_band = (key_chunk >= q_chunk) & (key_chunk <= q_chunk + BAND // lc)
        out_ref[qi] = jnp.where(in_band, acc * LOG2_E, -jnp.inf)
        return carry

    lax.fori_loop(0, rb, row, 0, unroll=BIAS_ROW_UNROLL)


def _expand_bias(table, lc, nb):
    rb = nb * lc
    kb = BAND + rb
    tab = jnp.pad(table.astype(F32), ((0, 0), (0, REL_TABLE_PAD - REL_TABLE)))
    out = pl.pallas_call(
        functools.partial(_bias_kernel, lc=lc, nb=nb),
        out_shape=jax.ShapeDtypeStruct((rb, ATT_HEADS, kb), F32),
        name="rel_bias_expand",
    )(tab)
    return out.transpose(1, 0, 2).reshape(HEAD_PAIRS, 2 * rb, kb)


def _even_kernel(*refs, tt, rb, ns, has_hist, cast_along, pos0):
    refs = list(refs)
    x_ref = refs.pop(0)
    if has_hist:
        hu_ref, hk_ref, hv_ref = refs[0:3]
        refs = refs[3:]
    gpre_ref, gpost_ref, win_ref, wmix_ref, scale_ref, bias_ref, wout_ref = refs[0:7]
    refs = refs[7:]
    if cast_along:
        wsrc_ref = refs.pop(0)
        y_ref, pool_ref, kout_ref, vout_ref, wdst_ref = refs[0:5]
        refs = refs[5:]
        wdst_ref[...] = wsrc_ref[...].astype(BF16)
    else:
        y_ref, pool_ref, kout_ref, vout_ref = refs[0:4]
        refs = refs[4:]
    uext, kext, vext, q_sc, gate_sc, mixed_sc, s_sc = refs
    t = pl.program_id(1)
    hr = POOL_HIST_ROWS
    kb = BAND + rb
    seqs = range(ns)

    def seq_rows(s):
        return slice(s * tt, (s + 1) * tt)

    @pl.when(t == 0)
    def _():
        for s in seqs:
            if has_hist:
                uext[s, 0:hr, :] = hu_ref[s]
                kext[s, 0:BAND, :] = hk_ref[s]
                vext[s, 0:BAND, :] = hv_ref[s]
            else:
                uext[s, 0:hr, :] = jnp.zeros((hr, POOL_WIDTH), F32)
                kext[s, 0:BAND, :] = jnp.zeros((BAND, ATT_WIDTH), BF16)
                vext[s, 0:BAND, :] = jnp.zeros((BAND, ATT_WIDTH), BF16)

    @pl.when(t > 0)
    def _():
        for s in seqs:
            uext[s, 0:hr, :] = uext[s, tt:tt + hr, :]
            kext[s, 0:BAND, :] = kext[s, tt:tt + BAND, :]
            vext[s, 0:BAND, :] = vext[s, tt:tt + BAND, :]

    def load_x():
        return x_ref[0] if ns == 1 else jnp.concatenate([x_ref[s] for s in seqs], axis=0)

    h = _rmsnorm(load_x(), gpre_ref[...]).astype(BF16)
    p_, a_ = POOL_WIDTH, ATT_WIDTH

    def project(part):
        cols = slice(part * a_, (part + 1) * a_)
        res = _dot(h, win_ref[:, cols])
        if part == 0:
            for s in seqs:
                uext[s, hr:hr + tt, :] = res[seq_rows(s)]
        elif part == 1:
            q_sc[...] = (res * (ATT_HEAD_DIM ** -0.5 * LOG2_E)).astype(BF16)
        elif part == 2:
            for s in seqs:
                kext[s, BAND:BAND + tt, :] = res[seq_rows(s)].astype(BF16)
                kout_ref[s] = res[seq_rows(s)]
        elif part == 3:
            for s in seqs:
                vext[s, BAND:BAND + tt, :] = res[seq_rows(s)].astype(BF16)
                vout_ref[s] = res[seq_rows(s)]
        else:
            gate_sc[:, (part - 4) * a_:(part - 3) * a_] = _silu(res)

    pos = pos0 + t * tt + lax.broadcasted_iota(jnp.int32, (tt, 1), 0)

    def pool_group(g):
        w = POOL_WINDOWS[g]
        ln = slice(g * POOL_GROUP_DIM, (g + 1) * POOL_GROUP_DIM)
        count = jnp.minimum(pos + 1, w).astype(F32)
        pooled = []
        for s in seqs:
            tok = uext[s, hr:hr + tt, ln]
            win_sum = tok
            for back in range(1, w):
                win_sum = win_sum + uext[s, hr - back:hr - back + tt, ln]
            pooled.append(win_sum / count - tok)
        pooled = pooled[0] if ns == 1 else jnp.concatenate(pooled, axis=0)
        mixed = _dot(pooled.astype(BF16), wmix_ref[g]) * scale_ref[:, ln]
        mixed_sc[:, ln] = (mixed * gate_sc[:, ln]).astype(BF16)

    assert p_ == a_ and POOL_GROUPS == 4
    project(0)
    project(4)
    for g, part in enumerate((1, 2, 3, 5)):
        project(part)
        pool_group(g)

    lane = lax.broadcasted_iota(jnp.int32, (rb, LANES), 1)
    even_head = lane < ATT_HEAD_DIM

    def attend(sequence_start):
        units = [(s, blk, p) for s in seqs for blk in range(tt // rb) for p in range(HEAD_PAIRS)]

        def band(blk):
            r0 = blk * rb
            k0 = BAND if sequence_start else r0
            c0 = BAND - r0 if sequence_start else 0
            return r0, k0, c0, kb - c0

        def stage_scores(i):
            s, blk, p = units[i]
            r0, k0, c0, kw = band(blk)
            ln = slice(p * LANES, (p + 1) * LANES)
            qp = q_sc[s * tt + r0:s * tt + r0 + rb, ln]
            zero = jnp.zeros_like(qp)
            q2 = jnp.concatenate([jnp.where(even_head, qp, zero), jnp.where(even_head, zero, qp)], axis=0)
            s_sc[i % SCORE_SLOTS, :, 0:kw] = _dot_nt(q2, kext[s, k0:k0 + kw, ln]) + bias_ref[p, :, c0:c0 + kw]

        def finish(i):
            s, blk, p = units[i]
            r0, k0, c0, kw = band(blk)
            ln = slice(p * LANES, (p + 1) * LANES)
            halves = []
            for half in range(2):
                sh = s_sc[i % SCORE_SLOTS, half * rb:(half + 1) * rb, 0:kw]
                e = jnp.exp2(sh - jnp.max(sh, axis=-1, keepdims=True))
                denom = jnp.sum(e, axis=-1, keepdims=True)
                halves.append(_dot(e.astype(BF16), vext[s, k0:k0 + kw, ln]) / denom)
            o = jnp.where(even_head, halves[0], halves[1])
            rows = slice(s * tt + r0, s * tt + r0 + rb)
            mo = slice(POOL_WIDTH + p * LANES, POOL_WIDTH + (p + 1) * LANES)
            mixed_sc[rows, mo] = (o * gate_sc[rows, mo]).astype(BF16)

        ahead = SCORE_SLOTS - 1
        for i in range(min(ahead, len(units))):
            stage_scores(i)
        for i in range(len(units)):
            if i + ahead < len(units):
                stage_scores(i + ahead)
            finish(i)

    if has_hist:
        attend(False)
    else:
        pl.when(t == 0)(functools.partial(attend, True))
        pl.when(t > 0)(functools.partial(attend, False))

    y = load_x() + _rmsnorm(_dot(mixed_sc[...], wout_ref[...]), gpost_ref[...])
    for s in seqs:
        y_ref[s] = y[seq_rows(s)]
        pool_ref[s] = uext[s, tt:tt + hr, :]


def _even_layer(x, hist, gpre, gpost, w_in, w_mix, scale, bias, w_out, *, tt, rb, ns, pos0, cast_along=None):
    b, t_len, d = x.shape
    has_hist = hist is not None
    nt = t_len // tt
    assert t_len % tt == 0 and tt % rb == 0 and b % ns == 0
    assert has_hist or rb % LANES == 0, "band start at a sequence start must stay lane aligned in the bias"
    keep = min(BAND, t_len)
    assert keep == tt, "key/value cache rows must be exactly the last time tile"
    assert nt == 1 or tt >= BAND
    kb = BAND + rb

    def whole(shape):
        return pl.BlockSpec(shape, lambda i, j: (0,) * len(shape))

    def per_seq(shape):
        return pl.BlockSpec((ns,) + shape, lambda i, j: (i, 0, 0))

    in_specs = [pl.BlockSpec((ns, tt, d), lambda i, j: (i, j, 0))]
    args = [x]
    if has_hist:
        in_specs += [per_seq((POOL_HIST_ROWS, POOL_WIDTH)), per_seq((BAND, ATT_WIDTH)), per_seq((BAND, ATT_WIDTH))]
        args += list(hist)
    in_specs += [whole((1, d)), whole((1, d)), whole((d, EVEN_IN)),
                 whole((POOL_GROUPS, POOL_GROUP_DIM, POOL_GROUP_DIM)), whole((1, POOL_WIDTH)),
                 whole((HEAD_PAIRS, 2 * rb, kb)), whole((POOL_WIDTH + ATT_WIDTH, d))]
    args += [gpre, gpost, w_in, w_mix, scale, bias, w_out]
    out_shape = [jax.ShapeDtypeStruct((b, t_len, d), F32),
                 jax.ShapeDtypeStruct((b, POOL_HIST_ROWS, POOL_WIDTH), F32),
                 jax.ShapeDtypeStruct((b, keep, ATT_WIDTH), F32),
                 jax.ShapeDtypeStruct((b, keep, ATT_WIDTH), F32)]
    out_specs = [pl.BlockSpec((ns, tt, d), lambda i, j: (i, j, 0)),
                 per_seq((POOL_HIST_ROWS, POOL_WIDTH)), per_seq((keep, ATT_WIDTH)), per_seq((keep, ATT_WIDTH))]
    if cast_along is not None:
        src, n_rows = cast_along
        steps = (b // ns) * nt
        share = n_rows // steps
        assert n_rows % steps == 0 and share % BF16_ROWS_PER_VREG == 0 and n_rows <= src.shape[0]
        share_spec = pl.BlockSpec((share, src.shape[1]), lambda i, j: (i * nt + j, 0))
        in_specs.append(share_spec)
        args.append(src)
        out_specs.append(share_spec)
        out_shape.append(jax.ShapeDtypeStruct((n_rows, src.shape[1]), BF16))
    scratch = [pltpu.VMEM((ns, POOL_HIST_ROWS + tt, POOL_WIDTH), F32),
               pltpu.VMEM((ns, BAND + tt, ATT_WIDTH), BF16),
               pltpu.VMEM((ns, BAND + tt, ATT_WIDTH), BF16),
               pltpu.VMEM((ns * tt, ATT_WIDTH), BF16),
               pltpu.VMEM((ns * tt, POOL_WIDTH + ATT_WIDTH), F32),
               pltpu.VMEM((ns * tt, POOL_WIDTH + ATT_WIDTH), BF16),
               pltpu.VMEM((SCORE_SLOTS, 2 * rb, kb), F32)]
    return pl.pallas_call(
        functools.partial(_even_kernel, tt=tt, rb=rb, ns=ns, has_hist=has_hist,
                          cast_along=cast_along is not None, pos0=pos0),
        grid=(b // ns, nt), in_specs=in_specs, out_specs=tuple(out_specs), out_shape=tuple(out_shape),
        scratch_shapes=scratch,
        compiler_params=pltpu.CompilerParams(dimension_semantics=("parallel", "arbitrary"),
                                             vmem_limit_bytes=V7X_VMEM_LIMIT_BYTES),
        name="even_layer_hist" if has_hist else "even_layer",
    )(*args)


def _odd_kernel(*refs, tt, lc, ns, has_state):
    if has_state:
        x_ref, c0_ref, n0_ref, m0_ref = refs[0:4]
        refs = refs[4:]
    else:
        x_ref = refs[0]
        refs = refs[1:]
    (gpre_ref, gpost_ref, w_ref, wg_ref, bg_ref, gain_ref, wout_ref,
     y_ref, cout_ref, nout_ref, mout_ref, q_sc, k_sc, v_sc, oz_sc, mixed_sc, c_sc, n_sc, m_sc) = refs
    t = pl.program_id(1)
    w_, dh, nh = ML_WIDTH, ML_HEAD_DIM, ML_HEADS
    assert nh <= SUBLANES, "the per-head gate lanes are moved to sublanes eight at a time"
    if ns == 1:
        groups = [[(0, c * lc)] for c in range(tt // lc)]
    else:
        assert tt == lc
        groups = [[(s, s * tt) for s in range(ns)]]
    n_groups = len(groups)
    g_rows = lc * len(groups[0])

    def group_rows(g):
        return slice(g * g_rows, (g + 1) * g_rows)

    @pl.when(t == 0)
    def _():
        if has_state:
            c_sc[...] = c0_ref[...]
            n_sc[...] = n0_ref[...]
            m_sc[...] = m0_ref[...]
        else:
            c_sc[...] = jnp.zeros(c_sc.shape, F32)
            n_sc[...] = jnp.zeros(n_sc.shape, F32)
            m_sc[...] = jnp.zeros(m_sc.shape, F32)

    def x_rows(g):
        if ns == 1:
            return x_ref[0, group_rows(g), :]
        return jnp.concatenate([x_ref[s] for s in range(ns)], axis=0)

    h = [_rmsnorm(x_rows(g), gpre_ref[...]).astype(BF16) for g in range(n_groups)]

    def project(g, part):
        rows = group_rows(g)
        if part == 0:
            q_sc[rows, :] = _dot_nt(h[g], w_ref[0:w_, :]).astype(BF16)
        elif part == 1:
            k_sc[rows, :] = (_dot_nt(h[g], w_ref[w_:2 * w_, :]) * (dh ** -0.5)).astype(BF16)
        elif part == 2:
            v_sc[rows, :] = _dot_nt(h[g], w_ref[2 * w_:3 * w_, :]).astype(BF16)
        else:
            oz_sc[rows, :] = (jax.nn.sigmoid(_dot_nt(h[g], w_ref[3 * w_:4 * w_, :]))
                              * _silu(_dot_nt(h[g], w_ref[4 * w_:5 * w_, :])))

    tri = jnp.where(lax.broadcasted_iota(jnp.int32, (lc, lc), 1) <= lax.broadcasted_iota(jnp.int32, (lc, lc), 0),
                    1.0, 0.0).astype(BF16)
    sel = jnp.where(lax.broadcasted_iota(jnp.int32, (SUBLANES, LANES), 0)
                    == lax.broadcasted_iota(jnp.int32, (SUBLANES, LANES), 1), 1.0, 0.0).astype(BF16)
    causal = (lax.broadcasted_iota(jnp.int32, (lc, lc), 1) <= lax.broadcasted_iota(jnp.int32, (lc, lc), 0))

    slabs = [(0, part) for part in range(4)]
    m_prev = [m_sc[s] for s in range(ns)]
    seg_gates = {}
    for g, segments in enumerate(groups):
        gates = _dot_nt(h[g], wg_ref[...]) + bg_ref[...]
        if slabs:
            project(*slabs.pop(0))
        for s, r0 in segments:
            lo = r0 - g * g_rows
            ig = gates[lo:lo + lc, 0:LANES]
            lf3 = _split3(jax.nn.log_sigmoid(gates[lo:lo + lc, LANES:2 * LANES]))
            b_all = _dot(tri, lf3[0]) + _dot(tri, lf3[1]) + _dot(tri, lf3[2])
            if slabs and (s, r0) == segments[0]:
                project(*slabs.pop(0))
            b3 = _split3(b_all)
            g3 = _split3(ig)
            brow = _dot_nt(sel, b3[0]) + _dot_nt(sel, b3[1]) + _dot_nt(sel, b3[2])
            igrow = _dot_nt(sel, g3[0]) + _dot_nt(sel, g3[1]) + _dot_nt(sel, g3[2])
            b_last_all = b_all[lc - 1:lc]
            g_all = b_last_all - b_all + ig
            m_new = jnp.maximum(b_last_all + m_prev[s], jnp.max(g_all, axis=0, keepdims=True))
            seg_gates[(s, r0)] = dict(b=b_all, inter=b_all + m_prev[s], brow=brow, igrow=igrow,
                                      decay=jnp.exp(b_last_all + m_prev[s] - m_new),
                                      wgt=jnp.exp(g_all - m_new))
            m_prev[s] = m_new
    for s in range(ns):
        m_sc[s] = m_prev[s]
    while slabs:
        project(*slabs.pop(0))

    def unit(seg, hd):
        s, r0 = seg
        rows = slice(r0, r0 + lc)
        ln = slice(hd * dh, (hd + 1) * dh)
        gts = seg_gates[seg]
        b_c = gts["b"][:, hd:hd + 1]
        b_r = gts["brow"][hd:hd + 1, :]
        ig_r = gts["igrow"][hd:hd + 1, :]
        inter = gts["inter"][:, hd:hd + 1]
        dmat = jnp.where(causal, b_c - b_r + ig_r, -jnp.inf)
        m_t = jnp.maximum(inter, jnp.max(dmat, axis=-1, keepdims=True))
        a = jnp.exp(inter - m_t)
        q = q_sc[rows, ln]
        k = k_sc[rows, ln]
        v = v_sc[rows, ln]
        sc = _dot_nt(q, k) * jnp.exp(dmat - m_t)
        c_old = c_sc[s, hd]
        n_old = n_sc[s, hd]
        num = a * _dot_nt(q, c_old.astype(BF16)) + _dot(sc.astype(BF16), v)
        qn = jnp.sum(q.astype(F32) * n_old, axis=-1, keepdims=True)
        den = a * qn + jnp.sum(sc, axis=-1, keepdims=True)
        hv = num / jnp.maximum(jnp.abs(den), jnp.exp(-m_t))
        mu = jnp.mean(hv, axis=-1, keepdims=True)
        dlt = hv - mu
        var = jnp.mean(dlt * dlt, axis=-1, keepdims=True)
        hn = dlt * lax.rsqrt(var + LN_EPS) * gain_ref[:, ln]
        mixed_sc[rows, ln] = (hn * oz_sc[rows, ln]).astype(BF16)
        decay = gts["decay"][:, hd:hd + 1]
        wgt = gts["wgt"][:, hd:hd + 1]
        vw = (v.astype(F32) * wgt).astype(BF16)
        c_sc[s, hd] = decay * c_old + _dot_tn(vw, k)
        n_sc[s, hd] = decay * n_old + jnp.sum(wgt * k.astype(F32), axis=0, keepdims=True)

    def out_project(g):
        res = x_rows(g) + _rmsnorm(_dot(mixed_sc[group_rows(g), :], wout_ref[...]), gpost_ref[...])
        if ns == 1:
            y_ref[0, group_rows(g), :] = res
        else:
            for s in range(ns):
                y_ref[s] = res[s * tt:(s + 1) * tt]

    n_parts = 4
    for g, segments in enumerate(groups):
        for i, (seg, hd) in enumerate((seg, hd) for seg in segments for hd in range(nh)):
            unit(seg, hd)
            if g + 1 < n_groups and i < n_parts:
                project(g + 1, i)
    for g in range(n_groups):
        out_project(g)
    cout_ref[...] = c_sc[...]
    nout_ref[...] = n_sc[...]
    mout_ref[...] = m_sc[...]


def _odd_layer(x, state, gpre, gpost, w_main, w_g, b_g, gain, w_out, *, tt, lc, ns):
    b, t_len, d = x.shape
    nt = t_len // tt
    assert t_len % tt == 0 and tt % lc == 0 and b % ns == 0
    nh, dh = ML_HEADS, ML_HEAD_DIM
    has_state = state is not None

    def whole(shape):
        return pl.BlockSpec(shape, lambda i, j: (0,) * len(shape))

    def per_seq(shape):
        return pl.BlockSpec((ns,) + shape, lambda i, j: (i,) + (0,) * len(shape))

    st_specs = [per_seq((nh, dh, dh)), per_seq((nh, 1, dh)), per_seq((1, LANES))]
    in_specs = [pl.BlockSpec((ns, tt, d), lambda i, j: (i, j, 0))]
    args = [x]
    if has_state:
        in_specs += st_specs
        args += list(state)
    in_specs += [whole((1, d)), whole((1, d)), whole((5 * ML_WIDTH, d)),
                 whole((2 * LANES, d)), whole((1, 2 * LANES)), whole((1, ML_WIDTH)), whole((ML_WIDTH, d))]
    args += [gpre, gpost, w_main, w_g, b_g, gain, w_out]
    out_shape = (jax.ShapeDtypeStruct((b, t_len, d), F32),
                 jax.ShapeDtypeStruct((b, nh, dh, dh), F32),
                 jax.ShapeDtypeStruct((b, nh, 1, dh), F32),
                 jax.ShapeDtypeStruct((b, 1, LANES), F32))
    out_specs = (pl.BlockSpec((ns, tt, d), lambda i, j: (i, j, 0)), *st_specs)
    rows = ns * tt
    scratch = [pltpu.VMEM((rows, ML_WIDTH), BF16), pltpu.VMEM((rows, ML_WIDTH), BF16),
               pltpu.VMEM((rows, ML_WIDTH), BF16), pltpu.VMEM((rows, ML_WIDTH), F32),
               pltpu.VMEM((rows, ML_WIDTH), BF16),
               pltpu.VMEM((ns, nh, dh, dh), F32), pltpu.VMEM((ns, nh, 1, dh), F32),
               pltpu.VMEM((ns, 1, LANES), F32)]
    return pl.pallas_call(
        functools.partial(_odd_kernel, tt=tt, lc=lc, ns=ns, has_state=has_state),
        grid=(b // ns, nt), in_specs=in_specs, out_specs=out_specs, out_shape=out_shape,
        scratch_shapes=scratch,
        compiler_params=pltpu.CompilerParams(dimension_semantics=("parallel", "arbitrary"),
                                             vmem_limit_bytes=V7X_VMEM_LIMIT_BYTES),
        name="odd_layer_state" if has_state else "odd_layer",
    )(*args)


PROMPT_TILE = 512
ATT_BLOCK_CHUNKS = 2
ML_PROMPT_CHUNK = 256
SAMPLE_SEQS_PER_STEP = 8


def kernel(x_prompt, x_sample, cache_pool, cache_k, cache_v, state_C, state_n, state_m, norm_pre, norm_post,
           w_in_even, w_pool_mix, pool_scale, rel_bias, w_out_even, w_in_odd, b_gate_odd, mlstm_norm, w_out_odd):
    depth = norm_pre.shape[0]
    bp, dec_seq = x_prompt.shape[0], x_sample.shape[1]
    bs = x_sample.shape[0]
    nh = ML_HEADS
    xp, xs = x_prompt, x_sample
    outs = {name: [] for name in ("pool_p", "k_p", "v_p", "C_p", "n_p", "m_p",
                                  "pool_s", "k_s", "v_s", "C_s", "n_s", "m_s")}
    for layer in range(depth):
        gpre = norm_pre[layer][None, :]
        gpost = norm_post[layer][None, :]
        if layer % 2 == 0:
            e = layer // 2
            w_in = _cast_range_bf16(w_in_even[e], 0, EVEN_IN, axis=1, block=CAST_BLOCK)
            w_mix = w_pool_mix[e].astype(BF16)
            w_out = w_out_even[e].astype(BF16)
            scale = pool_scale[e][None, :]
            bias_p = _expand_bias(rel_bias[e], CHUNK, ATT_BLOCK_CHUNKS)
            assert dec_seq <= CHUNK
            rb_p = ATT_BLOCK_CHUNKS * CHUNK
            bias_s = jnp.concatenate([bias_p[:, 0:dec_seq, 0:BAND + dec_seq],
                                      bias_p[:, rb_p:rb_p + dec_seq, 0:BAND + dec_seq]], axis=1)
            wt_next = jnp.swapaxes(w_in_odd[(layer + 1) // 2], 0, 1) if layer + 1 < depth else None
            res = _even_layer(xp, None, gpre, gpost, w_in, w_mix, scale, bias_p, w_out,
                              tt=PROMPT_TILE, rb=ATT_BLOCK_CHUNKS * CHUNK, ns=1, pos0=0,
                              cast_along=None if wt_next is None else (wt_next, 5 * ML_WIDTH))
            xp, pp, kp, vp = res[0:4]
            w_main_next = res[4] if wt_next is not None else None
            hist = (jnp.pad(cache_pool[e], ((0, 0), (POOL_HIST_ROWS - POOL_HIST, 0), (0, 0))),
                    cache_k[e].reshape(bs, BAND, ATT_WIDTH).astype(BF16),
                    cache_v[e].reshape(bs, BAND, ATT_WIDTH).astype(BF16))
            xs, ps, ks, vs = _even_layer(xs, hist, gpre, gpost, w_in, w_mix, scale, bias_s, w_out,
                                         tt=dec_seq, rb=dec_seq, ns=SAMPLE_SEQS_PER_STEP, pos0=PAST_LEN)
            outs["pool_p"].append(pp[:, POOL_HIST_ROWS - POOL_HIST:])
            outs["k_p"].append(kp.reshape(bp, -1, ATT_HEADS, ATT_HEAD_DIM))
            outs["v_p"].append(vp.reshape(bp, -1, ATT_HEADS, ATT_HEAD_DIM))
            outs["pool_s"].append(ps[:, POOL_HIST_ROWS - POOL_HIST:])
            outs["k_s"].append(ks.reshape(bs, -1, ATT_HEADS, ATT_HEAD_DIM))
            outs["v_s"].append(vs.reshape(bs, -1, ATT_HEADS, ATT_HEAD_DIM))
        else:
            o = layer // 2
            wt = jnp.swapaxes(w_in_odd[o], 0, 1)
            w_main = w_main_next
            w_gates = _cast_range_bf16(wt, 5 * ML_WIDTH, 2 * nh, axis=0, block=2 * nh)
            pad_h = ((0, LANES - nh), (0, 0))
            w_g = jnp.concatenate([jnp.pad(w_gates[:nh], pad_h), jnp.pad(w_gates[nh:], pad_h)], axis=0)
            b_g = jnp.concatenate([jnp.pad(b_gate_odd[o][:nh], (0, LANES - nh)),
                                   jnp.pad(b_gate_odd[o][nh:], (0, LANES - nh))])[None, :]
            gain = mlstm_norm[o][None, :]
            w_out = w_out_odd[o].astype(BF16)
            xp, cp, np_, mp = _odd_layer(xp, None, gpre, gpost, w_main, w_g, b_g, gain, w_out,
                                         tt=PROMPT_TILE, lc=ML_PROMPT_CHUNK, ns=1)
            state = (state_C[o], state_n[o][:, :, None, :],
                     jnp.pad(state_m[o], ((0, 0), (0, LANES - nh)))[:, None, :])
            xs, cs, ns, ms = _odd_layer(xs, state, gpre, gpost, w_main, w_g, b_g, gain, w_out,
                                        tt=dec_seq, lc=dec_seq, ns=SAMPLE_SEQS_PER_STEP)
            outs["C_p"].append(cp)
            outs["n_p"].append(np_[:, :, 0, :])
            outs["m_p"].append(mp[:, 0, :nh])
            outs["C_s"].append(cs)
            outs["n_s"].append(ns[:, :, 0, :])
            outs["m_s"].append(ms[:, 0, :nh])
    return (xp, xs,
            jnp.stack(outs["pool_p"]), jnp.stack(outs["k_p"]), jnp.stack(outs["v_p"]),
            jnp.stack(outs["C_p"]), jnp.stack(outs["n_p"]), jnp.stack(outs["m_p"]),
            jnp.stack(outs["pool_s"]), jnp.stack(outs["k_s"]), jnp.stack(outs["v_s"]),
            jnp.stack(outs["C_s"]), jnp.stack(outs["n_s"]), jnp.stack(outs["m_s"]))
```
